```python
import jax
import jax.numpy as jnp
from jax import lax
import numpy as np


D_MODEL = 1024
BATCH = 8
SEQ = 4096
DEPTH = 2

GRID_W = 64
CTX_LEN = 256
HEAD_DIM = 64
FOURIER_GROUPS = 4
FOURIER_GROUP_DIM = 64
FOURIER_WIDTH = FOURIER_GROUPS * FOURIER_GROUP_DIM
SWA_HEADS = 6
SWA_KV_HEADS = 2
SWA_Q_WIDTH = SWA_HEADS * HEAD_DIM
SWA_KV_WIDTH = SWA_KV_HEADS * HEAD_DIM
WINDOW = 128
BLOCK = 128
MLA_HEADS = 6
MLA_NOPE_DIM = 64
MLA_ROPE_DIM = 32
MLA_V_DIM = 64
MLA_Q_RANK = 256
MLA_KV_RANK = 128
MLA_SCALE = (MLA_NOPE_DIM + MLA_ROPE_DIM) ** -0.5
D_MIX = FOURIER_WIDTH + SWA_Q_WIDTH + MLA_HEADS * MLA_V_DIM
OFF_SWA_Q = FOURIER_WIDTH
OFF_SWA_K = OFF_SWA_Q + SWA_Q_WIDTH
OFF_SWA_V = OFF_SWA_K + SWA_KV_WIDTH
OFF_MLA_CQ = OFF_SWA_V + SWA_KV_WIDTH
OFF_MLA_CKV = OFF_MLA_CQ + MLA_Q_RANK
OFF_MLA_KR = OFF_MLA_CKV + MLA_KV_RANK
D_IN = OFF_MLA_KR + MLA_ROPE_DIM
IN_SPLITS = (OFF_SWA_Q, OFF_SWA_K, OFF_SWA_V, OFF_MLA_CQ, OFF_MLA_CKV, OFF_MLA_KR)
D_FF = 4 * D_MODEL
ROPE_THETA = 10000.0
NORM_EPS = 1e-6
NEG_INF = -1e30

kernel_name = 'hybrid_dit_fourier_swa_mla'


def rms_norm(x, g):
    xf = x.astype(jnp.float32)
    y = xf * lax.rsqrt(jnp.mean(xf * xf, axis=-1, keepdims=True) + NORM_EPS)
    return (y * g.astype(jnp.float32)).astype(x.dtype)


def modulate(h, shift, scale):
    return h * (1 + scale) + shift


def axial_rope_tables(rows, dim):
    r, col = jnp.meshgrid(jnp.arange(rows, dtype=jnp.float32), jnp.arange(GRID_W, dtype=jnp.float32), indexing='ij')
    r = r.reshape(-1)
    col = col.reshape(-1)
    n_freq = dim // 4
    inv_freq = ROPE_THETA ** (-jnp.arange(n_freq, dtype=jnp.float32) / n_freq)
    ang = jnp.concatenate([r[:, None] * inv_freq[None, :], col[:, None] * inv_freq[None, :]], axis=-1)
    return jnp.cos(ang), jnp.sin(ang)


def apply_rope(x, cos, sin):
    half = x.shape[-1] // 2
    xf = x.astype(jnp.float32)
    x1, x2 = xf[..., :half], xf[..., half:]
    c, s = cos[:, None, :], sin[:, None, :]
    return jnp.concatenate([x1 * c - x2 * s, x1 * s + x2 * c], axis=-1).astype(x.dtype)


def fourier_mix(f, w_f):
    B_, S_, _ = f.shape
    fg = f.reshape(B_, S_, FOURIER_GROUPS, FOURIER_GROUP_DIM).astype(jnp.float32)
    spec = jnp.fft.fft2(fg, axes=(1, 3), norm='ortho').real.astype(f.dtype)
    out = jnp.einsum('bsgc,gcd->bsgd', spec, w_f)
    return out.reshape(B_, S_, FOURIER_WIDTH)


def softmax_with_sink(logits, sink_kg):
    sk = jnp.broadcast_to(sink_kg.astype(jnp.float32)[:, :, None, None], logits.shape[:-1] + (1,))
    p = jax.nn.softmax(jnp.concatenate([logits, sk], axis=-1), axis=-1)
    return p[..., :-1]


def swa_latent(q, k, v, k_ctx, v_ctx, sink):
    B_, S_, H, d = q.shape
    G = H // SWA_KV_HEADS
    nb = S_ // BLOCK
    qb = q.reshape(B_, nb, BLOCK, SWA_KV_HEADS, G, d)
    pad = ((0, 0), (BLOCK, BLOCK), (0, 0), (0, 0))
    kp = jnp.pad(k, pad).reshape(B_, nb + 2, BLOCK, SWA_KV_HEADS, d)
    vp = jnp.pad(v, pad).reshape(B_, nb + 2, BLOCK, SWA_KV_HEADS, d)
    kb = jnp.concatenate([kp[:, :-2], kp[:, 1:-1], kp[:, 2:]], axis=2)
    vb = jnp.concatenate([vp[:, :-2], vp[:, 1:-1], vp[:, 2:]], axis=2)
    scale = d ** -0.5
    s_loc = jnp.einsum('bnqkgd,bnjkd->bnkgqj', qb, kb).astype(jnp.float32) * scale
    s_ctx = jnp.einsum('bnqkgd,bckd->bnkgqc', qb, k_ctx).astype(jnp.float32) * scale
    qpos = jnp.arange(nb)[:, None] * BLOCK + jnp.arange(BLOCK)[None, :]
    kpos = jnp.arange(nb)[:, None] * BLOCK - BLOCK + jnp.arange(3 * BLOCK)[None, :]
    rel = kpos[:, None, :] - qpos[:, :, None]
    valid = (jnp.abs(rel) <= WINDOW) & (kpos[:, None, :] >= 0) & (kpos[:, None, :] < S_)
    s_loc = jnp.where(valid[None, :, None, None, :, :], s_loc, NEG_INF)
    p = softmax_with_sink(jnp.concatenate([s_loc, s_ctx], axis=-1), sink.reshape(SWA_KV_HEADS, G)).astype(v.dtype)
    o = (jnp.einsum('bnkgqj,bnjkd->bnqkgd', p[..., :3 * BLOCK], vb)
         + jnp.einsum('bnkgqc,bckd->bnqkgd', p[..., 3 * BLOCK:], v_ctx))
    return o.reshape(B_, S_, H * d)


def swa_context(q, k, v, sink):
    B_, L, H, d = q.shape
    G = H // SWA_KV_HEADS
    qg = q.reshape(B_, L, SWA_KV_HEADS, G, d)
    s = jnp.einsum('bqkgd,bckd->bkgqc', qg, k).astype(jnp.float32) * (d ** -0.5)
    p = softmax_with_sink(s, sink.reshape(SWA_KV_HEADS, G)).astype(v.dtype)
    o = jnp.einsum('bkgqc,bckd->bqkgd', p, v)
    return o.reshape(B_, L, H * d)


def mla_queries(cq, q_norm, w_uq):
    B_, S_, _ = cq.shape
    q = (rms_norm(cq, q_norm) @ w_uq).reshape(B_, S_, MLA_HEADS, MLA_NOPE_DIM + MLA_ROPE_DIM)
    return q[..., :MLA_NOPE_DIM], q[..., MLA_NOPE_DIM:]


def mla_keys_values(ckv, kv_norm, w_ukv):
    B_, S_, _ = ckv.shape
    kv = (rms_norm(ckv, kv_norm) @ w_ukv).reshape(B_, S_, MLA_HEADS, MLA_NOPE_DIM + MLA_V_DIM)
    return kv[..., :MLA_NOPE_DIM], kv[..., MLA_NOPE_DIM:]


def mla_latent(qn, qr, kn, kr, v, kn_c, kr_c, v_c):
    B_, S_, H, _ = qn.shape
    nb = S_ // BLOCK

    def blocks(t):
        return t.reshape(B_, nb, BLOCK, *t.shape[2:]).swapaxes(0, 1)

    def one_block(qs):
        qn_i, qr_i = qs
        s_lat = jnp.einsum('bqhd,bkhd->bhqk', qn_i, kn) + jnp.einsum('bqhr,bkr->bhqk', qr_i, kr)
        s_ctx = jnp.einsum('bqhd,bchd->bhqc', qn_i, kn_c) + jnp.einsum('bqhr,bcr->bhqc', qr_i, kr_c)
        logits = jnp.concatenate([s_lat, s_ctx], axis=-1).astype(jnp.float32) * MLA_SCALE
        p = jax.nn.softmax(logits, axis=-1).astype(v.dtype)
        return (jnp.einsum('bhqk,bkhd->bqhd', p[..., :S_], v)
                + jnp.einsum('bhqc,bchd->bqhd', p[..., S_:], v_c))

    o = lax.map(one_block, (blocks(qn), blocks(qr)))
    return o.swapaxes(0, 1).reshape(B_, S_, H * MLA_V_DIM)


def mla_context(qn, qr, kn, kr, v):
    B_, L, H, _ = qn.shape
    s = jnp.einsum('bqhd,bkhd->bhqk', qn, kn) + jnp.einsum('bqhr,bkr->bhqk', qr, kr)
    p = jax.nn.softmax(s.astype(jnp.float32) * MLA_SCALE, axis=-1).astype(v.dtype)
    return jnp.einsum('bhqk,bkhd->bqhd', p, v).reshape(B_, L, H * MLA_V_DIM)


def squared_relu_mlp(h, w1, w2):
    return jnp.square(jax.nn.relu(h @ w1)) @ w2


def setup_inputs(seed: int = 0) -> dict:
    key = jax.random.key(seed)
    ks = jax.random.split(key, 20)

    def nrm(k, shape, scale):
        return jax.random.normal(k, shape, jnp.float32) * scale

    return {
        'x': nrm(ks[0], (BATCH, SEQ, D_MODEL), 1.0),
        'c': nrm(ks[1], (BATCH, D_MODEL), 1.0),
        'ctx': nrm(ks[2], (BATCH, CTX_LEN, D_MODEL), 1.0),
        'c_ctx': nrm(ks[3], (D_MODEL,), 1.0),
        'w_ada': nrm(ks[4], (DEPTH, D_MODEL, 6 * D_MODEL), 0.5 * D_MODEL ** -0.5),
        'b_ada': nrm(ks[5], (DEPTH, 6 * D_MODEL), 0.02),
        'norm1_g': 1.0 + nrm(ks[6], (DEPTH, D_MODEL), 0.05),
        'norm2_g': 1.0 + nrm(ks[7], (DEPTH, D_MODEL), 0.05),
        'w_in': nrm(ks[8], (DEPTH, D_MODEL, D_IN), D_MODEL ** -0.5),
        'w_fourier': nrm(ks[9], (DEPTH, FOURIER_GROUPS, FOURIER_GROUP_DIM, FOURIER_GROUP_DIM), FOURIER_GROUP_DIM ** -0.5),
        'swa_sink': nrm(ks[10], (DEPTH, SWA_HEADS), 0.5),
        'mla_q_norm': 1.0 + nrm(ks[11], (DEPTH, MLA_Q_RANK), 0.05),
        'w_uq': nrm(ks[12], (DEPTH, MLA_Q_RANK, MLA_HEADS * (MLA_NOPE_DIM + MLA_ROPE_DIM)), MLA_Q_RANK ** -0.5),
        'mla_kv_norm': 1.0 + nrm(ks[13], (DEPTH, MLA_KV_RANK), 0.05),
        'w_ukv': nrm(ks[14], (DEPTH, MLA_KV_RANK, MLA_HEADS * (MLA_NOPE_DIM + MLA_V_DIM)), MLA_KV_RANK ** -0.5),
        'w_out': nrm(ks[15], (DEPTH, D_MIX, D_MODEL), D_MIX ** -0.5),
        'w_mlp1': nrm(ks[16], (DEPTH, D_MODEL, D_FF), D_MODEL ** -0.5),
        'w_mlp2': nrm(ks[17], (DEPTH, D_FF, D_MODEL), D_FF ** -0.5),
        'final_norm_g': 1.0 + nrm(ks[18], (D_MODEL,), 0.05),
    }


def reference(x, c, ctx, c_ctx, w_ada, b_ada, norm1_g, norm2_g, w_in, w_fourier, swa_sink,
              mla_q_norm, w_uq, mla_kv_norm, w_ukv, w_out, w_mlp1, w_mlp2, final_norm_g):
    B_, S_, _ = x.shape
    L = ctx.shape[1]
    rows = S_ // GRID_W
    cos_h, sin_h = axial_rope_tables(rows, HEAD_DIM)
    cos_r, sin_r = axial_rope_tables(rows, MLA_ROPE_DIM)
    silu_c = jax.nn.silu(c)
    silu_cc = jax.nn.silu(c_ctx)[None, :]
    h, hc = x, ctx
    for l in range(DEPTH):
        last = l == DEPTH - 1
        mod = (silu_c @ w_ada[l] + b_ada[l])[:, None, :]
        mod_c = (silu_cc @ w_ada[l] + b_ada[l])[:, None, :]
        sh1, sc1, g1, sh2, sc2, g2 = jnp.split(mod, 6, axis=-1)
        csh1, csc1, cg1, csh2, csc2, cg2 = jnp.split(mod_c, 6, axis=-1)

        u = modulate(rms_norm(h, norm1_g[l]), sh1, sc1) @ w_in[l]
        uc = modulate(rms_norm(hc, norm1_g[l]), csh1, csc1) @ w_in[l]
        f, q, k, v, cq, ckv, kr = jnp.split(u, IN_SPLITS, axis=-1)
        fc, qc, kc, vc, cqc, ckvc, krc = jnp.split(uc, IN_SPLITS, axis=-1)

        q = apply_rope(q.reshape(B_, S_, SWA_HEADS, HEAD_DIM), cos_h, sin_h)
        k = apply_rope(k.reshape(B_, S_, SWA_KV_HEADS, HEAD_DIM), cos_h, sin_h)
        v = v.reshape(B_, S_, SWA_KV_HEADS, HEAD_DIM)
        kc = kc.reshape(B_, L, SWA_KV_HEADS, HEAD_DIM)
        vc = vc.reshape(B_, L, SWA_KV_HEADS, HEAD_DIM)
        swa_out = swa_latent(q, k, v, kc, vc, swa_sink[l])

        qn, qr = mla_queries(cq, mla_q_norm[l], w_uq[l])
        qr = apply_rope(qr, cos_r, sin_r)
        kn, mv = mla_keys_values(ckv, mla_kv_norm[l], w_ukv[l])
        kr = apply_rope(kr[:, :, None, :], cos_r, sin_r)[:, :, 0, :]
        knc, mvc = mla_keys_values(ckvc, mla_kv_norm[l], w_ukv[l])
        mla_out = mla_latent(qn, qr, kn, kr, mv, knc, krc, mvc)

        mix = jnp.concatenate([fourier_mix(f, w_fourier[l]), swa_out, mla_out], axis=-1) @ w_out[l]
        h = h + g1 * mix
        h = h + g2 * squared_relu_mlp(modulate(rms_norm(h, norm2_g[l]), sh2, sc2), w_mlp1[l], w_mlp2[l])

        if not last:
            qc = qc.reshape(B_, L, SWA_HEADS, HEAD_DIM)
            qnc, qrc = mla_queries(cqc, mla_q_norm[l], w_uq[l])
            mix_c = jnp.concatenate([fourier_mix(fc, w_fourier[l]),
                                     swa_context(qc, kc, vc, swa_sink[l]),
                                     mla_context(qnc, qrc, knc, krc, mvc)], axis=-1) @ w_out[l]
            hc = hc + cg1 * mix_c
            hc = hc + cg2 * squared_relu_mlp(modulate(rms_norm(hc, norm2_g[l]), csh2, csc2), w_mlp1[l], w_mlp2[l])
    return rms_norm(h, final_norm_g)
```

```python
import functools
import math

import jax
import jax.numpy as jnp
from jax import lax
from jax.experimental import pallas as pl
from jax.experimental.pallas import tpu as pltpu

D_MODEL = 1024
GRID_W = 64
HEAD_DIM = 64
FOURIER_GROUPS = 4
FOURIER_GROUP_DIM = 64
FOURIER_WIDTH = FOURIER_GROUPS * FOURIER_GROUP_DIM
SWA_HEADS = 6
SWA_KV_HEADS = 2
SWA_GROUP = SWA_HEADS // SWA_KV_HEADS
SWA_Q_WIDTH = SWA_HEADS * HEAD_DIM
SWA_KV_WIDTH = SWA_KV_HEADS * HEAD_DIM
WINDOW = 128
BLOCK = 128
MLA_HEADS = 6
MLA_NOPE_DIM = 64
MLA_ROPE_DIM = 32
MLA_V_DIM = 64
MLA_Q_RANK = 256
MLA_KV_RANK = 128
MLA_SCALE = (MLA_NOPE_DIM + MLA_ROPE_DIM) ** -0.5
SWA_SCALE = HEAD_DIM ** -0.5
D_MIX = FOURIER_WIDTH + SWA_Q_WIDTH + MLA_HEADS * MLA_V_DIM
OFF_SWA_Q = FOURIER_WIDTH
OFF_SWA_K = OFF_SWA_Q + SWA_Q_WIDTH
OFF_SWA_V = OFF_SWA_K + SWA_KV_WIDTH
OFF_MLA_CQ = OFF_SWA_V + SWA_KV_WIDTH
OFF_MLA_CKV = OFF_MLA_CQ + MLA_Q_RANK
OFF_MLA_KR = OFF_MLA_CKV + MLA_KV_RANK
D_FF = 4 * D_MODEL
ROPE_THETA = 10000.0
NORM_EPS = 1e-6
NEG_INF = -1e30

LANE = 128
MLA_PAD = LANE

C_F = 0
C_Q = C_F + FOURIER_WIDTH
C_K = C_Q + SWA_Q_WIDTH
C_V = C_K + SWA_KV_WIDTH
C_CQ = C_V + SWA_KV_WIDTH
C_CKV = C_CQ + MLA_Q_RANK
C_KR = C_CKV + MLA_KV_RANK
C_QROT = C_KR + LANE
C_KROT = C_QROT + SWA_Q_WIDTH
C_KRROT = C_KROT + SWA_KV_WIDTH
W1_COLS = C_KRROT + LANE

BF16 = jnp.bfloat16
F32 = jnp.float32

VMEM_LIMIT = 56 * 1024 * 1024


def _cparams(sem):
    return pltpu.CompilerParams(dimension_semantics=sem, vmem_limit_bytes=VMEM_LIMIT)


def _rms(x, g):
    ms = jnp.mean(x * x, axis=-1, keepdims=True)
    return x * lax.rsqrt(ms + NORM_EPS) * g


def _ada_kernel(cc_ref, w_ref, b_ref, o_ref):
    cc = cc_ref[...]
    s = cc * jax.nn.sigmoid(cc)
    o_ref[0] = jnp.dot(s, w_ref[0], preferred_element_type=F32,
                       precision=lax.Precision.HIGHEST) + b_ref[0]


def _ada_call(cc, w_ada, b_ada):
    depth = w_ada.shape[0]
    rows = cc.shape[0]
    tn = 1536
    n_out = w_ada.shape[2]
    return pl.pallas_call(
        _ada_kernel,
        grid=(depth, n_out // tn),
        in_specs=[
            pl.BlockSpec((rows, D_MODEL), lambda l, j: (0, 0)),
            pl.BlockSpec((1, D_MODEL, tn), lambda l, j: (l, 0, j)),
            pl.BlockSpec((1, 1, tn), lambda l, j: (l, 0, j)),
        ],
        out_specs=pl.BlockSpec((1, rows, tn), lambda l, j: (l, 0, j)),
        out_shape=jax.ShapeDtypeStruct((depth, rows, n_out), F32),
        compiler_params=_cparams(("arbitrary", "arbitrary")),
        name="ada_mod",
    )(cc, w_ada, b_ada.reshape(depth, 1, n_out))


def _fw_kernel(c_ref, s_ref, w_ref, o_ref):
    o_ref[...] = jnp.zeros(o_ref.shape, o_ref.dtype)
    inv = 1.0 / math.sqrt(FOURIER_GROUP_DIM)
    for g in range(FOURIER_GROUPS):
        w = w_ref[g]
        a = jnp.dot(c_ref[...], w, preferred_element_type=F32, precision=lax.Precision.HIGHEST) * inv
        b = jnp.dot(s_ref[...], w, preferred_element_type=F32, precision=lax.Precision.HIGHEST) * inv
        lo = g * FOURIER_GROUP_DIM
        hi = lo + FOURIER_GROUP_DIM
        o_ref[lo:hi, lo:hi] = a
        o_ref[lo:hi, FOURIER_WIDTH + lo:FOURIER_WIDTH + hi] = b


def _fw_call(c64, s64, w_f):
    return pl.pallas_call(
        _fw_kernel,
        out_shape=jax.ShapeDtypeStruct((FOURIER_WIDTH, 2 * FOURIER_WIDTH), F32),
        name="fourier_w",
    )(c64, s64, w_f)


def _proj_kernel(h_ref, mod_ref, g1_ref, w1_ref, rope_ref, wcs_ref, qn_ref, wuq_ref, wuqr_ref,
                 kvn_ref, wukvk_ref, wukvv_ref,
                 uv_o, sq_o, sk_o, sv_o, mq_o, mk_o, mv_o):
    x = h_ref[0]
    mod = mod_ref[0]
    sh = mod[:, 0:D_MODEL]
    sc = mod[:, D_MODEL:2 * D_MODEL]
    y = _rms(x, g1_ref[...]) * (1.0 + sc) + sh
    u = jnp.dot(y.astype(BF16), w1_ref[...], preferred_element_type=F32)

    cos_h = rope_ref[:, 0:LANE]
    sin_h = rope_ref[:, LANE:2 * LANE]
    cos_m = rope_ref[:, 2 * LANE:3 * LANE]
    sin_m = rope_ref[:, 3 * LANE:4 * LANE]

    f = u[:, C_F:C_F + FOURIER_WIDTH].astype(BF16)
    uv_o[0] = jnp.dot(f, wcs_ref[...], preferred_element_type=F32).astype(BF16)

    for j in range(SWA_Q_WIDTH // LANE):
        qj = (u[:, C_Q + LANE * j:C_Q + LANE * (j + 1)] * cos_h
              + u[:, C_QROT + LANE * j:C_QROT + LANE * (j + 1)] * sin_h)
        qj = (qj * SWA_SCALE).astype(BF16)
        sq_o[0, 2 * j] = qj[:, :HEAD_DIM]
        sq_o[0, 2 * j + 1] = qj[:, HEAD_DIM:]
    kk = (u[:, C_K:C_K + LANE] * cos_h + u[:, C_KROT:C_KROT + LANE] * sin_h).astype(BF16)
    sk_o[0, 0] = kk[:, :HEAD_DIM]
    sk_o[0, 1] = kk[:, HEAD_DIM:]
    vv = u[:, C_V:C_V + LANE].astype(BF16)
    sv_o[0, 0] = vv[:, :HEAD_DIM]
    sv_o[0, 1] = vv[:, HEAD_DIM:]

    cqn = _rms(u[:, C_CQ:C_CQ + MLA_Q_RANK], qn_ref[...]).astype(BF16)
    qa = jnp.dot(cqn, wuq_ref[...], preferred_element_type=F32)
    qr = jnp.dot(cqn, wuqr_ref[...], preferred_element_type=F32)
    for hh in range(MLA_HEADS):
        sl = slice(MLA_PAD * hh, MLA_PAD * (hh + 1))
        mq_o[0, hh] = ((qa[:, sl] * cos_m + qr[:, sl] * sin_m) * MLA_SCALE).astype(BF16)

    ckvn = _rms(u[:, C_CKV:C_CKV + MLA_KV_RANK], kvn_ref[...]).astype(BF16)
    kr = u[:, C_KR:C_KR + LANE] * cos_m + u[:, C_KRROT:C_KRROT + LANE] * sin_m
    kvk = jnp.dot(ckvn, wukvk_ref[...], preferred_element_type=F32)
    kvv = jnp.dot(ckvn, wukvv_ref[...], preferred_element_type=F32)
    for hh in range(MLA_HEADS):
        mk_o[0, hh] = (kvk[:, MLA_PAD * hh:MLA_PAD * (hh + 1)] + kr).astype(BF16)
        mv_o[0, hh] = kvv[:, MLA_V_DIM * hh:MLA_V_DIM * (hh + 1)].astype(BF16)


def _proj_call(h, mod3, mod_row, g1, w1, rope, wcs, qn, wuq, wuqr, kvn, wukvk, wukvv, tm):
    b_, s_, _ = h.shape
    const = lambda b, i: (0, 0)
    if mod_row is None:
        mod_map = lambda b, i: (b, 0, 0)
    else:
        mod_map = lambda b, i: (mod_row, 0, 0)
    out_shapes = (
        jax.ShapeDtypeStruct((b_, s_, 2 * FOURIER_WIDTH), BF16),
        jax.ShapeDtypeStruct((b_, SWA_HEADS, s_, HEAD_DIM), BF16),
        jax.ShapeDtypeStruct((b_, SWA_KV_HEADS, s_, HEAD_DIM), BF16),
        jax.ShapeDtypeStruct((b_, SWA_KV_HEADS, s_, HEAD_DIM), BF16),
        jax.ShapeDtypeStruct((b_, MLA_HEADS, s_, MLA_PAD), BF16),
        jax.ShapeDtypeStruct((b_, MLA_HEADS, s_, MLA_PAD), BF16),
        jax.ShapeDtypeStruct((b_, MLA_HEADS, s_, MLA_V_DIM), BF16),
    )
    out_specs = (
        pl.BlockSpec((1, tm, 2 * FOURIER_WIDTH), lambda b, i: (b, i, 0)),
        pl.BlockSpec((1, SWA_HEADS, tm, HEAD_DIM), lambda b, i: (b, 0, i, 0)),
        pl.BlockSpec((1, SWA_KV_HEADS, tm, HEAD_DIM), lambda b, i: (b, 0, i, 0)),
        pl.BlockSpec((1, SWA_KV_HEADS, tm, HEAD_DIM), lambda b, i: (b, 0, i, 0)),
        pl.BlockSpec((1, MLA_HEADS, tm, MLA_PAD), lambda b, i: (b, 0, i, 0)),
        pl.BlockSpec((1, MLA_HEADS, tm, MLA_PAD), lambda b, i: (b, 0, i, 0)),
        pl.BlockSpec((1, MLA_HEADS, tm, MLA_V_DIM), lambda b, i: (b, 0, i, 0)),
    )
    return pl.pallas_call(
        _proj_kernel,
        grid=(b_, s_ // tm),
        in_specs=[
            pl.BlockSpec((1, tm, D_MODEL), lambda b, i: (b, i, 0)),
            pl.BlockSpec((1, 1, 6 * D_MODEL), mod_map),
            pl.BlockSpec((1, D_MODEL), const),
            pl.BlockSpec((D_MODEL, W1_COLS), const),
            pl.BlockSpec((tm, 4 * LANE), lambda b, i: (i, 0)),
            pl.BlockSpec((FOURIER_WIDTH, 2 * FOURIER_WIDTH), const),
            pl.BlockSpec((1, MLA_Q_RANK), const),
            pl.BlockSpec((MLA_Q_RANK, MLA_HEADS * MLA_PAD), const),
            pl.BlockSpec((MLA_Q_RANK, MLA_HEADS * MLA_PAD), const),
            pl.BlockSpec((1, MLA_KV_RANK), const),
            pl.BlockSpec((MLA_KV_RANK, MLA_HEADS * MLA_PAD), const),
            pl.BlockSpec((MLA_KV_RANK, MLA_HEADS * MLA_V_DIM), const),
        ],
        out_specs=out_specs,
        out_shape=out_shapes,
        compiler_params=_cparams(("parallel", "parallel")),
        name="norm_in_proj",
    )(h, mod3, g1, w1, rope, wcs, qn, wuq, wuqr, kvn, wukvk, wukvv)


def _dft_kernel(c_ref, s_ref, uv_ref, o_ref):
    uv = uv_ref[0]
    o = jnp.dot(c_ref[...], uv[:, :FOURIER_WIDTH], preferred_element_type=F32)
    o = o + jnp.dot(s_ref[...], uv[:, FOURIER_WIDTH:], preferred_element_type=F32)
    o_ref[0] = o.astype(o_ref.dtype)


def _dft_call(ctab, stab, uv, tk):
    b_, s_, _ = uv.shape
    return pl.pallas_call(
        _dft_kernel,
        grid=(s_ // tk, b_),
        in_specs=[
            pl.BlockSpec((tk, s_), lambda i, b: (i, 0)),
            pl.BlockSpec((tk, s_), lambda i, b: (i, 0)),
            pl.BlockSpec((1, s_, 2 * FOURIER_WIDTH), lambda i, b: (b, 0, 0)),
        ],
        out_specs=pl.BlockSpec((1, tk, FOURIER_WIDTH), lambda i, b: (b, i, 0)),
        out_shape=jax.ShapeDtypeStruct((b_, s_, FOURIER_WIDTH), BF16),
        compiler_params=_cparams(("parallel", "arbitrary")),
        name="fourier_dft",
    )(ctab, stab, uv)


_NT = (((1,), (1,)), ((), ()))


def _mla_kernel(*refs, n_chunks, tk, heads_per_step):
    if n_chunks:
        q_ref, k_ref, v_ref, kc_ref, vc_ref, o_ref = refs
    else:
        q_ref, kc_ref, vc_ref, o_ref = refs
    outs = []
    for hh in range(heads_per_step):
        q = q_ref[0, hh]
        s = lax.dot_general(q, kc_ref[0, hh], _NT, preferred_element_type=F32)
        m = jnp.max(s, axis=-1, keepdims=True)
        p = jnp.exp(s - m)
        l = jnp.sum(p, axis=-1, keepdims=True)
        acc = jnp.dot(p.astype(BF16), vc_ref[0, hh], preferred_element_type=F32)

        if n_chunks:
            def body(c, carry, hh=hh, q=q):
                m, l, acc = carry
                off = pl.multiple_of(c * tk, tk)
                ks = k_ref[0, hh, pl.ds(off, tk), :]
                vs = v_ref[0, hh, pl.ds(off, tk), :]
                s = lax.dot_general(q, ks, _NT, preferred_element_type=F32)
                m_new = jnp.maximum(m, jnp.max(s, axis=-1, keepdims=True))
                alpha = jnp.exp(m - m_new)
                p = jnp.exp(s - m_new)
                l = alpha * l + jnp.sum(p, axis=-1, keepdims=True)
                acc = alpha * acc + jnp.dot(p.astype(BF16), vs, preferred_element_type=F32)
                return m_new, l, acc

            m, l, acc = lax.fori_loop(0, n_chunks, body, (m, l, acc))
        outs.append(acc / l)
    o_ref[0] = jnp.concatenate(outs, axis=-1).astype(o_ref.dtype)


def _mla_call(q, k, v, kc, vc, tq, tk):
    b_, h_, sq, _ = q.shape
    lc = kc.shape[2]
    hps = 2
    n_chunks = 0 if k is None else k.shape[2] // tk
    qmap = lambda b, p, i: (b, p, i, 0)
    kvmap = lambda b, p, i: (b, p, 0, 0)
    in_specs = [pl.BlockSpec((1, hps, tq, MLA_PAD), qmap)]
    args = [q]
    if k is not None:
        sk = k.shape[2]
        in_specs += [pl.BlockSpec((1, hps, sk, MLA_PAD), kvmap),
                     pl.BlockSpec((1, hps, sk, MLA_V_DIM), kvmap)]
        args += [k, v]
    in_specs += [pl.BlockSpec((1, hps, lc, MLA_PAD), kvmap),
                 pl.BlockSpec((1, hps, lc, MLA_V_DIM), kvmap)]
    args += [kc, vc]
    return pl.pallas_call(
        functools.partial(_mla_kernel, n_chunks=n_chunks, tk=tk, heads_per_step=hps),
        grid=(b_, h_ // hps, sq // tq),
        in_specs=in_specs,
        out_specs=pl.BlockSpec((1, tq, hps * MLA_V_DIM), lambda b, p, i: (b, i, p)),
        out_shape=jax.ShapeDtypeStruct((b_, sq, h_ * MLA_V_DIM), BF16),
        compiler_params=_cparams(("parallel", "parallel", "arbitrary")),
        name="mla_attn",
    )(*args)


def _swa_kernel(*refs, local, s_len):
    if local:
        sink_ref, q_ref, kp_ref, k0_ref, kn_ref, vp_ref, v0_ref, vn_ref, kc_ref, vc_ref, o_ref = refs
    else:
        sink_ref, q_ref, kc_ref, vc_ref, o_ref = refs
    tq = q_ref.shape[2]
    rows = SWA_GROUP * tq
    outs = []
    row = lax.broadcasted_iota(jnp.int32, (rows, 1), 0)
    if local:
        n = pl.program_id(1)
        nk = 3 * BLOCK
        assert tq & (tq - 1) == 0
        qpos = n * BLOCK + (lax.broadcasted_iota(jnp.int32, (rows, nk), 0) & (tq - 1))
        kpos = (n - 1) * BLOCK + lax.broadcasted_iota(jnp.int32, (rows, nk), 1)
        valid = (jnp.abs(kpos - qpos) <= WINDOW) & (kpos >= 0) & (kpos < s_len)
    for kh in range(SWA_KV_HEADS):
        q = q_ref[0, SWA_GROUP * kh:SWA_GROUP * (kh + 1)].reshape(rows, HEAD_DIM)
        sink = jnp.full((rows, 1), sink_ref[SWA_GROUP * kh], F32)
        for g in range(1, SWA_GROUP):
            sink = jnp.where(row >= g * tq, sink_ref[SWA_GROUP * kh + g], sink)
        kc = kc_ref[0, kh]
        vc = vc_ref[0, kh]
        s_c = lax.dot_general(q, kc, _NT, preferred_element_type=F32)
        m = jnp.maximum(jnp.max(s_c, axis=-1, keepdims=True), sink)
        if local:
            kl = jnp.concatenate([kp_ref[0, kh], k0_ref[0, kh], kn_ref[0, kh]], axis=0)
            vl = jnp.concatenate([vp_ref[0, kh], v0_ref[0, kh], vn_ref[0, kh]], axis=0)
            s_l = lax.dot_general(q, kl, _NT, preferred_element_type=F32)
            s_l = jnp.where(valid, s_l, NEG_INF)
            m = jnp.maximum(m, jnp.max(s_l, axis=-1, keepdims=True))
        p_c = jnp.exp(s_c - m)
        l = jnp.sum(p_c, axis=-1, keepdims=True) + jnp.exp(sink - m)
        o = jnp.dot(p_c.astype(BF16), vc, preferred_element_type=F32)
        if local:
            p_l = jnp.exp(s_l - m)
            l = l + jnp.sum(p_l, axis=-1, keepdims=True)
            o = o + jnp.dot(p_l.astype(BF16), vl, preferred_element_type=F32)
        o = o / l
        for g in range(SWA_GROUP):
            outs.append(o[g * tq:(g + 1) * tq])
    o_ref[0] = jnp.concatenate(outs, axis=-1).astype(o_ref.dtype)


def _swa_call(sink, q, k, v, kc, vc):
    b_, _, sq, _ = q.shape
    lc = kc.shape[2]
    local = k is not None
    tq = BLOCK if local else sq
    nb = sq // tq
    in_specs = [pl.BlockSpec(memory_space=pltpu.SMEM),
                pl.BlockSpec((1, SWA_HEADS, tq, HEAD_DIM), lambda b, n: (b, 0, n, 0))]
    args = [sink, q]
    if local:
        prev = lambda b, n: (b, 0, jnp.maximum(n - 1, 0), 0)
        cur = lambda b, n: (b, 0, n, 0)
        nxt = lambda b, n: (b, 0, jnp.minimum(n + 1, nb - 1), 0)
        blk = (1, SWA_KV_HEADS, BLOCK, HEAD_DIM)
        in_specs += [pl.BlockSpec(blk, prev), pl.BlockSpec(blk, cur), pl.BlockSpec(blk, nxt)] * 2
        args += [k, k, k, v, v, v]
    cblk = (1, SWA_KV_HEADS, lc, HEAD_DIM)
    in_specs += [pl.BlockSpec(cblk, lambda b, n: (b, 0, 0, 0))] * 2
    args += [kc, vc]
    return pl.pallas_call(
        functools.partial(_swa_kernel, local=local, s_len=sq),
        grid=(b_, nb),
        in_specs=in_specs,
        out_specs=pl.BlockSpec((1, tq, SWA_Q_WIDTH), lambda b, n: (b, n, 0)),
        out_shape=jax.ShapeDtypeStruct((b_, sq, SWA_Q_WIDTH), BF16),
        compiler_params=_cparams(("parallel", "arbitrary")),
        name="swa_attn",
    )(*args)


def _mlp_kernel(fo_ref, so_ref, mo_ref, h_ref, mod_ref, g2_ref, wo_ref, w1_ref, w2_ref, gf_ref, o_ref, *,
                final, ff_chunk):
    mod = mod_ref[0]
    g1 = mod[:, 2 * D_MODEL:3 * D_MODEL]
    sh2 = mod[:, 3 * D_MODEL:4 * D_MODEL]
    sc2 = mod[:, 4 * D_MODEL:5 * D_MODEL]
    g2 = mod[:, 5 * D_MODEL:6 * D_MODEL]
    mix = jnp.concatenate([fo_ref[0], so_ref[0], mo_ref[0]], axis=-1)
    h = h_ref[0] + g1 * jnp.dot(mix, wo_ref[...], preferred_element_type=F32)
    y = (_rms(h, g2_ref[...]) * (1.0 + sc2) + sh2).astype(BF16)
    acc = None
    for c in range(D_FF // ff_chunk):
        a = jnp.dot(y, w1_ref[:, c * ff_chunk:(c + 1) * ff_chunk], preferred_element_type=F32)
        a = jnp.maximum(a, 0.0)
        a = (a * a).astype(BF16)
        part = jnp.dot(a, w2_ref[c * ff_chunk:(c + 1) * ff_chunk, :], preferred_element_type=F32)
        acc = part if acc is None else acc + part
    h = h + g2 * acc
    if final:
        h = _rms(h, gf_ref[...])
    o_ref[0] = h


def _mlp_call(fo, so, mo, h, mod3, mod_row, g2, wo, w1, w2, gf, tm, final):
    b_, s_, _ = h.shape
    const = lambda b, i: (0, 0)
    tile = lambda b, i: (b, i, 0)
    if mod_row is None:
        mod_map = lambda b, i: (b, 0, 0)
    else:
        mod_map = lambda b, i: (mod_row, 0, 0)
    once = pl.Buffered(1)
    return pl.pallas_call(
        functools.partial(_mlp_kernel, final=final, ff_chunk=1024),
        grid=(b_, s_ // tm),
        in_specs=[
            pl.BlockSpec((1, tm, FOURIER_WIDTH), tile),
            pl.BlockSpec((1, tm, SWA_Q_WIDTH), tile),
            pl.BlockSpec((1, tm, MLA_HEADS * MLA_V_DIM), tile),
            pl.BlockSpec((1, tm, D_MODEL), tile),
            pl.BlockSpec((1, 1, 6 * D_MODEL), mod_map),
            pl.BlockSpec((1, D_MODEL), const),
            pl.BlockSpec((D_MIX, D_MODEL), const, pipeline_mode=once),
            pl.BlockSpec((D_MODEL, D_FF), const, pipeline_mode=once),
            pl.BlockSpec((D_FF, D_MODEL), const, pipeline_mode=once),
            pl.BlockSpec((1, D_MODEL), const),
        ],
        out_specs=pl.BlockSpec((1, tm, D_MODEL), tile),
        out_shape=jax.ShapeDtypeStruct((b_, s_, D_MODEL), F32),
        compiler_params=_cparams(("parallel", "parallel")),
        name="out_proj_mlp",
    )(fo, so, mo, h, mod3, g2, wo, w1, w2, gf)


def _rot_cols(w, half):
    return jnp.concatenate([-w[:, half:], w[:, :half]], axis=1)


def _pack_w_in(w_in):
    z = lambda n: jnp.zeros((D_MODEL, n), F32)
    wq = w_in[:, OFF_SWA_Q:OFF_SWA_K]
    wk = w_in[:, OFF_SWA_K:OFF_SWA_V]
    wkr = w_in[:, OFF_MLA_KR:OFF_MLA_KR + MLA_ROPE_DIM]
    half = HEAD_DIM // 2
    wq_rot = jnp.concatenate(
        [_rot_cols(wq[:, HEAD_DIM * i:HEAD_DIM * (i + 1)], half) for i in range(SWA_HEADS)], axis=1)
    wk_rot = jnp.concatenate(
        [_rot_cols(wk[:, HEAD_DIM * i:HEAD_DIM * (i + 1)], half) for i in range(SWA_KV_HEADS)], axis=1)
    pad_kr = lambda w: jnp.concatenate([z(MLA_NOPE_DIM), w, z(LANE - MLA_NOPE_DIM - MLA_ROPE_DIM)], axis=1)
    cols = [w_in[:, :OFF_MLA_KR], pad_kr(wkr), wq_rot, wk_rot, pad_kr(_rot_cols(wkr, MLA_ROPE_DIM // 2))]
    return jnp.concatenate(cols, axis=1).astype(BF16)


def _pack_w_uq(w_uq):
    per = MLA_NOPE_DIM + MLA_ROPE_DIM
    z = jnp.zeros((MLA_Q_RANK, MLA_PAD - per), F32)
    zn = jnp.zeros((MLA_Q_RANK, MLA_NOPE_DIM), F32)
    main, rot = [], []
    for hh in range(MLA_HEADS):
        w = w_uq[:, per * hh:per * (hh + 1)]
        main += [w, z]
        rot += [zn, _rot_cols(w[:, MLA_NOPE_DIM:], MLA_ROPE_DIM // 2), z]
    return jnp.concatenate(main, axis=1).astype(BF16), jnp.concatenate(rot, axis=1).astype(BF16)


def _pack_w_ukv(w_ukv):
    per = MLA_NOPE_DIM + MLA_V_DIM
    z = jnp.zeros((MLA_KV_RANK, MLA_PAD - MLA_NOPE_DIM), F32)
    ks, vs = [], []
    for hh in range(MLA_HEADS):
        w = w_ukv[:, per * hh:per * (hh + 1)]
        ks += [w[:, :MLA_NOPE_DIM], z]
        vs.append(w[:, MLA_NOPE_DIM:])
    return jnp.concatenate(ks, axis=1).astype(BF16), jnp.concatenate(vs, axis=1).astype(BF16)


def _rope_table(rows):
    r, col = jnp.meshgrid(jnp.arange(rows, dtype=F32), jnp.arange(GRID_W, dtype=F32), indexing="ij")
    r = r.reshape(-1)
    col = col.reshape(-1)

    def tables(dim):
        n_freq = dim // 4
        inv = ROPE_THETA ** (-jnp.arange(n_freq, dtype=F32) / n_freq)
        ang = jnp.concatenate([r[:, None] * inv[None, :], col[:, None] * inv[None, :]], axis=-1)
        return jnp.cos(ang), jnp.sin(ang)

    ch, sh = tables(HEAD_DIM)
    cr, sr = tables(MLA_ROPE_DIM)
    s_ = r.shape[0]
    ones = jnp.ones((s_, MLA_NOPE_DIM), F32)
    zn = jnp.zeros((s_, MLA_NOPE_DIM), F32)
    zp = jnp.zeros((s_, LANE - MLA_NOPE_DIM - MLA_ROPE_DIM), F32)
    return jnp.concatenate([ch, ch, ch, ch, sh, sh, sh, sh,
                            ones, cr, cr, zp, zn, sr, sr, zp], axis=-1)


def _identity_rope_table(n):
    ones = jnp.ones((n, LANE), F32)
    zeros = jnp.zeros((n, LANE), F32)
    m = jnp.concatenate([jnp.ones((n, MLA_NOPE_DIM + MLA_ROPE_DIM), F32),
                         jnp.zeros((n, LANE - MLA_NOPE_DIM - MLA_ROPE_DIM), F32)], axis=-1)
    return jnp.concatenate([ones, zeros, m, zeros], axis=-1)


def _dft_tables(n):
    k = jnp.arange(n, dtype=jnp.int32)
    ks = (k[:, None] * k[None, :]) % n
    ang = ks.astype(F32) * (2.0 * math.pi / n)
    scale = 1.0 / math.sqrt(n)
    return (jnp.cos(ang) * scale).astype(BF16), (-jnp.sin(ang) * scale).astype(BF16)


def _dft_tables_big(n, n1):
    n2 = n // n1
    k = jnp.arange(n, dtype=jnp.int32)
    a1 = ((k[:, None] * jnp.arange(n1, dtype=jnp.int32)[None, :]) % n1).astype(F32) * (2.0 * math.pi / n1)
    a2 = ((k[:, None] * jnp.arange(n2, dtype=jnp.int32)[None, :]) % n).astype(F32) * (2.0 * math.pi / n)
    c1, s1 = jnp.cos(a1)[:, :, None], jnp.sin(a1)[:, :, None]
    c2, s2 = jnp.cos(a2)[:, None, :], jnp.sin(a2)[:, None, :]
    scale = 1.0 / math.sqrt(n)
    ct = ((c1 * c2 - s1 * s2) * scale).reshape(n, n).astype(BF16)
    st = (-(s1 * c2 + c1 * s2) * scale).reshape(n, n).astype(BF16)
    return ct, st


def kernel(x, c, ctx, c_ctx, w_ada, b_ada, norm1_g, norm2_g, w_in, w_fourier, swa_sink, mla_q_norm, w_uq,
           mla_kv_norm, w_ukv, w_out, w_mlp1, w_mlp2, final_norm_g):
    b_, s_, _ = x.shape
    lc = ctx.shape[1]
    depth = w_ada.shape[0]
    rows = s_ // GRID_W

    mod_rows = 16
    cc = jnp.concatenate([c, c_ctx[None, :], jnp.zeros((mod_rows - b_ - 1, D_MODEL), F32)], axis=0)
    mod_all = _ada_call(cc, w_ada, b_ada)

    rope = _rope_table(rows)
    rope_c = _identity_rope_table(lc)
    ctab, stab = _dft_tables_big(s_, GRID_W)
    ctab_c, stab_c = _dft_tables(lc)
    kk = jnp.arange(FOURIER_GROUP_DIM, dtype=jnp.int32)
    ang64 = ((kk[:, None] * kk[None, :]) % FOURIER_GROUP_DIM).astype(F32) * (2.0 * math.pi / FOURIER_GROUP_DIM)
    c64, s64 = jnp.cos(ang64), jnp.sin(ang64)

    h, hc = x, ctx
    for l in range(depth):
        last = l == depth - 1
        mod3 = mod_all[l].reshape(mod_rows, 1, 6 * D_MODEL)
        g1 = norm1_g[l].reshape(1, D_MODEL)
        g2 = norm2_g[l].reshape(1, D_MODEL)
        gf = final_norm_g.reshape(1, D_MODEL)
        w1 = _pack_w_in(w_in[l])
        wuq, wuqr = _pack_w_uq(w_uq[l])
        wukvk, wukvv = _pack_w_ukv(w_ukv[l])
        wcs = _fw_call(c64, s64, w_fourier[l]).astype(BF16)
        qn = mla_q_norm[l].reshape(1, MLA_Q_RANK)
        kvn = mla_kv_norm[l].reshape(1, MLA_KV_RANK)
        wo = w_out[l].astype(BF16)
        wm1 = w_mlp1[l].astype(BF16)
        wm2 = w_mlp2[l].astype(BF16)
        sink = swa_sink[l]

        uv, sq, sk, sv, mq, mk, mv = _proj_call(h, mod3, None, g1, w1, rope, wcs, qn, wuq, wuqr, kvn,
                                                wukvk, wukvv, tm=512)
        uvc, sqc, skc, svc, mqc, mkc, mvc = _proj_call(hc, mod3, b_, g1, w1, rope_c, wcs, qn, wuq, wuqr, kvn,
                                                       wukvk, wukvv, tm=lc)

        fo = _dft_call(ctab, stab, uv, tk=512)
        so = _swa_call(sink, sq, sk, sv, skc, svc)
        mo = _mla_call(mq, mk, mv, mkc, mvc, tq=256, tk=512)
        h = _mlp_call(fo, so, mo, h, mod3, None, g2, wo, wm1, wm2, gf, tm=512, final=last)

        if not last:
            foc = _dft_call(ctab_c, stab_c, uvc, tk=lc)
            soc = _swa_call(sink, sqc, None, None, skc, svc)
            moc = _mla_call(mqc, None, None, mkc, mvc, tq=lc, tk=lc)
            hc = _mlp_call(foc, soc, moc, hc, mod3, b_, g2, wo, wm1, wm2, gf, tm=lc, final=False)
    return h
```

```python
import functools
import math

import jax
import jax.numpy as jnp
from jax import lax
from jax.experimental import pallas as pl
from jax.experimental.pallas import tpu as pltpu

D_MODEL = 1024
GRID_W = 64
HEAD_DIM = 64
FOURIER_GROUPS = 4
FOURIER_GROUP_DIM = 64
FOURIER_WIDTH = FOURIER_GROUPS * FOURIER_GROUP_DIM
SWA_HEADS = 6
SWA_KV_HEADS = 2
SWA_GROUP = SWA_HEADS // SWA_KV_HEADS
SWA_Q_WIDTH = SWA_HEADS * HEAD_DIM
SWA_KV_WIDTH = SWA_KV_HEADS * HEAD_DIM
WINDOW = 128
BLOCK = 128
MLA_HEADS = 6
MLA_NOPE_DIM = 64
MLA_ROPE_DIM = 32
MLA_V_DIM = 64
MLA_Q_RANK = 256
MLA_KV_RANK = 128
MLA_SCALE = (MLA_NOPE_DIM + MLA_ROPE_DIM) ** -0.5
SWA_SCALE = HEAD_DIM ** -0.5
D_MIX = FOURIER_WIDTH + SWA_Q_WIDTH + MLA_HEADS * MLA_V_DIM
OFF_SWA_Q = FOURIER_WIDTH
OFF_SWA_K = OFF_SWA_Q + SWA_Q_WIDTH
OFF_SWA_V = OFF_SWA_K + SWA_KV_WIDTH
OFF_MLA_CQ = OFF_SWA_V + SWA_KV_WIDTH
OFF_MLA_CKV = OFF_MLA_CQ + MLA_Q_RANK
OFF_MLA_KR = OFF_MLA_CKV + MLA_KV_RANK
D_FF = 4 * D_MODEL
ROPE_THETA = 10000.0
NORM_EPS = 1e-6
NEG_INF = -1e30

LANE = 128
MLA_PAD = LANE

C_F = 0
C_Q = C_F + FOURIER_WIDTH
C_K = C_Q + SWA_Q_WIDTH
C_V = C_K + SWA_KV_WIDTH
C_CQ = C_V + SWA_KV_WIDTH
C_CKV = C_CQ + MLA_Q_RANK
C_KR = C_CKV + MLA_KV_RANK
C_QROT = C_KR + LANE
C_KROT = C_QROT + SWA_Q_WIDTH
C_KRROT = C_KROT + SWA_KV_WIDTH
W1_COLS = C_KRROT + LANE

BF16 = jnp.bfloat16
F32 = jnp.float32
LOG2E = math.log2(math.e)
_NT = (((1,), (1,)), ((), ()))
MLA_TK = 512

VMEM_LIMIT = 56 * 1024 * 1024


def _cparams(sem):
    return pltpu.CompilerParams(dimension_semantics=sem, vmem_limit_bytes=VMEM_LIMIT)


def _rms(x, g):
    ms = jnp.mean(x * x, axis=-1, keepdims=True)
    return x * lax.rsqrt(ms + NORM_EPS) * g


def _ada_kernel(cc_ref, w_ref, b_ref, o_ref):
    cc = cc_ref[...]
    s = cc * jax.nn.sigmoid(cc)
    o_ref[0] = jnp.dot(s, w_ref[0], preferred_element_type=F32,
                       precision=lax.Precision.HIGHEST) + b_ref[0]


def _ada_call(cc, w_ada, b_ada):
    depth = w_ada.shape[0]
    rows = cc.shape[0]
    tn = 1536
    n_out = w_ada.shape[2]
    return pl.pallas_call(
        _ada_kernel,
        grid=(depth, n_out // tn),
        in_specs=[
            pl.BlockSpec((rows, D_MODEL), lambda l, j: (0, 0)),
            pl.BlockSpec((1, D_MODEL, tn), lambda l, j: (l, 0, j)),
            pl.BlockSpec((1, 1, tn), lambda l, j: (l, 0, j)),
        ],
        out_specs=pl.BlockSpec((1, rows, tn), lambda l, j: (l, 0, j)),
        out_shape=jax.ShapeDtypeStruct((depth, rows, n_out), F32),
        compiler_params=_cparams(("arbitrary", "arbitrary")),
        name="ada_mod",
    )(cc, w_ada, b_ada.reshape(depth, 1, n_out))


def _fw_kernel(c_ref, s_ref, w_ref, o_ref):
    o_ref[...] = jnp.zeros(o_ref.shape, o_ref.dtype)
    inv = 1.0 / math.sqrt(FOURIER_GROUP_DIM)
    for g in range(FOURIER_GROUPS):
        w = w_ref[g]
        a = jnp.dot(c_ref[...], w, preferred_element_type=F32, precision=lax.Precision.HIGHEST) * inv
        b = jnp.dot(s_ref[...], w, preferred_element_type=F32, precision=lax.Precision.HIGHEST) * inv
        lo = g * FOURIER_GROUP_DIM
        hi = lo + FOURIER_GROUP_DIM
        o_ref[lo:hi, lo:hi] = a
        o_ref[lo:hi, FOURIER_WIDTH + lo:FOURIER_WIDTH + hi] = b


def _fw_call(c64, s64, w_f):
    return pl.pallas_call(
        _fw_kernel,
        out_shape=jax.ShapeDtypeStruct((FOURIER_WIDTH, 2 * FOURIER_WIDTH), F32),
        name="fourier_w",
    )(c64, s64, w_f)


def _proj_kernel(h_ref, mod_ref, g1_ref, w1_ref, rope_ref, ropet_ref, wcs_ref, qn_ref, wuq_ref, wuqr_ref,
                 kvn_ref, wukvk_ref, wukvv_ref,
                 uv_o, sq_o, sk_o, sv_o, mq_o, mk_o, mv_o):
    x = h_ref[0]
    mod = mod_ref[0]
    sh = mod[:, 0:D_MODEL]
    sc = mod[:, D_MODEL:2 * D_MODEL]
    y = _rms(x, g1_ref[...]) * (1.0 + sc) + sh
    u = jnp.dot(y.astype(BF16), w1_ref[...], preferred_element_type=F32)

    cos_h = rope_ref[:, 0:LANE]
    sin_h = rope_ref[:, LANE:2 * LANE]
    cos_m = rope_ref[:, 2 * LANE:3 * LANE]
    sin_m = rope_ref[:, 3 * LANE:4 * LANE]

    f = u[:, C_F:C_F + FOURIER_WIDTH].astype(BF16)
    uv_o[0] = jnp.dot(f, wcs_ref[...], preferred_element_type=F32).astype(BF16)

    for j in range(SWA_Q_WIDTH // LANE):
        qj = (u[:, C_Q + LANE * j:C_Q + LANE * (j + 1)] * cos_h
              + u[:, C_QROT + LANE * j:C_QROT + LANE * (j + 1)] * sin_h)
        qj = (qj * SWA_SCALE).astype(BF16)
        sq_o[0, 2 * j] = qj[:, :HEAD_DIM]
        sq_o[0, 2 * j + 1] = qj[:, HEAD_DIM:]
    kk = (u[:, C_K:C_K + LANE] * cos_h + u[:, C_KROT:C_KROT + LANE] * sin_h).astype(BF16)
    sk_o[0, 0] = kk[:, :HEAD_DIM]
    sk_o[0, 1] = kk[:, HEAD_DIM:]
    vv = u[:, C_V:C_V + LANE].astype(BF16)
    sv_o[0, 0] = vv[:, :HEAD_DIM]
    sv_o[0, 1] = vv[:, HEAD_DIM:]

    cqn = _rms(u[:, C_CQ:C_CQ + MLA_Q_RANK], qn_ref[...]).astype(BF16)
    qa = lax.dot_general(wuq_ref[...], cqn, _NT, preferred_element_type=F32)
    qr = lax.dot_general(wuqr_ref[...], cqn, _NT, preferred_element_type=F32)
    cos_mt = ropet_ref[0:LANE, :]
    sin_mt = ropet_ref[LANE:2 * LANE, :]
    for hh in range(MLA_HEADS):
        sl = slice(MLA_PAD * hh, MLA_PAD * (hh + 1))
        mq_o[0, sl, :] = ((qa[sl] * cos_mt + qr[sl] * sin_mt) * (MLA_SCALE * LOG2E)).astype(BF16)

    ckvn = _rms(u[:, C_CKV:C_CKV + MLA_KV_RANK], kvn_ref[...]).astype(BF16)
    kr = u[:, C_KR:C_KR + LANE] * cos_m + u[:, C_KRROT:C_KRROT + LANE] * sin_m
    kvk = jnp.dot(ckvn, wukvk_ref[...], preferred_element_type=F32)
    vt = lax.dot_general(wukvv_ref[...], ckvn, _NT, preferred_element_type=F32).astype(BF16)
    for hh in range(MLA_HEADS):
        mk_o[0, hh] = (kvk[:, MLA_PAD * hh:MLA_PAD * (hh + 1)] + kr).astype(BF16)
        mv_o[0, hh, 0] = vt[MLA_V_DIM * hh:MLA_V_DIM * (hh + 1)]


def _proj_call(h, mod3, mod_row, g1, w1, rope, ropet, wcs, qn, wuq, wuqr, kvn, wukvk, wukvv, tm):
    b_, s_, _ = h.shape
    const = lambda b, i: (0, 0)
    if mod_row is None:
        mod_map = lambda b, i: (b, 0, 0)
    else:
        mod_map = lambda b, i: (mod_row, 0, 0)
    out_shapes = (
        jax.ShapeDtypeStruct((b_, s_, 2 * FOURIER_WIDTH), BF16),
        jax.ShapeDtypeStruct((b_, SWA_HEADS, s_, HEAD_DIM), BF16),
        jax.ShapeDtypeStruct((b_, SWA_KV_HEADS, s_, HEAD_DIM), BF16),
        jax.ShapeDtypeStruct((b_, SWA_KV_HEADS, s_, HEAD_DIM), BF16),
        jax.ShapeDtypeStruct((b_, MLA_HEADS * MLA_PAD, s_), BF16),
        jax.ShapeDtypeStruct((b_, MLA_HEADS, s_, MLA_PAD), BF16),
        jax.ShapeDtypeStruct((b_, MLA_HEADS, s_ // tm, MLA_V_DIM, tm), BF16),
    )
    out_specs = (
        pl.BlockSpec((1, tm, 2 * FOURIER_WIDTH), lambda b, i: (b, i, 0)),
        pl.BlockSpec((1, SWA_HEADS, tm, HEAD_DIM), lambda b, i: (b, 0, i, 0)),
        pl.BlockSpec((1, SWA_KV_HEADS, tm, HEAD_DIM), lambda b, i: (b, 0, i, 0)),
        pl.BlockSpec((1, SWA_KV_HEADS, tm, HEAD_DIM), lambda b, i: (b, 0, i, 0)),
        pl.BlockSpec((1, MLA_HEADS * MLA_PAD, tm), lambda b, i: (b, 0, i)),
        pl.BlockSpec((1, MLA_HEADS, tm, MLA_PAD), lambda b, i: (b, 0, i, 0)),
        pl.BlockSpec((1, MLA_HEADS, 1, MLA_V_DIM, tm), lambda b, i: (b, 0, i, 0, 0)),
    )
    return pl.pallas_call(
        _proj_kernel,
        grid=(b_, s_ // tm),
        in_specs=[
            pl.BlockSpec((1, tm, D_MODEL), lambda b, i: (b, i, 0)),
            pl.BlockSpec((1, 1, 6 * D_MODEL), mod_map),
            pl.BlockSpec((1, D_MODEL), const),
            pl.BlockSpec((D_MODEL, W1_COLS), const),
            pl.BlockSpec((tm, 4 * LANE), lambda b, i: (i, 0)),
            pl.BlockSpec((2 * LANE, tm), lambda b, i: (0, i)),
            pl.BlockSpec((FOURIER_WIDTH, 2 * FOURIER_WIDTH), const),
            pl.BlockSpec((1, MLA_Q_RANK), const),
            pl.BlockSpec((MLA_HEADS * MLA_PAD, MLA_Q_RANK), const),
            pl.BlockSpec((MLA_HEADS * MLA_PAD, MLA_Q_RANK), const),
            pl.BlockSpec((1, MLA_KV_RANK), const),
            pl.BlockSpec((MLA_KV_RANK, MLA_HEADS * MLA_PAD), const),
            pl.BlockSpec((MLA_HEADS * MLA_V_DIM, MLA_KV_RANK), const),
        ],
        out_specs=out_specs,
        out_shape=out_shapes,
        compiler_params=_cparams(("parallel", "parallel")),
        name="norm_in_proj",
    )(h, mod3, g1, w1, rope, ropet, wcs, qn, wuq, wuqr, kvn, wukvk, wukvv)


def _dft_kernel(c_ref, s_ref, uv_ref, o_ref):
    uv = uv_ref[0]
    o = jnp.dot(c_ref[...], uv[:, :FOURIER_WIDTH], preferred_element_type=F32)
    o = o + jnp.dot(s_ref[...], uv[:, FOURIER_WIDTH:], preferred_element_type=F32)
    o_ref[0] = o.astype(o_ref.dtype)


def _dft_call(ctab, stab, uv, tk):
    b_, s_, _ = uv.shape
    return pl.pallas_call(
        _dft_kernel,
        grid=(s_ // tk, b_),
        in_specs=[
            pl.BlockSpec((tk, s_), lambda i, b: (i, 0)),
            pl.BlockSpec((tk, s_), lambda i, b: (i, 0)),
            pl.BlockSpec((1, s_, 2 * FOURIER_WIDTH), lambda i, b: (b, 0, 0)),
        ],
        out_specs=pl.BlockSpec((1, tk, FOURIER_WIDTH), lambda i, b: (b, i, 0)),
        out_shape=jax.ShapeDtypeStruct((b_, s_, FOURIER_WIDTH), BF16),
        compiler_params=_cparams(("parallel", "arbitrary")),
        name="fourier_dft",
    )(ctab, stab, uv)


def _mla_kernel(*refs, n_chunks, heads_per_step):
    hps = heads_per_step
    n_in = 5 if n_chunks else 3
    if n_chunks:
        qt_ref, k_ref, vt_ref, kc_ref, vct_ref = refs[:n_in]
    else:
        qt_ref, kc_ref, vct_ref = refs[:n_in]
    o_ref = refs[n_in]
    s_refs = [refs[n_in + 1 + 2 * hh:n_in + 3 + 2 * hh] for hh in range(hps)]
    tq = qt_ref.shape[2]
    lc = kc_ref.shape[2]
    qts = [qt_ref[0, MLA_PAD * hh:MLA_PAD * (hh + 1), :] for hh in range(hps)]

    def logits(ks, slot, rows):
        cms = []
        for hh in range(hps):
            s = jnp.dot(ks(hh), qts[hh], preferred_element_type=F32)
            s_refs[hh][slot][0:rows, :] = s
            cms.append(jnp.max(s, axis=0, keepdims=True))
        return tuple(cms)

    def accumulate(slot, rows, vts, carries, cms):
        out = []
        for hh in range(hps):
            m, l, acc = carries[hh]
            m_new = jnp.maximum(m, cms[hh])
            alpha = jnp.exp2(m - m_new)
            p = jnp.exp2(s_refs[hh][slot][0:rows, :] - m_new)
            l = alpha * l + jnp.sum(p, axis=0, keepdims=True)
            acc = alpha * acc + jnp.dot(vts(hh), p.astype(BF16), preferred_element_type=F32)
            out.append((m_new, l, acc))
        return tuple(out)

    def k_lat(c):
        return lambda hh: k_ref[0, hh, pl.ds(pl.multiple_of(c * MLA_TK, MLA_TK), MLA_TK), :]

    def v_lat(c):
        return lambda hh: vt_ref[0, hh, c]

    k_ctx = lambda hh: kc_ref[0, hh]
    v_ctx = lambda hh: vct_ref[0, hh, 0]

    init = (jnp.full((1, tq), NEG_INF, F32), jnp.zeros((1, tq), F32), jnp.zeros((MLA_V_DIM, tq), F32))
    carries = tuple(init for _ in range(hps))
    cms = logits(k_ctx, 0, lc)
    if n_chunks:
        assert n_chunks % 2 == 0
        cms1 = logits(k_lat(0), 1, MLA_TK)
        carries = accumulate(0, lc, v_ctx, carries, cms)

        def body(j, state):
            carries, cms1 = state
            c = 2 * j + 1
            cms0 = logits(k_lat(c), 0, MLA_TK)
            carries = accumulate(1, MLA_TK, v_lat(c - 1), carries, cms1)
            cms1 = logits(k_lat(c + 1), 1, MLA_TK)
            carries = accumulate(0, MLA_TK, v_lat(c), carries, cms0)
            return carries, cms1

        carries, cms1 = lax.fori_loop(0, (n_chunks - 2) // 2, body, (carries, cms1))
        cms0 = logits(k_lat(n_chunks - 1), 0, MLA_TK)
        carries = accumulate(1, MLA_TK, v_lat(n_chunks - 2), carries, cms1)
        carries = accumulate(0, MLA_TK, v_lat(n_chunks - 1), carries, cms0)
    else:
        carries = accumulate(0, lc, v_ctx, carries, cms)
    o = jnp.concatenate([acc / l for (_, l, acc) in carries], axis=0)
    o_ref[0] = o.T.astype(o_ref.dtype)


def _mla_call(qt, k, vt, kc, vct, tq):
    b_, _, sq = qt.shape
    h_ = kc.shape[1]
    lc = kc.shape[2]
    hps = 2
    n_chunks = 0 if k is None else k.shape[2] // MLA_TK
    in_specs = [pl.BlockSpec((1, hps * MLA_PAD, tq), lambda b, p, i: (b, p, i))]
    args = [qt]
    if k is not None:
        sk = k.shape[2]
        in_specs += [pl.BlockSpec((1, hps, sk, MLA_PAD), lambda b, p, i: (b, p, 0, 0)),
                     pl.BlockSpec((1, hps, n_chunks, MLA_V_DIM, MLA_TK), lambda b, p, i: (b, p, 0, 0, 0))]
        args += [k, vt]
    in_specs += [pl.BlockSpec((1, hps, lc, MLA_PAD), lambda b, p, i: (b, p, 0, 0)),
                 pl.BlockSpec((1, hps, 1, MLA_V_DIM, lc), lambda b, p, i: (b, p, 0, 0, 0))]
    args += [kc, vct]
    return pl.pallas_call(
        functools.partial(_mla_kernel, n_chunks=n_chunks, heads_per_step=hps),
        grid=(b_, h_ // hps, sq // tq),
        in_specs=in_specs,
        out_specs=pl.BlockSpec((1, tq, hps * MLA_V_DIM), lambda b, p, i: (b, i, p)),
        out_shape=jax.ShapeDtypeStruct((b_, sq, h_ * MLA_V_DIM), BF16),
        scratch_shapes=[pltpu.VMEM((MLA_TK if n_chunks else lc, tq), F32)] * (2 * hps),
        compiler_params=_cparams(("parallel", "parallel", "arbitrary")),
        name="mla_attn",
    )(*args)


def _swa_kernel(*refs, local, s_len):
    if local:
        sink_ref, q_ref, kp_ref, k0_ref, kn_ref, vp_ref, v0_ref, vn_ref, kc_ref, vc_ref, o_ref = refs
    else:
        sink_ref, q_ref, kc_ref, vc_ref, o_ref = refs
    tq = q_ref.shape[2]
    rows = SWA_GROUP * tq
    outs = []
    row = lax.broadcasted_iota(jnp.int32, (rows, 1), 0)
    if local:
        n = pl.program_id(1)
        nk = 3 * BLOCK
        assert tq & (tq - 1) == 0
        qpos = n * BLOCK + (lax.broadcasted_iota(jnp.int32, (rows, nk), 0) & (tq - 1))
        kpos = (n - 1) * BLOCK + lax.broadcasted_iota(jnp.int32, (rows, nk), 1)
        valid = (jnp.abs(kpos - qpos) <= WINDOW) & (kpos >= 0) & (kpos < s_len)
    for kh in range(SWA_KV_HEADS):
        q = q_ref[0, SWA_GROUP * kh:SWA_GROUP * (kh + 1)].reshape(rows, HEAD_DIM)
        sink = jnp.full((rows, 1), sink_ref[SWA_GROUP * kh], F32)
        for g in range(1, SWA_GROUP):
            sink = jnp.where(row >= g * tq, sink_ref[SWA_GROUP * kh + g], sink)
        kc = kc_ref[0, kh]
        vc = vc_ref[0, kh]
        s_c = lax.dot_general(q, kc, _NT, preferred_element_type=F32)
        m = jnp.maximum(jnp.max(s_c, axis=-1, keepdims=True), sink)
        if local:
            kl = jnp.concatenate([kp_ref[0, kh], k0_ref[0, kh], kn_ref[0, kh]], axis=0)
            vl = jnp.concatenate([vp_ref[0, kh], v0_ref[0, kh], vn_ref[0, kh]], axis=0)
            s_l = lax.dot_general(q, kl, _NT, preferred_element_type=F32)
            s_l = jnp.where(valid, s_l, NEG_INF)
            m = jnp.maximum(m, jnp.max(s_l, axis=-1, keepdims=True))
        p_c = jnp.exp(s_c - m)
        l = jnp.sum(p_c, axis=-1, keepdims=True) + jnp.exp(sink - m)
        o = jnp.dot(p_c.astype(BF16), vc, preferred_element_type=F32)
        if local:
            p_l = jnp.exp(s_l - m)
            l = l + jnp.sum(p_l, axis=-1, keepdims=True)
            o = o + jnp.dot(p_l.astype(BF16), vl, preferred_element_type=F32)
        o = o / l
        for g in range(SWA_GROUP):
            outs.append(o[g * tq:(g + 1) * tq])
    o_ref[0] = jnp.concatenate(outs, axis=-1).astype(o_ref.dtype)


def _swa_call(sink, q, k, v, kc, vc):
    b_, _, sq, _ = q.shape
    lc = kc.shape[2]
    local = k is not None
    tq = BLOCK if local else sq
    nb = sq // tq
    in_specs = [pl.BlockSpec(memory_space=pltpu.SMEM),
                pl.BlockSpec((1, SWA_HEADS, tq, HEAD_DIM), lambda b, n: (b, 0, n, 0))]
    args = [sink, q]
    if local:
        prev = lambda b, n: (b, 0, jnp.maximum(n - 1, 0), 0)
        cur = lambda b, n: (b, 0, n, 0)
        nxt = lambda b, n: (b, 0, jnp.minimum(n + 1, nb - 1), 0)
        blk = (1, SWA_KV_HEADS, BLOCK, HEAD_DIM)
        in_specs += [pl.BlockSpec(blk, prev), pl.BlockSpec(blk, cur), pl.BlockSpec(blk, nxt)] * 2
        args += [k, k, k, v, v, v]
    cblk = (1, SWA_KV_HEADS, lc, HEAD_DIM)
    in_specs += [pl.BlockSpec(cblk, lambda b, n: (b, 0, 0, 0))] * 2
    args += [kc, vc]
    return pl.pallas_call(
        functools.partial(_swa_kernel, local=local, s_len=sq),
        grid=(b_, nb),
        in_specs=in_specs,
        out_specs=pl.BlockSpec((1, tq, SWA_Q_WIDTH), lambda b, n: (b, n, 0)),
        out_shape=jax.ShapeDtypeStruct((b_, sq, SWA_Q_WIDTH), BF16),
        compiler_params=_cparams(("parallel", "arbitrary")),
        name="swa_attn",
    )(*args)


def _mlp_kernel(fo_ref, so_ref, mo_ref, h_ref, mod_ref, g2_ref, wo_ref, w1_ref, w2_ref, gf_ref, o_ref, *,
                final, ff_chunk):
    mod = mod_ref[0]
    g1 = mod[:, 2 * D_MODEL:3 * D_MODEL]
    sh2 = mod[:, 3 * D_MODEL:4 * D_MODEL]
    sc2 = mod[:, 4 * D_MODEL:5 * D_MODEL]
    g2 = mod[:, 5 * D_MODEL:6 * D_MODEL]
    mix = jnp.concatenate([fo_ref[0], so_ref[0], mo_ref[0]], axis=-1)
    h = h_ref[0] + g1 * jnp.dot(mix, wo_ref[...], preferred_element_type=F32)
    y = (_rms(h, g2_ref[...]) * (1.0 + sc2) + sh2).astype(BF16)
    acc = None
    for c in range(D_FF // ff_chunk):
        a = jnp.dot(y, w1_ref[:, c * ff_chunk:(c + 1) * ff_chunk], preferred_element_type=F32)
        a = jnp.maximum(a, 0.0)
        a = (a * a).astype(BF16)
        part = jnp.dot(a, w2_ref[c * ff_chunk:(c + 1) * ff_chunk, :], preferred_element_type=F32)
        acc = part if acc is None else acc + part
    h = h + g2 * acc
    if final:
        h = _rms(h, gf_ref[...])
    o_ref[0] = h


def _mlp_call(fo, so, mo, h, mod3, mod_row, g2, wo, w1, w2, gf, tm, final):
    b_, s_, _ = h.shape
    const = lambda b, i: (0, 0)
    tile = lambda b, i: (b, i, 0)
    if mod_row is None:
        mod_map = lambda b, i: (b, 0, 0)
    else:
        mod_map = lambda b, i: (mod_row, 0, 0)
    once = pl.Buffered(1)
    return pl.pallas_call(
        functools.partial(_mlp_kernel, final=final, ff_chunk=1024),
        grid=(b_, s_ // tm),
        in_specs=[
            pl.BlockSpec((1, tm, FOURIER_WIDTH), tile),
            pl.BlockSpec((1, tm, SWA_Q_WIDTH), tile),
            pl.BlockSpec((1, tm, MLA_HEADS * MLA_V_DIM), tile),
            pl.BlockSpec((1, tm, D_MODEL), tile),
            pl.BlockSpec((1, 1, 6 * D_MODEL), mod_map),
            pl.BlockSpec((1, D_MODEL), const),
            pl.BlockSpec((D_MIX, D_MODEL), const, pipeline_mode=once),
            pl.BlockSpec((D_MODEL, D_FF), const, pipeline_mode=once),
            pl.BlockSpec((D_FF, D_MODEL), const, pipeline_mode=once),
            pl.BlockSpec((1, D_MODEL), const),
        ],
        out_specs=pl.BlockSpec((1, tm, D_MODEL), tile),
        out_shape=jax.ShapeDtypeStruct((b_, s_, D_MODEL), F32),
        compiler_params=_cparams(("parallel", "parallel")),
        name="out_proj_mlp",
    )(fo, so, mo, h, mod3, g2, wo, w1, w2, gf)


def _rot_cols(w, half):
    return jnp.concatenate([-w[:, half:], w[:, :half]], axis=1)


def _pack_w_in(w_in):
    z = lambda n: jnp.zeros((D_MODEL, n), F32)
    wq = w_in[:, OFF_SWA_Q:OFF_SWA_K]
    wk = w_in[:, OFF_SWA_K:OFF_SWA_V]
    wkr = w_in[:, OFF_MLA_KR:OFF_MLA_KR + MLA_ROPE_DIM]
    half = HEAD_DIM // 2
    wq_rot = jnp.concatenate(
        [_rot_cols(wq[:, HEAD_DIM * i:HEAD_DIM * (i + 1)], half) for i in range(SWA_HEADS)], axis=1)
    wk_rot = jnp.concatenate(
        [_rot_cols(wk[:, HEAD_DIM * i:HEAD_DIM * (i + 1)], half) for i in range(SWA_KV_HEADS)], axis=1)
    pad_kr = lambda w: jnp.concatenate([z(MLA_NOPE_DIM), w, z(LANE - MLA_NOPE_DIM - MLA_ROPE_DIM)], axis=1)
    cols = [w_in[:, :OFF_MLA_KR], pad_kr(wkr), wq_rot, wk_rot, pad_kr(_rot_cols(wkr, MLA_ROPE_DIM // 2))]
    return jnp.concatenate(cols, axis=1).astype(BF16)


def _pack_w_uq(w_uq):
    per = MLA_NOPE_DIM + MLA_ROPE_DIM
    z = jnp.zeros((MLA_Q_RANK, MLA_PAD - per), F32)
    zn = jnp.zeros((MLA_Q_RANK, MLA_NOPE_DIM), F32)
    main, rot = [], []
    for hh in range(MLA_HEADS):
        w = w_uq[:, per * hh:per * (hh + 1)]
        main += [w, z]
        rot += [zn, _rot_cols(w[:, MLA_NOPE_DIM:], MLA_ROPE_DIM // 2), z]
    return jnp.concatenate(main, axis=1).T.astype(BF16), jnp.concatenate(rot, axis=1).T.astype(BF16)


def _pack_w_ukv(w_ukv):
    per = MLA_NOPE_DIM + MLA_V_DIM
    z = jnp.zeros((MLA_KV_RANK, MLA_PAD - MLA_NOPE_DIM), F32)
    ks, vs = [], []
    for hh in range(MLA_HEADS):
        w = w_ukv[:, per * hh:per * (hh + 1)]
        ks += [w[:, :MLA_NOPE_DIM], z]
        vs.append(w[:, MLA_NOPE_DIM:])
    return jnp.concatenate(ks, axis=1).astype(BF16), jnp.concatenate(vs, axis=1).T.astype(BF16)


def _rope_table(rows):
    r, col = jnp.meshgrid(jnp.arange(rows, dtype=F32), jnp.arange(GRID_W, dtype=F32), indexing="ij")
    r = r.reshape(-1)
    col = col.reshape(-1)

    def tables(dim):
        n_freq = dim // 4
        inv = ROPE_THETA ** (-jnp.arange(n_freq, dtype=F32) / n_freq)
        ang = jnp.concatenate([r[:, None] * inv[None, :], col[:, None] * inv[None, :]], axis=-1)
        return jnp.cos(ang), jnp.sin(ang)

    ch, sh = tables(HEAD_DIM)
    cr, sr = tables(MLA_ROPE_DIM)
    s_ = r.shape[0]
    ones = jnp.ones((s_, MLA_NOPE_DIM), F32)
    zn = jnp.zeros((s_, MLA_NOPE_DIM), F32)
    zp = jnp.zeros((s_, LANE - MLA_NOPE_DIM - MLA_ROPE_DIM), F32)
    return jnp.concatenate([ch, ch, ch, ch, sh, sh, sh, sh,
                            ones, cr, cr, zp, zn, sr, sr, zp], axis=-1)


def _identity_rope_table(n):
    ones = jnp.ones((n, LANE), F32)
    zeros = jnp.zeros((n, LANE), F32)
    m = jnp.concatenate([jnp.ones((n, MLA_NOPE_DIM + MLA_ROPE_DIM), F32),
                         jnp.zeros((n, LANE - MLA_NOPE_DIM - MLA_ROPE_DIM), F32)], axis=-1)
    return jnp.concatenate([ones, zeros, m, zeros], axis=-1)


def _dft_tables(n):
    k = jnp.arange(n, dtype=jnp.int32)
    ks = (k[:, None] * k[None, :]) % n
    ang = ks.astype(F32) * (2.0 * math.pi / n)
    scale = 1.0 / math.sqrt(n)
    return (jnp.cos(ang) * scale).astype(BF16), (-jnp.sin(ang) * scale).astype(BF16)


def _dft_tables_big(n, n1):
    n2 = n // n1
    k = jnp.arange(n, dtype=jnp.int32)
    a1 = ((k[:, None] * jnp.arange(n1, dtype=jnp.int32)[None, :]) % n1).astype(F32) * (2.0 * math.pi / n1)
    a2 = ((k[:, None] * jnp.arange(n2, dtype=jnp.int32)[None, :]) % n).astype(F32) * (2.0 * math.pi / n)
    c1, s1 = jnp.cos(a1)[:, :, None], jnp.sin(a1)[:, :, None]
    c2, s2 = jnp.cos(a2)[:, None, :], jnp.sin(a2)[:, None, :]
    scale = 1.0 / math.sqrt(n)
    ct = ((c1 * c2 - s1 * s2) * scale).reshape(n, n).astype(BF16)
    st = (-(s1 * c2 + c1 * s2) * scale).reshape(n, n).astype(BF16)
    return ct, st


def kernel(x, c, ctx, c_ctx, w_ada, b_ada, norm1_g, norm2_g, w_in, w_fourier, swa_sink, mla_q_norm, w_uq,
           mla_kv_norm, w_ukv, w_out, w_mlp1, w_mlp2, final_norm_g):
    b_, s_, _ = x.shape
    lc = ctx.shape[1]
    depth = w_ada.shape[0]
    rows = s_ // GRID_W

    mod_rows = 16
    cc = jnp.concatenate([c, c_ctx[None, :], jnp.zeros((mod_rows - b_ - 1, D_MODEL), F32)], axis=0)
    mod_all = _ada_call(cc, w_ada, b_ada)

    rope = _rope_table(rows)
    rope_c = _identity_rope_table(lc)
    ropet = rope[:, 2 * LANE:].T
    ropet_c = rope_c[:, 2 * LANE:].T
    ctab, stab = _dft_tables_big(s_, GRID_W)
    ctab_c, stab_c = _dft_tables(lc)
    kk = jnp.arange(FOURIER_GROUP_DIM, dtype=jnp.int32)
    ang64 = ((kk[:, None] * kk[None, :]) % FOURIER_GROUP_DIM).astype(F32) * (2.0 * math.pi / FOURIER_GROUP_DIM)
    c64, s64 = jnp.cos(ang64), jnp.sin(ang64)

    h, hc = x, ctx
    for l in range(depth):
        last = l == depth - 1
        mod3 = mod_all[l].reshape(mod_rows, 1, 6 * D_MODEL)
        g1 = norm1_g[l].reshape(1, D_MODEL)
        g2 = norm2_g[l].reshape(1, D_MODEL)
        gf = final_norm_g.reshape(1, D_MODEL)
        w1 = _pack_w_in(w_in[l])
        wuq, wuqr = _pack_w_uq(w_uq[l])
        wukvk, wukvv = _pack_w_ukv(w_ukv[l])
        wcs = _fw_call(c64, s64, w_fourier[l]).astype(BF16)
        qn = mla_q_norm[l].reshape(1, MLA_Q_RANK)
        kvn = mla_kv_norm[l].reshape(1, MLA_KV_RANK)
        wo = w_out[l].astype(BF16)
        wm1 = w_mlp1[l].astype(BF16)
        wm2 = w_mlp2[l].astype(BF16)
        sink = swa_sink[l]

        uv, sq, sk, sv, mq, mk, mv = _proj_call(h, mod3, None, g1, w1, rope, ropet, wcs, qn, wuq, wuqr, kvn,
                                                wukvk, wukvv, tm=MLA_TK)
        uvc, sqc, skc, svc, mqc, mkc, mvc = _proj_call(hc, mod3, b_, g1, w1, rope_c, ropet_c, wcs, qn, wuq, wuqr,
                                                       kvn, wukvk, wukvv, tm=lc)

        fo = _dft_call(ctab, stab, uv, tk=512)
        so = _swa_call(sink, sq, sk, sv, skc, svc)
        mo = _mla_call(mq, mk, mv, mkc, mvc, tq=256)
        h = _mlp_call(fo, so, mo, h, mod3, None, g2, wo, wm1, wm2, gf, tm=512, final=last)

        if not last:
            foc = _dft_call(ctab_c, stab_c, uvc, tk=lc)
            soc = _swa_call(sink, sqc, None, None, skc, svc)
            moc = _mla_call(mqc, None, None, mkc, mvc, tq=lc)
            hc = _mlp_call(foc, soc, moc, hc, mod3, b_, g2, wo, wm1, wm2, gf, tm=lc, final=False)
    return h
```

```python
import functools
import math

import jax
import jax.numpy as jnp
from jax import lax
from jax.experimental import pallas as pl
from jax.experimental.pallas import tpu as pltpu

D_MODEL = 1024
GRID_W = 64
HEAD_DIM = 64
FOURIER_GROUPS = 4
FOURIER_GROUP_DIM = 64
FOURIER_WIDTH = FOURIER_GROUPS * FOURIER_GROUP_DIM
SWA_HEADS = 6
SWA_KV_HEADS = 2
SWA_GROUP = SWA_HEADS // SWA_KV_HEADS
SWA_Q_WIDTH = SWA_HEADS * HEAD_DIM
SWA_KV_WIDTH = SWA_KV_HEADS * HEAD_DIM
WINDOW = 128
BLOCK = 128
MLA_HEADS = 6
MLA_NOPE_DIM = 64
MLA_ROPE_DIM = 32
MLA_V_DIM = 64
MLA_Q_RANK = 256
MLA_KV_RANK = 128
MLA_SCALE = (MLA_NOPE_DIM + MLA_ROPE_DIM) ** -0.5
SWA_SCALE = HEAD_DIM ** -0.5
D_MIX = FOURIER_WIDTH + SWA_Q_WIDTH + MLA_HEADS * MLA_V_DIM
OFF_SWA_Q = FOURIER_WIDTH
OFF_SWA_K = OFF_SWA_Q + SWA_Q_WIDTH
OFF_SWA_V = OFF_SWA_K + SWA_KV_WIDTH
OFF_MLA_CQ = OFF_SWA_V + SWA_KV_WIDTH
OFF_MLA_CKV = OFF_MLA_CQ + MLA_Q_RANK
OFF_MLA_KR = OFF_MLA_CKV + MLA_KV_RANK
D_FF = 4 * D_MODEL
ROPE_THETA = 10000.0
NORM_EPS = 1e-6
NEG_INF = -1e30

LANE = 128
MLA_PAD = LANE

C_F = 0
C_K = C_F + FOURIER_WIDTH
C_CQ = C_K + SWA_KV_WIDTH
C_CKV = C_CQ + MLA_Q_RANK
C_KR = C_CKV + MLA_KV_RANK
C_KROT = C_KR + LANE
C_KRROT = C_KROT + SWA_KV_WIDTH
W1_COLS = C_KRROT + LANE
R_Q = 0
R_QROT = R_Q + SWA_Q_WIDTH
R_V = R_QROT + SWA_Q_WIDTH
W1T_ROWS = R_V + SWA_KV_WIDTH

BF16 = jnp.bfloat16
F32 = jnp.float32
LOG2E = math.log2(math.e)
_NT = (((1,), (1,)), ((), ()))
MLA_TK = 512
MLA_VT_ROWS = MLA_V_DIM + 16
SWA_VT_ROWS = HEAD_DIM + 16
SWA_TQ = 256

VMEM_LIMIT = 56 * 1024 * 1024


def _cparams(sem):
    return pltpu.CompilerParams(dimension_semantics=sem, vmem_limit_bytes=VMEM_LIMIT)


def _rms(x, g):
    ms = jnp.mean(x * x, axis=-1, keepdims=True)
    return x * lax.rsqrt(ms + NORM_EPS) * g


def _ada_kernel(cc_ref, w_ref, b_ref, o_ref):
    cc = cc_ref[...]
    s = cc * jax.nn.sigmoid(cc)
    o_ref[0] = jnp.dot(s, w_ref[0], preferred_element_type=F32,
                       precision=lax.Precision.HIGHEST) + b_ref[0]


def _ada_call(cc, w_ada, b_ada):
    depth = w_ada.shape[0]
    rows = cc.shape[0]
    tn = 1536
    n_out = w_ada.shape[2]
    return pl.pallas_call(
        _ada_kernel,
        grid=(depth, n_out // tn),
        in_specs=[
            pl.BlockSpec((rows, D_MODEL), lambda l, j: (0, 0)),
            pl.BlockSpec((1, D_MODEL, tn), lambda l, j: (l, 0, j)),
            pl.BlockSpec((1, 1, tn), lambda l, j: (l, 0, j)),
        ],
        out_specs=pl.BlockSpec((1, rows, tn), lambda l, j: (l, 0, j)),
        out_shape=jax.ShapeDtypeStruct((depth, rows, n_out), F32),
        compiler_params=_cparams(("arbitrary", "arbitrary")),
        name="ada_mod",
    )(cc, w_ada, b_ada.reshape(depth, 1, n_out))


def _fw_kernel(c_ref, s_ref, w_ref, o_ref):
    o_ref[...] = jnp.zeros(o_ref.shape, o_ref.dtype)
    inv = 1.0 / math.sqrt(FOURIER_GROUP_DIM)
    for g in range(FOURIER_GROUPS):
        w = w_ref[g]
        a = jnp.dot(c_ref[...], w, preferred_element_type=F32, precision=lax.Precision.HIGHEST) * inv
        b = jnp.dot(s_ref[...], w, preferred_element_type=F32, precision=lax.Precision.HIGHEST) * inv
        lo = g * FOURIER_GROUP_DIM
        hi = lo + FOURIER_GROUP_DIM
        o_ref[lo:hi, lo:hi] = a
        o_ref[lo:hi, FOURIER_WIDTH + lo:FOURIER_WIDTH + hi] = b


def _fw_call(c64, s64, w_f):
    return pl.pallas_call(
        _fw_kernel,
        out_shape=jax.ShapeDtypeStruct((FOURIER_WIDTH, 2 * FOURIER_WIDTH), F32),
        name="fourier_w",
    )(c64, s64, w_f)


def _ones_row(rows, cols):
    return (lax.broadcasted_iota(jnp.int32, (rows, cols), 0) == 0).astype(BF16)


def _proj_kernel(h_ref, mod_ref, g1_ref, w1_ref, w1t_ref, rope_ref, ropet_ref, wcs_ref, qn_ref, wuq_ref,
                 wuqr_ref, kvn_ref, wukvk_ref, wukvv_ref,
                 u_o, v_o, sq_o, sk_o, sv_o, mq_o, mk_o, mv_o):
    x = h_ref[0]
    tm = x.shape[0]
    mod = mod_ref[0]
    sh = mod[:, 0:D_MODEL]
    sc = mod[:, D_MODEL:2 * D_MODEL]
    y = (_rms(x, g1_ref[...]) * (1.0 + sc) + sh).astype(BF16)
    u = jnp.dot(y, w1_ref[...], preferred_element_type=F32)
    ut = lax.dot_general(w1t_ref[...], y, _NT, preferred_element_type=F32)

    cos_h = rope_ref[:, 0:LANE]
    sin_h = rope_ref[:, LANE:2 * LANE]
    cos_m = rope_ref[:, 2 * LANE:3 * LANE]
    sin_m = rope_ref[:, 3 * LANE:4 * LANE]
    cos_mt = ropet_ref[0:LANE, :]
    sin_mt = ropet_ref[LANE:2 * LANE, :]
    cos_ht = ropet_ref[2 * LANE:3 * LANE, :]
    sin_ht = ropet_ref[3 * LANE:4 * LANE, :]

    f = u[:, C_F:C_F + FOURIER_WIDTH].astype(BF16)
    uv = jnp.dot(f, wcs_ref[...], preferred_element_type=F32).astype(BF16)
    u_o[0] = uv[:, :FOURIER_WIDTH]
    v_o[0] = uv[:, FOURIER_WIDTH:]

    for j in range(SWA_Q_WIDTH // LANE):
        sl = slice(LANE * j, LANE * (j + 1))
        qj = ut[R_Q:R_QROT][sl] * cos_ht + ut[R_QROT:R_V][sl] * sin_ht
        sq_o[0, sl, :] = (qj * (SWA_SCALE * LOG2E)).astype(BF16)
    kk = (u[:, C_K:C_K + LANE] * cos_h + u[:, C_KROT:C_KROT + LANE] * sin_h).astype(BF16)
    ones_swa = _ones_row(SWA_VT_ROWS - HEAD_DIM, tm)
    for kh in range(SWA_KV_HEADS):
        sk_o[0, kh] = kk[:, HEAD_DIM * kh:HEAD_DIM * (kh + 1)]
        sv_o[0, kh, 0:HEAD_DIM, :] = ut[R_V + HEAD_DIM * kh:R_V + HEAD_DIM * (kh + 1)].astype(BF16)
        sv_o[0, kh, HEAD_DIM:SWA_VT_ROWS, :] = ones_swa

    cqn = _rms(u[:, C_CQ:C_CQ + MLA_Q_RANK], qn_ref[...]).astype(BF16)
    qa = lax.dot_general(wuq_ref[...], cqn, _NT, preferred_element_type=F32)
    qr = lax.dot_general(wuqr_ref[...], cqn, _NT, preferred_element_type=F32)
    for hh in range(MLA_HEADS):
        sl = slice(MLA_PAD * hh, MLA_PAD * (hh + 1))
        mq_o[0, sl, :] = ((qa[sl] * cos_mt + qr[sl] * sin_mt) * (MLA_SCALE * LOG2E)).astype(BF16)

    ckvn = _rms(u[:, C_CKV:C_CKV + MLA_KV_RANK], kvn_ref[...]).astype(BF16)
    kr = u[:, C_KR:C_KR + LANE] * cos_m + u[:, C_KRROT:C_KRROT + LANE] * sin_m
    kvk = jnp.dot(ckvn, wukvk_ref[...], preferred_element_type=F32)
    vt = lax.dot_general(wukvv_ref[...], ckvn, _NT, preferred_element_type=F32).astype(BF16)
    ones_mla = _ones_row(MLA_VT_ROWS - MLA_V_DIM, tm)
    for hh in range(MLA_HEADS):
        mk_o[0, hh] = (kvk[:, MLA_PAD * hh:MLA_PAD * (hh + 1)] + kr).astype(BF16)
        mv_o[0, hh, 0, 0:MLA_V_DIM, :] = vt[MLA_V_DIM * hh:MLA_V_DIM * (hh + 1)]
        mv_o[0, hh, 0, MLA_V_DIM:MLA_VT_ROWS, :] = ones_mla


def _proj_call(h, mod3, mod_row, g1, w1, w1t, rope, ropet, wcs, qn, wuq, wuqr, kvn, wukvk, wukvv, tm):
    b_, s_, _ = h.shape
    const = lambda b, i: (0, 0)
    if mod_row is None:
        mod_map = lambda b, i: (b, 0, 0)
    else:
        mod_map = lambda b, i: (mod_row, 0, 0)
    out_shapes = (
        jax.ShapeDtypeStruct((b_, s_, FOURIER_WIDTH), BF16),
        jax.ShapeDtypeStruct((b_, s_, FOURIER_WIDTH), BF16),
        jax.ShapeDtypeStruct((b_, SWA_Q_WIDTH, s_), BF16),
        jax.ShapeDtypeStruct((b_, SWA_KV_HEADS, s_, HEAD_DIM), BF16),
        jax.ShapeDtypeStruct((b_, SWA_KV_HEADS, SWA_VT_ROWS, s_), BF16),
        jax.ShapeDtypeStruct((b_, MLA_HEADS * MLA_PAD, s_), BF16),
        jax.ShapeDtypeStruct((b_, MLA_HEADS, s_, MLA_PAD), BF16),
        jax.ShapeDtypeStruct((b_, MLA_HEADS, s_ // tm, MLA_VT_ROWS, tm), BF16),
    )
    out_specs = (
        pl.BlockSpec((1, tm, FOURIER_WIDTH), lambda b, i: (b, i, 0)),
        pl.BlockSpec((1, tm, FOURIER_WIDTH), lambda b, i: (b, i, 0)),
        pl.BlockSpec((1, SWA_Q_WIDTH, tm), lambda b, i: (b, 0, i)),
        pl.BlockSpec((1, SWA_KV_HEADS, tm, HEAD_DIM), lambda b, i: (b, 0, i, 0)),
        pl.BlockSpec((1, SWA_KV_HEADS, SWA_VT_ROWS, tm), lambda b, i: (b, 0, 0, i)),
        pl.BlockSpec((1, MLA_HEADS * MLA_PAD, tm), lambda b, i: (b, 0, i)),
        pl.BlockSpec((1, MLA_HEADS, tm, MLA_PAD), lambda b, i: (b, 0, i, 0)),
        pl.BlockSpec((1, MLA_HEADS, 1, MLA_VT_ROWS, tm), lambda b, i: (b, 0, i, 0, 0)),
    )
    return pl.pallas_call(
        _proj_kernel,
        grid=(b_, s_ // tm),
        in_specs=[
            pl.BlockSpec((1, tm, D_MODEL), lambda b, i: (b, i, 0)),
            pl.BlockSpec((1, 1, 6 * D_MODEL), mod_map),
            pl.BlockSpec((1, D_MODEL), const),
            pl.BlockSpec((D_MODEL, W1_COLS), const),
            pl.BlockSpec((W1T_ROWS, D_MODEL), const),
            pl.BlockSpec((tm, 4 * LANE), lambda b, i: (i, 0)),
            pl.BlockSpec((4 * LANE, tm), lambda b, i: (0, i)),
            pl.BlockSpec((FOURIER_WIDTH, 2 * FOURIER_WIDTH), const),
            pl.BlockSpec((1, MLA_Q_RANK), const),
            pl.BlockSpec((MLA_HEADS * MLA_PAD, MLA_Q_RANK), const),
            pl.BlockSpec((MLA_HEADS * MLA_PAD, MLA_Q_RANK), const),
            pl.BlockSpec((1, MLA_KV_RANK), const),
            pl.BlockSpec((MLA_KV_RANK, MLA_HEADS * MLA_PAD), const),
            pl.BlockSpec((MLA_HEADS * MLA_V_DIM, MLA_KV_RANK), const),
        ],
        out_specs=out_specs,
        out_shape=out_shapes,
        compiler_params=_cparams(("parallel", "parallel")),
        name="norm_in_proj",
    )(h, mod3, g1, w1, w1t, rope, ropet, wcs, qn, wuq, wuqr, kvn, wukvk, wukvv)


def _dft_kernel(c_ref, s_ref, u_ref, v_ref, o_ref):
    o = jnp.dot(c_ref[...], u_ref[0], preferred_element_type=F32)
    o = o + jnp.dot(s_ref[...], v_ref[0], preferred_element_type=F32)
    o_ref[0] = o.astype(o_ref.dtype)


def _dft_call(ctab, stab, u, v, tk):
    b_, s_, _ = u.shape
    whole = pl.BlockSpec((1, s_, FOURIER_WIDTH), lambda i, b: (b, 0, 0))
    return pl.pallas_call(
        _dft_kernel,
        grid=(s_ // tk, b_),
        in_specs=[
            pl.BlockSpec((tk, s_), lambda i, b: (i, 0)),
            pl.BlockSpec((tk, s_), lambda i, b: (i, 0)),
            whole, whole,
        ],
        out_specs=pl.BlockSpec((1, tk, FOURIER_WIDTH), lambda i, b: (b, i, 0)),
        out_shape=jax.ShapeDtypeStruct((b_, s_, FOURIER_WIDTH), BF16),
        compiler_params=_cparams(("parallel", "arbitrary")),
        name="fourier_dft",
    )(ctab, stab, u, v)


def _mla_kernel(*refs, n_chunks, heads_per_step):
    hps = heads_per_step
    n_in = 5 if n_chunks else 3
    if n_chunks:
        qt_ref, k_ref, vt_ref, kc_ref, vct_ref = refs[:n_in]
    else:
        qt_ref, kc_ref, vct_ref = refs[:n_in]
    o_ref = refs[n_in]
    s_refs = [refs[n_in + 1 + 2 * hh:n_in + 3 + 2 * hh] for hh in range(hps)]
    tq = qt_ref.shape[2]
    lc = kc_ref.shape[2]
    qts = [qt_ref[0, MLA_PAD * hh:MLA_PAD * (hh + 1), :] for hh in range(hps)]

    def logits(ks, slot, rows):
        cms = []
        for hh in range(hps):
            s = jnp.dot(ks(hh), qts[hh], preferred_element_type=F32)
            s_refs[hh][slot][0:rows, :] = s
            cms.append(jnp.max(s, axis=0, keepdims=True))
        return tuple(cms)

    def accumulate(slot, rows, vts, carries, cms):
        out = []
        for hh in range(hps):
            m, acc = carries[hh]
            m_new = jnp.maximum(m, cms[hh])
            alpha = jnp.exp2(m - m_new)
            p = jnp.exp2(s_refs[hh][slot][0:rows, :] - m_new).astype(BF16)
            acc = alpha * acc + jnp.dot(vts(hh), p, preferred_element_type=F32)
            out.append((m_new, acc))
        return tuple(out)

    def k_lat(c):
        return lambda hh: k_ref[0, hh, c * MLA_TK:(c + 1) * MLA_TK, :]

    def v_lat(c):
        return lambda hh: vt_ref[0, hh, c]

    k_ctx = lambda hh: kc_ref[0, hh]
    v_ctx = lambda hh: vct_ref[0, hh, 0]

    init = (jnp.full((1, tq), NEG_INF, F32), jnp.zeros((MLA_VT_ROWS, tq), F32))
    carries = tuple(init for _ in range(hps))
    cms = logits(k_ctx, 0, lc)
    if n_chunks:
        prev_cms, prev_rows, prev_v = cms, lc, v_ctx
        for c in range(n_chunks):
            new_cms = logits(k_lat(c), (c + 1) % 2, MLA_TK)
            carries = accumulate(c % 2, prev_rows, prev_v, carries, prev_cms)
            prev_cms, prev_rows, prev_v = new_cms, MLA_TK, v_lat(c)
        carries = accumulate(n_chunks % 2, MLA_TK, prev_v, carries, prev_cms)
    else:
        carries = accumulate(0, lc, v_ctx, carries, cms)
    o = jnp.concatenate([acc[0:MLA_V_DIM] / acc[MLA_V_DIM:MLA_V_DIM + 1] for (_, acc) in carries],
                        axis=0)
    o_ref[0] = o.T.astype(o_ref.dtype)


def _mla_call(qt, k, vt, kc, vct, tq):
    b_, _, sq = qt.shape
    h_ = kc.shape[1]
    lc = kc.shape[2]
    hps = 2
    n_chunks = 0 if k is None else k.shape[2] // MLA_TK
    in_specs = [pl.BlockSpec((1, hps * MLA_PAD, tq), lambda b, p, i: (b, p, i))]
    args = [qt]
    if k is not None:
        sk = k.shape[2]
        in_specs += [pl.BlockSpec((1, hps, sk, MLA_PAD), lambda b, p, i: (b, p, 0, 0)),
                     pl.BlockSpec((1, hps, n_chunks, MLA_VT_ROWS, MLA_TK), lambda b, p, i: (b, p, 0, 0, 0))]
        args += [k, vt]
    in_specs += [pl.BlockSpec((1, hps, lc, MLA_PAD), lambda b, p, i: (b, p, 0, 0)),
                 pl.BlockSpec((1, hps, 1, MLA_VT_ROWS, lc), lambda b, p, i: (b, p, 0, 0, 0))]
    args += [kc, vct]
    return pl.pallas_call(
        functools.partial(_mla_kernel, n_chunks=n_chunks, heads_per_step=hps),
        grid=(b_, h_ // hps, sq // tq),
        in_specs=in_specs,
        out_specs=pl.BlockSpec((1, tq, hps * MLA_V_DIM), lambda b, p, i: (b, i, p)),
        out_shape=jax.ShapeDtypeStruct((b_, sq, h_ * MLA_V_DIM), BF16),
        scratch_shapes=[pltpu.VMEM((MLA_TK if n_chunks else lc, tq), F32)] * (2 * hps),
        compiler_params=_cparams(("parallel", "parallel", "arbitrary")),
        name="mla_attn",
    )(*args)


def _swa_kernel(*refs, n_local, s_len):
    sink_ref, qt_ref = refs[:2]
    k_refs = refs[2:2 + n_local]
    vt_refs = refs[2 + n_local:2 + 2 * n_local]
    kc_ref, vct_ref, o_ref = refs[2 + 2 * n_local:5 + 2 * n_local]
    s_refs = refs[5 + 2 * n_local:]
    tq = qt_ref.shape[2]
    lc = kc_ref.shape[2]
    nk = n_local * BLOCK
    if n_local:
        n = pl.program_id(1)
        kpos = n * tq - BLOCK + lax.broadcasted_iota(jnp.int32, (nk, tq), 0)
        qpos = n * tq + lax.broadcasted_iota(jnp.int32, (nk, tq), 1)
        valid = (jnp.abs(kpos - qpos) <= WINDOW) & (kpos >= 0) & (kpos < s_len)
    sinks = [sink_ref[hh] * LOG2E for hh in range(SWA_HEADS)]

    def keys(kh):
        kc = kc_ref[0, kh]
        vct = vct_ref[0, kh]
        if not n_local:
            return kc, vct
        return (jnp.concatenate([kc] + [r[0, kh] for r in k_refs], axis=0),
                jnp.concatenate([vct] + [r[0, kh] for r in vt_refs], axis=1))

    def logits(hh, k_all):
        qt = qt_ref[0, HEAD_DIM * hh:HEAD_DIM * (hh + 1), :]
        s_ref = s_refs[hh % 2]
        s_c = jnp.dot(k_all[0:lc], qt, preferred_element_type=F32)
        s_ref[0:lc, :] = s_c
        m = jnp.maximum(jnp.max(s_c, axis=0, keepdims=True), sinks[hh])
        if n_local:
            s_l = jnp.where(valid, jnp.dot(k_all[lc:], qt, preferred_element_type=F32), NEG_INF)
            s_ref[lc:lc + nk, :] = s_l
            m = jnp.maximum(m, jnp.max(s_l, axis=0, keepdims=True))
        return m

    def accumulate(hh, vt_all, m):
        p = jnp.exp2(s_refs[hh % 2][...] - m).astype(BF16)
        acc = jnp.dot(vt_all, p, preferred_element_type=F32)
        l = acc[HEAD_DIM:HEAD_DIM + 1] + jnp.exp2(sinks[hh] - m)
        return acc[0:HEAD_DIM] / l

    kv = [keys(kh) for kh in range(SWA_KV_HEADS)]
    outs = []
    m_prev = logits(0, kv[0][0])
    for hh in range(1, SWA_HEADS):
        m_next = logits(hh, kv[hh // SWA_GROUP][0])
        outs.append(accumulate(hh - 1, kv[(hh - 1) // SWA_GROUP][1], m_prev))
        m_prev = m_next
    outs.append(accumulate(SWA_HEADS - 1, kv[SWA_KV_HEADS - 1][1], m_prev))
    o = jnp.concatenate(outs, axis=0)
    o_ref[0] = o.T.astype(o_ref.dtype)


def _swa_call(sink, qt, k, vt, kc, vct):
    b_, _, sq = qt.shape
    lc = kc.shape[2]
    local = k is not None
    tq = SWA_TQ if local else sq
    nb = sq // tq
    n_local = tq // BLOCK + 2 if local else 0
    in_specs = [pl.BlockSpec(memory_space=pltpu.SMEM),
                pl.BlockSpec((1, SWA_Q_WIDTH, tq), lambda b, n: (b, 0, n))]
    args = [sink, qt]
    if local:
        nkb = sq // BLOCK
        per = tq // BLOCK

        def blk_idx(n, j):
            return jnp.clip(n * per - 1 + j, 0, nkb - 1)

        in_specs += [pl.BlockSpec((1, SWA_KV_HEADS, BLOCK, HEAD_DIM),
                                  functools.partial(lambda b, n, j: (b, 0, blk_idx(n, j), 0), j=j))
                     for j in range(n_local)]
        in_specs += [pl.BlockSpec((1, SWA_KV_HEADS, SWA_VT_ROWS, BLOCK),
                                  functools.partial(lambda b, n, j: (b, 0, 0, blk_idx(n, j)), j=j))
                     for j in range(n_local)]
        args += [k] * n_local + [vt] * n_local
    in_specs += [pl.BlockSpec((1, SWA_KV_HEADS, lc, HEAD_DIM), lambda b, n: (b, 0, 0, 0)),
                 pl.BlockSpec((1, SWA_KV_HEADS, SWA_VT_ROWS, lc), lambda b, n: (b, 0, 0, 0))]
    args += [kc, vct]
    return pl.pallas_call(
        functools.partial(_swa_kernel, n_local=n_local, s_len=sq),
        grid=(b_, nb),
        in_specs=in_specs,
        out_specs=pl.BlockSpec((1, tq, SWA_Q_WIDTH), lambda b, n: (b, n, 0)),
        out_shape=jax.ShapeDtypeStruct((b_, sq, SWA_Q_WIDTH), BF16),
        scratch_shapes=[pltpu.VMEM((lc + n_local * BLOCK, tq), F32)] * 2,
        compiler_params=_cparams(("parallel", "arbitrary")),
        name="swa_attn",
    )(*args)


def _mlp_kernel(fo_ref, so_ref, mo_ref, h_ref, mod_ref, g2_ref, wo_ref, w1_ref, w2_ref, gf_ref, o_ref, *,
                final, ff_chunk):
    mod = mod_ref[0]
    g1 = mod[:, 2 * D_MODEL:3 * D_MODEL]
    sh2 = mod[:, 3 * D_MODEL:4 * D_MODEL]
    sc2 = mod[:, 4 * D_MODEL:5 * D_MODEL]
    g2 = mod[:, 5 * D_MODEL:6 * D_MODEL]
    mix = jnp.concatenate([fo_ref[0], so_ref[0], mo_ref[0]], axis=-1)
    h = h_ref[0] + g1 * jnp.dot(mix, wo_ref[...], preferred_element_type=F32)
    y = (_rms(h, g2_ref[...]) * (1.0 + sc2) + sh2).astype(BF16)
    acc = None
    for c in range(D_FF // ff_chunk):
        a = jnp.dot(y, w1_ref[:, c * ff_chunk:(c + 1) * ff_chunk], preferred_element_type=F32)
        a = jnp.maximum(a, 0.0)
        a = (a * a).astype(BF16)
        part = jnp.dot(a, w2_ref[c * ff_chunk:(c + 1) * ff_chunk, :], preferred_element_type=F32)
        acc = part if acc is None else acc + part
    h = h + g2 * acc
    if final:
        h = _rms(h, gf_ref[...])
    o_ref[0] = h


def _mlp_call(fo, so, mo, h, mod3, mod_row, g2, wo, w1, w2, gf, tm, final):
    b_, s_, _ = h.shape
    const = lambda b, i: (0, 0)
    tile = lambda b, i: (b, i, 0)
    if mod_row is None:
        mod_map = lambda b, i: (b, 0, 0)
    else:
        mod_map = lambda b, i: (mod_row, 0, 0)
    once = pl.Buffered(1)
    return pl.pallas_call(
        functools.partial(_mlp_kernel, final=final, ff_chunk=1024),
        grid=(b_, s_ // tm),
        in_specs=[
            pl.BlockSpec((1, tm, FOURIER_WIDTH), tile),
            pl.BlockSpec((1, tm, SWA_Q_WIDTH), tile),
            pl.BlockSpec((1, tm, MLA_HEADS * MLA_V_DIM), tile),
            pl.BlockSpec((1, tm, D_MODEL), tile),
            pl.BlockSpec((1, 1, 6 * D_MODEL), mod_map),
            pl.BlockSpec((1, D_MODEL), const),
            pl.BlockSpec((D_MIX, D_MODEL), const, pipeline_mode=once),
            pl.BlockSpec((D_MODEL, D_FF), const, pipeline_mode=once),
            pl.BlockSpec((D_FF, D_MODEL), const, pipeline_mode=once),
            pl.BlockSpec((1, D_MODEL), const),
        ],
        out_specs=pl.BlockSpec((1, tm, D_MODEL), tile),
        out_shape=jax.ShapeDtypeStruct((b_, s_, D_MODEL), F32),
        compiler_params=_cparams(("parallel", "parallel")),
        name="out_proj_mlp",
    )(fo, so, mo, h, mod3, g2, wo, w1, w2, gf)


def _rot_cols(w, half):
    return jnp.concatenate([-w[:, half:], w[:, :half]], axis=1)


def _pack_w_in(w_in):
    z = lambda n: jnp.zeros((D_MODEL, n), F32)
    wq = w_in[:, OFF_SWA_Q:OFF_SWA_K]
    wk = w_in[:, OFF_SWA_K:OFF_SWA_V]
    wkr = w_in[:, OFF_MLA_KR:OFF_MLA_KR + MLA_ROPE_DIM]
    half = HEAD_DIM // 2
    wq_rot = jnp.concatenate(
        [_rot_cols(wq[:, HEAD_DIM * i:HEAD_DIM * (i + 1)], half) for i in range(SWA_HEADS)], axis=1)
    wk_rot = jnp.concatenate(
        [_rot_cols(wk[:, HEAD_DIM * i:HEAD_DIM * (i + 1)], half) for i in range(SWA_KV_HEADS)], axis=1)
    pad_kr = lambda w: jnp.concatenate([z(MLA_NOPE_DIM), w, z(LANE - MLA_NOPE_DIM - MLA_ROPE_DIM)], axis=1)
    cols = [w_in[:, :OFF_SWA_Q], wk, w_in[:, OFF_MLA_CQ:OFF_MLA_KR], pad_kr(wkr), wk_rot,
            pad_kr(_rot_cols(wkr, MLA_ROPE_DIM // 2))]
    rows = [wq, wq_rot, w_in[:, OFF_SWA_V:OFF_MLA_CQ]]
    return jnp.concatenate(cols, axis=1).astype(BF16), jnp.concatenate(rows, axis=1).T.astype(BF16)


def _pack_w_uq(w_uq):
    per = MLA_NOPE_DIM + MLA_ROPE_DIM
    z = jnp.zeros((MLA_Q_RANK, MLA_PAD - per), F32)
    zn = jnp.zeros((MLA_Q_RANK, MLA_NOPE_DIM), F32)
    main, rot = [], []
    for hh in range(MLA_HEADS):
        w = w_uq[:, per * hh:per * (hh + 1)]
        main += [w, z]
        rot += [zn, _rot_cols(w[:, MLA_NOPE_DIM:], MLA_ROPE_DIM // 2), z]
    return jnp.concatenate(main, axis=1).T.astype(BF16), jnp.concatenate(rot, axis=1).T.astype(BF16)


def _pack_w_ukv(w_ukv):
    per = MLA_NOPE_DIM + MLA_V_DIM
    z = jnp.zeros((MLA_KV_RANK, MLA_PAD - MLA_NOPE_DIM), F32)
    ks, vs = [], []
    for hh in range(MLA_HEADS):
        w = w_ukv[:, per * hh:per * (hh + 1)]
        ks += [w[:, :MLA_NOPE_DIM], z]
        vs.append(w[:, MLA_NOPE_DIM:])
    return jnp.concatenate(ks, axis=1).astype(BF16), jnp.concatenate(vs, axis=1).T.astype(BF16)


def _rope_table(rows):
    r, col = jnp.meshgrid(jnp.arange(rows, dtype=F32), jnp.arange(GRID_W, dtype=F32), indexing="ij")
    r = r.reshape(-1)
    col = col.reshape(-1)

    def tables(dim):
        n_freq = dim // 4
        inv = ROPE_THETA ** (-jnp.arange(n_freq, dtype=F32) / n_freq)
        ang = jnp.concatenate([r[:, None] * inv[None, :], col[:, None] * inv[None, :]], axis=-1)
        return jnp.cos(ang), jnp.sin(ang)

    ch, sh = tables(HEAD_DIM)
    cr, sr = tables(MLA_ROPE_DIM)
    s_ = r.shape[0]
    ones = jnp.ones((s_, MLA_NOPE_DIM), F32)
    zn = jnp.zeros((s_, MLA_NOPE_DIM), F32)
    zp = jnp.zeros((s_, LANE - MLA_NOPE_DIM - MLA_ROPE_DIM), F32)
    return jnp.concatenate([ch, ch, ch, ch, sh, sh, sh, sh,
                            ones, cr, cr, zp, zn, sr, sr, zp], axis=-1)


def _identity_rope_table(n):
    ones = jnp.ones((n, LANE), F32)
    zeros = jnp.zeros((n, LANE), F32)
    m = jnp.concatenate([jnp.ones((n, MLA_NOPE_DIM + MLA_ROPE_DIM), F32),
                         jnp.zeros((n, LANE - MLA_NOPE_DIM - MLA_ROPE_DIM), F32)], axis=-1)
    return jnp.concatenate([ones, zeros, m, zeros], axis=-1)


def _dft_tables(n):
    k = jnp.arange(n, dtype=jnp.int32)
    ks = (k[:, None] * k[None, :]) % n
    ang = ks.astype(F32) * (2.0 * math.pi / n)
    scale = 1.0 / math.sqrt(n)
    return (jnp.cos(ang) * scale).astype(BF16), (-jnp.sin(ang) * scale).astype(BF16)


def _dft_tables_big(n, n1):
    n2 = n // n1
    k = jnp.arange(n, dtype=jnp.int32)
    a1 = ((k[:, None] * jnp.arange(n1, dtype=jnp.int32)[None, :]) % n1).astype(F32) * (2.0 * math.pi / n1)
    a2 = ((k[:, None] * jnp.arange(n2, dtype=jnp.int32)[None, :]) % n).astype(F32) * (2.0 * math.pi / n)
    c1, s1 = jnp.cos(a1)[:, :, None], jnp.sin(a1)[:, :, None]
    c2, s2 = jnp.cos(a2)[:, None, :], jnp.sin(a2)[:, None, :]
    scale = 1.0 / math.sqrt(n)
    ct = ((c1 * c2 - s1 * s2) * scale).reshape(n, n).astype(BF16)
    st = (-(s1 * c2 + c1 * s2) * scale).reshape(n, n).astype(BF16)
    return ct, st


def kernel(x, c, ctx, c_ctx, w_ada, b_ada, norm1_g, norm2_g, w_in, w_fourier, swa_sink, mla_q_norm, w_uq,
           mla_kv_norm, w_ukv, w_out, w_mlp1, w_mlp2, final_norm_g):
    b_, s_, _ = x.shape
    lc = ctx.shape[1]
    depth = w_ada.shape[0]
    rows = s_ // GRID_W

    mod_rows = 16
    cc = jnp.concatenate([c, c_ctx[None, :], jnp.zeros((mod_rows - b_ - 1, D_MODEL), F32)], axis=0)
    mod_all = _ada_call(cc, w_ada, b_ada)

    rope = _rope_table(rows)
    rope_c = _identity_rope_table(lc)
    to_feature_major = lambda t: jnp.concatenate([t[:, 2 * LANE:], t[:, :2 * LANE]], axis=1).T
    ropet = to_feature_major(rope)
    ropet_c = to_feature_major(rope_c)
    ctab, stab = _dft_tables_big(s_, GRID_W)
    ctab_c, stab_c = _dft_tables(lc)
    kk = jnp.arange(FOURIER_GROUP_DIM, dtype=jnp.int32)
    ang64 = ((kk[:, None] * kk[None, :]) % FOURIER_GROUP_DIM).astype(F32) * (2.0 * math.pi / FOURIER_GROUP_DIM)
    c64, s64 = jnp.cos(ang64), jnp.sin(ang64)

    h, hc = x, ctx
    for l in range(depth):
        last = l == depth - 1
        mod3 = mod_all[l].reshape(mod_rows, 1, 6 * D_MODEL)
        g1 = norm1_g[l].reshape(1, D_MODEL)
        g2 = norm2_g[l].reshape(1, D_MODEL)
        gf = final_norm_g.reshape(1, D_MODEL)
        w1, w1t = _pack_w_in(w_in[l])
        wuq, wuqr = _pack_w_uq(w_uq[l])
        wukvk, wukvv = _pack_w_ukv(w_ukv[l])
        wcs = _fw_call(c64, s64, w_fourier[l]).astype(BF16)
        qn = mla_q_norm[l].reshape(1, MLA_Q_RANK)
        kvn = mla_kv_norm[l].reshape(1, MLA_KV_RANK)
        wo = w_out[l].astype(BF16)
        wm1 = w_mlp1[l].astype(BF16)
        wm2 = w_mlp2[l].astype(BF16)
        sink = swa_sink[l]

        fu, fv, sq, sk, sv, mq, mk, mv = _proj_call(h, mod3, None, g1, w1, w1t, rope, ropet, wcs, qn, wuq, wuqr,
                                                    kvn, wukvk, wukvv, tm=MLA_TK)
        fuc, fvc, sqc, skc, svc, mqc, mkc, mvc = _proj_call(hc, mod3, b_, g1, w1, w1t, rope_c, ropet_c, wcs, qn,
                                                            wuq, wuqr, kvn, wukvk, wukvv, tm=lc)

        fo = _dft_call(ctab, stab, fu, fv, tk=512)
        so = _swa_call(sink, sq, sk, sv, skc, svc)
        mo = _mla_call(mq, mk, mv, mkc, mvc, tq=512)
        h = _mlp_call(fo, so, mo, h, mod3, None, g2, wo, wm1, wm2, gf, tm=512, final=last)

        if not last:
            foc = _dft_call(ctab_c, stab_c, fuc, fvc, tk=lc)
            soc = _swa_call(sink, sqc, None, None, skc, svc)
            moc = _mla_call(mqc, None, None, mkc, mvc, tq=lc)
            hc = _mlp_call(foc, soc, moc, hc, mod3, b_, g2, wo, wm1, wm2, gf, tm=lc, final=False)
    return h
```

```python
import functools
import math

import jax
import jax.numpy as jnp
from jax import lax
from jax.experimental import pallas as pl
from jax.experimental.pallas import tpu as pltpu

D_MODEL = 1024
GRID_W = 64
HEAD_DIM = 64
FOURIER_GROUPS = 4
FOURIER_GROUP_DIM = 64
FOURIER_WIDTH = FOURIER_GROUPS * FOURIER_GROUP_DIM
SWA_HEADS = 6
SWA_KV_HEADS = 2
SWA_GROUP = SWA_HEADS // SWA_KV_HEADS
SWA_Q_WIDTH = SWA_HEADS * HEAD_DIM
SWA_KV_WIDTH = SWA_KV_HEADS * HEAD_DIM
WINDOW = 128
BLOCK = 128
MLA_HEADS = 6
MLA_NOPE_DIM = 64
MLA_ROPE_DIM = 32
MLA_V_DIM = 64
MLA_Q_RANK = 256
MLA_KV_RANK = 128
MLA_SCALE = (MLA_NOPE_DIM + MLA_ROPE_DIM) ** -0.5
SWA_SCALE = HEAD_DIM ** -0.5
D_MIX = FOURIER_WIDTH + SWA_Q_WIDTH + MLA_HEADS * MLA_V_DIM
OFF_SWA_Q = FOURIER_WIDTH
OFF_SWA_K = OFF_SWA_Q + SWA_Q_WIDTH
OFF_SWA_V = OFF_SWA_K + SWA_KV_WIDTH
OFF_MLA_CQ = OFF_SWA_V + SWA_KV_WIDTH
OFF_MLA_CKV = OFF_MLA_CQ + MLA_Q_RANK
OFF_MLA_KR = OFF_MLA_CKV + MLA_KV_RANK
D_FF = 4 * D_MODEL
ROPE_THETA = 10000.0
NORM_EPS = 1e-6
NEG_INF = -1e30

LANE = 128
MLA_PAD = LANE

C_F = 0
C_K = C_F + FOURIER_WIDTH
C_CQ = C_K + SWA_KV_WIDTH
C_CKV = C_CQ + MLA_Q_RANK
C_KR = C_CKV + MLA_KV_RANK
C_KROT = C_KR + LANE
C_KRROT = C_KROT + SWA_KV_WIDTH
W1_COLS = C_KRROT + LANE
R_Q = 0
R_QROT = R_Q + SWA_Q_WIDTH
R_V = R_QROT + SWA_Q_WIDTH
W1T_ROWS = R_V + SWA_KV_WIDTH

BF16 = jnp.bfloat16
F32 = jnp.float32
LOG2E = math.log2(math.e)
_NT = (((1,), (1,)), ((), ()))
MLA_TK = 512
MLA_VT_ROWS = MLA_V_DIM + 16
MLA_S_SLOTS = 2
SWA_VT_ROWS = HEAD_DIM + 16
SWA_TQ = 256

VMEM_LIMIT = 56 * 1024 * 1024


def _cparams(sem):
    return pltpu.CompilerParams(dimension_semantics=sem, vmem_limit_bytes=VMEM_LIMIT)


def _rms(x, g):
    ms = jnp.mean(x * x, axis=-1, keepdims=True)
    return x * lax.rsqrt(ms + NORM_EPS) * g


def _ada_kernel(cc_ref, w_ref, b_ref, o_ref):
    cc = cc_ref[...]
    s = cc * jax.nn.sigmoid(cc)
    o_ref[0] = jnp.dot(s, w_ref[0], preferred_element_type=F32,
                       precision=lax.Precision.HIGHEST) + b_ref[0]


def _ada_call(cc, w_ada, b_ada):
    depth = w_ada.shape[0]
    rows = cc.shape[0]
    tn = 1536
    n_out = w_ada.shape[2]
    return pl.pallas_call(
        _ada_kernel,
        grid=(depth, n_out // tn),
        in_specs=[
            pl.BlockSpec((rows, D_MODEL), lambda l, j: (0, 0)),
            pl.BlockSpec((1, D_MODEL, tn), lambda l, j: (l, 0, j)),
            pl.BlockSpec((1, 1, tn), lambda l, j: (l, 0, j)),
        ],
        out_specs=pl.BlockSpec((1, rows, tn), lambda l, j: (l, 0, j)),
        out_shape=jax.ShapeDtypeStruct((depth, rows, n_out), F32),
        compiler_params=_cparams(("arbitrary", "arbitrary")),
        name="ada_mod",
    )(cc, w_ada, b_ada.reshape(depth, 1, n_out))


def _fw_kernel(c_ref, s_ref, w_ref, o_ref):
    o_ref[...] = jnp.zeros(o_ref.shape, o_ref.dtype)
    inv = 1.0 / math.sqrt(FOURIER_GROUP_DIM)
    for g in range(FOURIER_GROUPS):
        w = w_ref[g]
        a = jnp.dot(c_ref[...], w, preferred_element_type=F32, precision=lax.Precision.HIGHEST) * inv
        b = jnp.dot(s_ref[...], w, preferred_element_type=F32, precision=lax.Precision.HIGHEST) * inv
        lo = g * FOURIER_GROUP_DIM
        hi = lo + FOURIER_GROUP_DIM
        o_ref[lo:hi, lo:hi] = a
        o_ref[lo:hi, FOURIER_WIDTH + lo:FOURIER_WIDTH + hi] = b


def _fw_call(c64, s64, w_f):
    return pl.pallas_call(
        _fw_kernel,
        out_shape=jax.ShapeDtypeStruct((FOURIER_WIDTH, 2 * FOURIER_WIDTH), F32),
        name="fourier_w",
    )(c64, s64, w_f)


def _ones_row(rows, cols):
    return (lax.broadcasted_iota(jnp.int32, (rows, cols), 0) == 0).astype(BF16)


def _proj_kernel(h_ref, mod_ref, g1_ref, w1_ref, w1t_ref, rope_ref, ropet_ref, wcs_ref, qn_ref, wuq_ref,
                 wuqr_ref, kvn_ref, wukvk_ref, wukvv_ref,
                 u_o, v_o, sq_o, sk_o, sv_o, mq_o, mk_o, mv_o):
    x = h_ref[0]
    tm = x.shape[0]
    mod = mod_ref[0]
    sh = mod[:, 0:D_MODEL]
    sc = mod[:, D_MODEL:2 * D_MODEL]
    y = (_rms(x, g1_ref[...]) * (1.0 + sc) + sh).astype(BF16)
    u = jnp.dot(y, w1_ref[...], preferred_element_type=F32)
    ut = lax.dot_general(w1t_ref[...], y, _NT, preferred_element_type=F32)

    cos_h = rope_ref[:, 0:LANE]
    sin_h = rope_ref[:, LANE:2 * LANE]
    cos_m = rope_ref[:, 2 * LANE:3 * LANE]
    sin_m = rope_ref[:, 3 * LANE:4 * LANE]
    cos_mt = ropet_ref[0:LANE, :]
    sin_mt = ropet_ref[LANE:2 * LANE, :]
    cos_ht = ropet_ref[2 * LANE:3 * LANE, :]
    sin_ht = ropet_ref[3 * LANE:4 * LANE, :]

    f = u[:, C_F:C_F + FOURIER_WIDTH].astype(BF16)
    uv = jnp.dot(f, wcs_ref[...], preferred_element_type=F32).astype(BF16)
    u_o[0] = uv[:, :FOURIER_WIDTH]
    v_o[0] = uv[:, FOURIER_WIDTH:]

    for j in range(SWA_Q_WIDTH // LANE):
        sl = slice(LANE * j, LANE * (j + 1))
        qj = ut[R_Q:R_QROT][sl] * cos_ht + ut[R_QROT:R_V][sl] * sin_ht
        sq_o[0, sl, :] = (qj * (SWA_SCALE * LOG2E)).astype(BF16)
    kk = (u[:, C_K:C_K + LANE] * cos_h + u[:, C_KROT:C_KROT + LANE] * sin_h).astype(BF16)
    ones_swa = _ones_row(SWA_VT_ROWS - HEAD_DIM, tm)
    for kh in range(SWA_KV_HEADS):
        sk_o[0, kh] = kk[:, HEAD_DIM * kh:HEAD_DIM * (kh + 1)]
        sv_o[0, kh, 0:HEAD_DIM, :] = ut[R_V + HEAD_DIM * kh:R_V + HEAD_DIM * (kh + 1)].astype(BF16)
        sv_o[0, kh, HEAD_DIM:SWA_VT_ROWS, :] = ones_swa

    cqn = _rms(u[:, C_CQ:C_CQ + MLA_Q_RANK], qn_ref[...]).astype(BF16)
    qa = lax.dot_general(wuq_ref[...], cqn, _NT, preferred_element_type=F32)
    qr = lax.dot_general(wuqr_ref[...], cqn, _NT, preferred_element_type=F32)
    for hh in range(MLA_HEADS):
        sl = slice(MLA_PAD * hh, MLA_PAD * (hh + 1))
        mq_o[0, sl, :] = ((qa[sl] * cos_mt + qr[sl] * sin_mt) * (MLA_SCALE * LOG2E)).astype(BF16)

    ckvn = _rms(u[:, C_CKV:C_CKV + MLA_KV_RANK], kvn_ref[...]).astype(BF16)
    kr = u[:, C_KR:C_KR + LANE] * cos_m + u[:, C_KRROT:C_KRROT + LANE] * sin_m
    kvk = jnp.dot(ckvn, wukvk_ref[...], preferred_element_type=F32)
    vt = lax.dot_general(wukvv_ref[...], ckvn, _NT, preferred_element_type=F32).astype(BF16)
    ones_mla = _ones_row(MLA_VT_ROWS - MLA_V_DIM, tm)
    for hh in range(MLA_HEADS):
        mk_o[0, hh] = (kvk[:, MLA_PAD * hh:MLA_PAD * (hh + 1)] + kr).astype(BF16)
        mv_o[0, hh, 0, 0:MLA_V_DIM, :] = vt[MLA_V_DIM * hh:MLA_V_DIM * (hh + 1)]
        mv_o[0, hh, 0, MLA_V_DIM:MLA_VT_ROWS, :] = ones_mla


def _proj_call(h, mod3, mod_row, g1, w1, w1t, rope, ropet, wcs, qn, wuq, wuqr, kvn, wukvk, wukvv, tm):
    b_, s_, _ = h.shape
    const = lambda b, i: (0, 0)
    if mod_row is None:
        mod_map = lambda b, i: (b, 0, 0)
    else:
        mod_map = lambda b, i: (mod_row, 0, 0)
    out_shapes = (
        jax.ShapeDtypeStruct((b_, s_, FOURIER_WIDTH), BF16),
        jax.ShapeDtypeStruct((b_, s_, FOURIER_WIDTH), BF16),
        jax.ShapeDtypeStruct((b_, SWA_Q_WIDTH, s_), BF16),
        jax.ShapeDtypeStruct((b_, SWA_KV_HEADS, s_, HEAD_DIM), BF16),
        jax.ShapeDtypeStruct((b_, SWA_KV_HEADS, SWA_VT_ROWS, s_), BF16),
        jax.ShapeDtypeStruct((b_, MLA_HEADS * MLA_PAD, s_), BF16),
        jax.ShapeDtypeStruct((b_, MLA_HEADS, s_, MLA_PAD), BF16),
        jax.ShapeDtypeStruct((b_, MLA_HEADS, s_ // tm, MLA_VT_ROWS, tm), BF16),
    )
    out_specs = (
        pl.BlockSpec((1, tm, FOURIER_WIDTH), lambda b, i: (b, i, 0)),
        pl.BlockSpec((1, tm, FOURIER_WIDTH), lambda b, i: (b, i, 0)),
        pl.BlockSpec((1, SWA_Q_WIDTH, tm), lambda b, i: (b, 0, i)),
        pl.BlockSpec((1, SWA_KV_HEADS, tm, HEAD_DIM), lambda b, i: (b, 0, i, 0)),
        pl.BlockSpec((1, SWA_KV_HEADS, SWA_VT_ROWS, tm), lambda b, i: (b, 0, 0, i)),
        pl.BlockSpec((1, MLA_HEADS * MLA_PAD, tm), lambda b, i: (b, 0, i)),
        pl.BlockSpec((1, MLA_HEADS, tm, MLA_PAD), lambda b, i: (b, 0, i, 0)),
        pl.BlockSpec((1, MLA_HEADS, 1, MLA_VT_ROWS, tm), lambda b, i: (b, 0, i, 0, 0)),
    )
    return pl.pallas_call(
        _proj_kernel,
        grid=(b_, s_ // tm),
        in_specs=[
            pl.BlockSpec((1, tm, D_MODEL), lambda b, i: (b, i, 0)),
            pl.BlockSpec((1, 1, 6 * D_MODEL), mod_map),
            pl.BlockSpec((1, D_MODEL), const),
            pl.BlockSpec((D_MODEL, W1_COLS), const),
            pl.BlockSpec((W1T_ROWS, D_MODEL), const),
            pl.BlockSpec((tm, 4 * LANE), lambda b, i: (i, 0)),
            pl.BlockSpec((4 * LANE, tm), lambda b, i: (0, i)),
            pl.BlockSpec((FOURIER_WIDTH, 2 * FOURIER_WIDTH), const),
            pl.BlockSpec((1, MLA_Q_RANK), const),
            pl.BlockSpec((MLA_HEADS * MLA_PAD, MLA_Q_RANK), const),
            pl.BlockSpec((MLA_HEADS * MLA_PAD, MLA_Q_RANK), const),
            pl.BlockSpec((1, MLA_KV_RANK), const),
            pl.BlockSpec((MLA_KV_RANK, MLA_HEADS * MLA_PAD), const),
            pl.BlockSpec((MLA_HEADS * MLA_V_DIM, MLA_KV_RANK), const),
        ],
        out_specs=out_specs,
        out_shape=out_shapes,
        compiler_params=_cparams(("parallel", "parallel")),
        name="norm_in_proj",
    )(h, mod3, g1, w1, w1t, rope, ropet, wcs, qn, wuq, wuqr, kvn, wukvk, wukvv)


def _dft_kernel(c_ref, s_ref, u_ref, v_ref, o_ref):
    o = jnp.dot(c_ref[...], u_ref[0], preferred_element_type=F32)
    o = o + jnp.dot(s_ref[...], v_ref[0], preferred_element_type=F32)
    o_ref[0] = o.astype(o_ref.dtype)


def _dft_call(ctab, stab, u, v, tk):
    b_, s_, _ = u.shape
    whole = pl.BlockSpec((1, s_, FOURIER_WIDTH), lambda i, b: (b, 0, 0))
    return pl.pallas_call(
        _dft_kernel,
        grid=(s_ // tk, b_),
        in_specs=[
            pl.BlockSpec((tk, s_), lambda i, b: (i, 0)),
            pl.BlockSpec((tk, s_), lambda i, b: (i, 0)),
            whole, whole,
        ],
        out_specs=pl.BlockSpec((1, tk, FOURIER_WIDTH), lambda i, b: (b, i, 0)),
        out_shape=jax.ShapeDtypeStruct((b_, s_, FOURIER_WIDTH), BF16),
        compiler_params=_cparams(("parallel", "arbitrary")),
        name="fourier_dft",
    )(ctab, stab, u, v)


def _fft1_kernel(m1_ref, m2_ref, tc_ref, ts_ref, u_ref, v_ref, bre_ref, bim_ref):
    n1 = tc_ref.shape[0]
    a = jnp.dot(m1_ref[...], u_ref[0], preferred_element_type=F32)
    a = a + jnp.dot(m2_ref[...], v_ref[0], preferred_element_type=F32)
    are, aim = a[0:n1], a[n1:2 * n1]
    tc, ts = tc_ref[...], ts_ref[...]
    bre_ref[0] = (are * tc + aim * ts).astype(bre_ref.dtype)
    bim_ref[0] = (aim * tc - are * ts).astype(bim_ref.dtype)


def _fft2_kernel(g1_ref, g2_ref, bre_ref, bim_ref, o_ref):
    n2 = g1_ref.shape[1]
    for j in range(bre_ref.shape[1] // n2):
        rows = slice(n2 * j, n2 * (j + 1))
        x = jnp.dot(g1_ref[...], bre_ref[0, rows, :], preferred_element_type=F32)
        x = x + jnp.dot(g2_ref[...], bim_ref[0, rows, :], preferred_element_type=F32)
        o_ref[0, :, FOURIER_WIDTH * j:FOURIER_WIDTH * (j + 1)] = x.astype(o_ref.dtype)


def _fft_call(tabs, u, v):
    m1, m2, tc, ts, g1, g2 = tabs
    b_, s_, w = u.shape
    n1 = tc.shape[0]
    n2 = s_ // n1
    tn = 16 * w
    u2 = u.reshape(b_, n1, n2 * w)
    v2 = v.reshape(b_, n1, n2 * w)
    tile = pl.BlockSpec((1, n1, tn), lambda t, b: (b, 0, t))
    tab = pl.BlockSpec((n1, tn), lambda t, b: (0, t))
    mat = pl.BlockSpec((2 * n1, n1), lambda t, b: (0, 0))
    bre, bim = pl.pallas_call(
        _fft1_kernel,
        grid=(n2 * w // tn, b_),
        in_specs=[mat, mat, tab, tab, tile, tile],
        out_specs=(tile, tile),
        out_shape=(jax.ShapeDtypeStruct((b_, n1, n2 * w), BF16),) * 2,
        compiler_params=_cparams(("parallel", "arbitrary")),
        name="fourier_fft1",
    )(m1, m2, tc, ts, u2, v2)
    k1_per_step = 16
    rows = k1_per_step * n2
    blk = pl.BlockSpec((1, rows, w), lambda b, t: (b, t, 0))
    mat2 = pl.BlockSpec((n2, n2), lambda b, t: (0, 0))
    out = pl.pallas_call(
        _fft2_kernel,
        grid=(b_, n1 // k1_per_step),
        in_specs=[mat2, mat2, blk, blk],
        out_specs=pl.BlockSpec((1, n2, k1_per_step * w), lambda b, t: (b, 0, t)),
        out_shape=jax.ShapeDtypeStruct((b_, n2, n1 * w), BF16),
        compiler_params=_cparams(("parallel", "parallel")),
        name="fourier_fft2",
    )(g1, g2, bre.reshape(b_, s_, w), bim.reshape(b_, s_, w))
    return out.reshape(b_, s_, w)


def _fft_tables(s_, n1, w):
    n2 = s_ // n1
    k1 = jnp.arange(n1, dtype=jnp.int32)
    a1 = ((k1[:, None] * k1[None, :]) % n1).astype(F32) * (2.0 * math.pi / n1)
    c1, s1 = jnp.cos(a1), jnp.sin(a1)
    m1 = jnp.concatenate([c1, -s1], axis=0).astype(BF16)
    m2 = jnp.concatenate([-s1, -c1], axis=0).astype(BF16)
    s2 = jnp.arange(n2, dtype=jnp.int32)
    th = ((k1[:, None] * s2[None, :]) % s_).astype(F32) * (2.0 * math.pi / s_)
    tc = jnp.repeat(jnp.cos(th), w, axis=1)
    ts = jnp.repeat(jnp.sin(th), w, axis=1)
    a2 = ((s2[:, None] * s2[None, :]) % n2).astype(F32) * (2.0 * math.pi / n2)
    scale = 1.0 / math.sqrt(s_)
    g1 = (jnp.cos(a2) * scale).astype(BF16)
    g2 = (jnp.sin(a2) * scale).astype(BF16)
    return m1, m2, tc, ts, g1, g2


def _mla_kernel(*refs, n_chunks, heads_per_step):
    hps = heads_per_step
    n_in = 5 if n_chunks else 3
    if n_chunks:
        qt_ref, k_ref, vt_ref, kc_ref, vct_ref = refs[:n_in]
    else:
        qt_ref, kc_ref, vct_ref = refs[:n_in]
    o_ref = refs[n_in]
    scratch = refs[n_in + 1:]
    s_refs = [scratch[MLA_S_SLOTS * hh:MLA_S_SLOTS * (hh + 1)] for hh in range(hps)]
    tq = qt_ref.shape[2]
    lc = kc_ref.shape[2]
    qts = [qt_ref[0, MLA_PAD * hh:MLA_PAD * (hh + 1), :] for hh in range(hps)]

    def logits(ks, idx, rows):
        cms = []
        for hh in range(hps):
            s = jnp.dot(ks(hh), qts[hh], preferred_element_type=F32)
            s_refs[hh][idx % MLA_S_SLOTS][0:rows, :] = s
            cms.append(jnp.max(s, axis=0, keepdims=True))
        return tuple(cms)

    def accumulate(idx, rows, vts, carries, cms):
        out = []
        for hh in range(hps):
            m, acc = carries[hh]
            m_new = jnp.maximum(m, cms[hh])
            alpha = jnp.exp2(m - m_new)
            p = jnp.exp2(s_refs[hh][idx % MLA_S_SLOTS][0:rows, :] - m_new).astype(BF16)
            acc = alpha * acc + jnp.dot(vts(hh), p, preferred_element_type=F32)
            out.append((m_new, acc))
        return tuple(out)

    def k_lat(c):
        return lambda hh: k_ref[0, hh, c * MLA_TK:(c + 1) * MLA_TK, :]

    def v_lat(c):
        return lambda hh: vt_ref[0, hh, c]

    k_ctx = lambda hh: kc_ref[0, hh]
    v_ctx = lambda hh: vct_ref[0, hh, 0]

    init = (jnp.full((1, tq), NEG_INF, F32), jnp.zeros((MLA_VT_ROWS, tq), F32))
    carries = tuple(init for _ in range(hps))
    cms = logits(k_ctx, 0, lc)
    if n_chunks:
        prev_cms, prev_rows, prev_v = cms, lc, v_ctx
        for c in range(n_chunks):
            new_cms = logits(k_lat(c), c + 1, MLA_TK)
            carries = accumulate(c, prev_rows, prev_v, carries, prev_cms)
            prev_cms, prev_rows, prev_v = new_cms, MLA_TK, v_lat(c)
        carries = accumulate(n_chunks, MLA_TK, prev_v, carries, prev_cms)
    else:
        carries = accumulate(0, lc, v_ctx, carries, cms)
    o = jnp.concatenate([acc[0:MLA_V_DIM] / acc[MLA_V_DIM:MLA_V_DIM + 1] for (_, acc) in carries],
                        axis=0)
    o_ref[0] = o.T.astype(o_ref.dtype)


def _mla_call(qt, k, vt, kc, vct, tq):
    b_, _, sq = qt.shape
    h_ = kc.shape[1]
    lc = kc.shape[2]
    hps = 2
    n_chunks = 0 if k is None else k.shape[2] // MLA_TK
    in_specs = [pl.BlockSpec((1, hps * MLA_PAD, tq), lambda b, p, i: (b, p, i))]
    args = [qt]
    if k is not None:
        sk = k.shape[2]
        in_specs += [pl.BlockSpec((1, hps, sk, MLA_PAD), lambda b, p, i: (b, p, 0, 0)),
                     pl.BlockSpec((1, hps, n_chunks, MLA_VT_ROWS, MLA_TK), lambda b, p, i: (b, p, 0, 0, 0))]
        args += [k, vt]
    in_specs += [pl.BlockSpec((1, hps, lc, MLA_PAD), lambda b, p, i: (b, p, 0, 0)),
                 pl.BlockSpec((1, hps, 1, MLA_VT_ROWS, lc), lambda b, p, i: (b, p, 0, 0, 0))]
    args += [kc, vct]
    return pl.pallas_call(
        functools.partial(_mla_kernel, n_chunks=n_chunks, heads_per_step=hps),
        grid=(b_, h_ // hps, sq // tq),
        in_specs=in_specs,
        out_specs=pl.BlockSpec((1, tq, hps * MLA_V_DIM), lambda b, p, i: (b, i, p)),
        out_shape=jax.ShapeDtypeStruct((b_, sq, h_ * MLA_V_DIM), BF16),
        scratch_shapes=[pltpu.VMEM((MLA_TK if n_chunks else lc, tq), F32)] * (MLA_S_SLOTS * hps),
        compiler_params=_cparams(("parallel", "parallel", "arbitrary")),
        name="mla_attn",
    )(*args)


def _swa_kernel(*refs, n_local, s_len):
    sink_ref, qt_ref = refs[:2]
    k_refs = refs[2:2 + n_local]
    vt_refs = refs[2 + n_local:2 + 2 * n_local]
    kc_ref, vct_ref, o_ref = refs[2 + 2 * n_local:5 + 2 * n_local]
    s_refs = refs[5 + 2 * n_local:]
    tq = qt_ref.shape[2]
    lc = kc_ref.shape[2]
    nk = n_local * BLOCK
    if n_local:
        n = pl.program_id(1)
        kpos = n * tq - BLOCK + lax.broadcasted_iota(jnp.int32, (nk, tq), 0)
        qpos = n * tq + lax.broadcasted_iota(jnp.int32, (nk, tq), 1)
        valid = (jnp.abs(kpos - qpos) <= WINDOW) & (kpos >= 0) & (kpos < s_len)
    sinks = [sink_ref[hh] * LOG2E for hh in range(SWA_HEADS)]

    def keys(kh):
        kc = kc_ref[0, kh]
        vct = vct_ref[0, kh]
        if not n_local:
            return kc, vct
        return (jnp.concatenate([kc] + [r[0, kh] for r in k_refs], axis=0),
                jnp.concatenate([vct] + [r[0, kh] for r in vt_refs], axis=1))

    def logits(hh, k_all):
        qt = qt_ref[0, HEAD_DIM * hh:HEAD_DIM * (hh + 1), :]
        s_ref = s_refs[hh % 2]
        s_c = jnp.dot(k_all[0:lc], qt, preferred_element_type=F32)
        s_ref[0:lc, :] = s_c
        m = jnp.maximum(jnp.max(s_c, axis=0, keepdims=True), sinks[hh])
        if n_local:
            s_l = jnp.where(valid, jnp.dot(k_all[lc:], qt, preferred_element_type=F32), NEG_INF)
            s_ref[lc:lc + nk, :] = s_l
            m = jnp.maximum(m, jnp.max(s_l, axis=0, keepdims=True))
        return m

    def accumulate(hh, vt_all, m):
        p = jnp.exp2(s_refs[hh % 2][...] - m).astype(BF16)
        acc = jnp.dot(vt_all, p, preferred_element_type=F32)
        l = acc[HEAD_DIM:HEAD_DIM + 1] + jnp.exp2(sinks[hh] - m)
        return acc[0:HEAD_DIM] / l

    kv = [keys(kh) for kh in range(SWA_KV_HEADS)]
    outs = []
    m_prev = logits(0, kv[0][0])
    for hh in range(1, SWA_HEADS):
        m_next = logits(hh, kv[hh // SWA_GROUP][0])
        outs.append(accumulate(hh - 1, kv[(hh - 1) // SWA_GROUP][1], m_prev))
        m_prev = m_next
    outs.append(accumulate(SWA_HEADS - 1, kv[SWA_KV_HEADS - 1][1], m_prev))
    o = jnp.concatenate(outs, axis=0)
    o_ref[0] = o.T.astype(o_ref.dtype)


def _swa_call(sink, qt, k, vt, kc, vct):
    b_, _, sq = qt.shape
    lc = kc.shape[2]
    local = k is not None
    tq = SWA_TQ if local else sq
    nb = sq // tq
    n_local = tq // BLOCK + 2 if local else 0
    in_specs = [pl.BlockSpec(memory_space=pltpu.SMEM),
                pl.BlockSpec((1, SWA_Q_WIDTH, tq), lambda b, n: (b, 0, n))]
    args = [sink, qt]
    if local:
        nkb = sq // BLOCK
        per = tq // BLOCK

        def blk_idx(n, j):
            return jnp.clip(n * per - 1 + j, 0, nkb - 1)

        in_specs += [pl.BlockSpec((1, SWA_KV_HEADS, BLOCK, HEAD_DIM),
                                  functools.partial(lambda b, n, j: (b, 0, blk_idx(n, j), 0), j=j))
                     for j in range(n_local)]
        in_specs += [pl.BlockSpec((1, SWA_KV_HEADS, SWA_VT_ROWS, BLOCK),
                                  functools.partial(lambda b, n, j: (b, 0, 0, blk_idx(n, j)), j=j))
                     for j in range(n_local)]
        args += [k] * n_local + [vt] * n_local
    in_specs += [pl.BlockSpec((1, SWA_KV_HEADS, lc, HEAD_DIM), lambda b, n: (b, 0, 0, 0)),
                 pl.BlockSpec((1, SWA_KV_HEADS, SWA_VT_ROWS, lc), lambda b, n: (b, 0, 0, 0))]
    args += [kc, vct]
    return pl.pallas_call(
        functools.partial(_swa_kernel, n_local=n_local, s_len=sq),
        grid=(b_, nb),
        in_specs=in_specs,
        out_specs=pl.BlockSpec((1, tq, SWA_Q_WIDTH), lambda b, n: (b, n, 0)),
        out_shape=jax.ShapeDtypeStruct((b_, sq, SWA_Q_WIDTH), BF16),
        scratch_shapes=[pltpu.VMEM((lc + n_local * BLOCK, tq), F32)] * 2,
        compiler_params=_cparams(("parallel", "arbitrary")),
        name="swa_attn",
    )(*args)


def _mlp_kernel(fo_ref, so_ref, mo_ref, h_ref, mod_ref, g2_ref, wo_ref, w1_ref, w2_ref, gf_ref, o_ref, *,
                final, ff_chunk):
    mod = mod_ref[0]
    g1 = mod[:, 2 * D_MODEL:3 * D_MODEL]
    sh2 = mod[:, 3 * D_MODEL:4 * D_MODEL]
    sc2 = mod[:, 4 * D_MODEL:5 * D_MODEL]
    g2 = mod[:, 5 * D_MODEL:6 * D_MODEL]
    mix = jnp.concatenate([fo_ref[0], so_ref[0], mo_ref[0]], axis=-1)
    h = h_ref[0] + g1 * jnp.dot(mix, wo_ref[...], preferred_element_type=F32)
    y = (_rms(h, g2_ref[...]) * (1.0 + sc2) + sh2).astype(BF16)
    acc = None
    for c in range(D_FF // ff_chunk):
        a = jnp.dot(y, w1_ref[:, c * ff_chunk:(c + 1) * ff_chunk], preferred_element_type=F32)
        a = jnp.maximum(a, 0.0)
        a = (a * a).astype(BF16)
        part = jnp.dot(a, w2_ref[c * ff_chunk:(c + 1) * ff_chunk, :], preferred_element_type=F32)
        acc = part if acc is None else acc + part
    h = h + g2 * acc
    if final:
        h = _rms(h, gf_ref[...])
    o_ref[0] = h


def _mlp_call(fo, so, mo, h, mod3, mod_row, g2, wo, w1, w2, gf, tm, final):
    b_, s_, _ = h.shape
    const = lambda b, i: (0, 0)
    tile = lambda b, i: (b, i, 0)
    if mod_row is None:
        mod_map = lambda b, i: (b, 0, 0)
    else:
        mod_map = lambda b, i: (mod_row, 0, 0)
    once = pl.Buffered(1)
    return pl.pallas_call(
        functools.partial(_mlp_kernel, final=final, ff_chunk=1024),
        grid=(b_, s_ // tm),
        in_specs=[
            pl.BlockSpec((1, tm, FOURIER_WIDTH), tile),
            pl.BlockSpec((1, tm, SWA_Q_WIDTH), tile),
            pl.BlockSpec((1, tm, MLA_HEADS * MLA_V_DIM), tile),
            pl.BlockSpec((1, tm, D_MODEL), tile),
            pl.BlockSpec((1, 1, 6 * D_MODEL), mod_map),
            pl.BlockSpec((1, D_MODEL), const),
            pl.BlockSpec((D_MIX, D_MODEL), const, pipeline_mode=once),
            pl.BlockSpec((D_MODEL, D_FF), const, pipeline_mode=once),
            pl.BlockSpec((D_FF, D_MODEL), const, pipeline_mode=once),
            pl.BlockSpec((1, D_MODEL), const),
        ],
        out_specs=pl.BlockSpec((1, tm, D_MODEL), tile),
        out_shape=jax.ShapeDtypeStruct((b_, s_, D_MODEL), F32),
        compiler_params=_cparams(("parallel", "parallel")),
        name="out_proj_mlp",
    )(fo, so, mo, h, mod3, g2, wo, w1, w2, gf)


def _rot_cols(w, heads, dim):
    w4 = w.reshape(w.shape[0], heads, 2, dim // 2)
    return jnp.stack([-w4[:, :, 1], w4[:, :, 0]], axis=2).reshape(w.shape[0], heads * dim)


def _pad_lanes(w, heads, before, after):
    w3 = w.reshape(w.shape[0], heads, w.shape[1] // heads)
    w3 = jnp.pad(w3, ((0, 0), (0, 0), (before, after)))
    return w3.reshape(w.shape[0], -1)


def _pack_w_in(w_in):
    wq = w_in[:, OFF_SWA_Q:OFF_SWA_K]
    wk = w_in[:, OFF_SWA_K:OFF_SWA_V]
    wkr = w_in[:, OFF_MLA_KR:OFF_MLA_KR + MLA_ROPE_DIM]
    pad_kr = lambda w: _pad_lanes(w, 1, MLA_NOPE_DIM, LANE - MLA_NOPE_DIM - MLA_ROPE_DIM)
    cols = [w_in[:, :OFF_SWA_Q], wk, w_in[:, OFF_MLA_CQ:OFF_MLA_KR], pad_kr(wkr),
            _rot_cols(wk, SWA_KV_HEADS, HEAD_DIM), pad_kr(_rot_cols(wkr, 1, MLA_ROPE_DIM))]
    rows = [wq, _rot_cols(wq, SWA_HEADS, HEAD_DIM), w_in[:, OFF_SWA_V:OFF_MLA_CQ]]
    return jnp.concatenate(cols, axis=1).astype(BF16), jnp.concatenate(rows, axis=1).T.astype(BF16)


def _pack_w_uq(w_uq):
    per = MLA_NOPE_DIM + MLA_ROPE_DIM
    main = _pad_lanes(w_uq, MLA_HEADS, 0, MLA_PAD - per)
    w3 = w_uq.reshape(MLA_Q_RANK, MLA_HEADS, per)
    rope = w3[:, :, MLA_NOPE_DIM:].reshape(MLA_Q_RANK, MLA_HEADS * MLA_ROPE_DIM)
    rot = _pad_lanes(_rot_cols(rope, MLA_HEADS, MLA_ROPE_DIM), MLA_HEADS, MLA_NOPE_DIM, MLA_PAD - per)
    return main.T.astype(BF16), rot.T.astype(BF16)


def _pack_w_ukv(w_ukv):
    w3 = w_ukv.reshape(MLA_KV_RANK, MLA_HEADS, MLA_NOPE_DIM + MLA_V_DIM)
    wk = w3[:, :, :MLA_NOPE_DIM].reshape(MLA_KV_RANK, MLA_HEADS * MLA_NOPE_DIM)
    wv = w3[:, :, MLA_NOPE_DIM:].reshape(MLA_KV_RANK, MLA_HEADS * MLA_V_DIM)
    return _pad_lanes(wk, MLA_HEADS, 0, MLA_PAD - MLA_NOPE_DIM).astype(BF16), wv.T.astype(BF16)


def _rope_table(rows):
    r, col = jnp.meshgrid(jnp.arange(rows, dtype=F32), jnp.arange(GRID_W, dtype=F32), indexing="ij")
    r = r.reshape(-1)
    col = col.reshape(-1)

    def tables(dim):
        n_freq = dim // 4
        inv = ROPE_THETA ** (-jnp.arange(n_freq, dtype=F32) / n_freq)
        ang = jnp.concatenate([r[:, None] * inv[None, :], col[:, None] * inv[None, :]], axis=-1)
        return jnp.cos(ang), jnp.sin(ang)

    ch, sh = tables(HEAD_DIM)
    cr, sr = tables(MLA_ROPE_DIM)
    mla_pad = ((0, 0), (MLA_NOPE_DIM, LANE - MLA_NOPE_DIM - MLA_ROPE_DIM))
    nope_ones = (jnp.arange(LANE) < MLA_NOPE_DIM).astype(F32)[None, :]
    return jnp.concatenate([jnp.tile(ch, (1, 4)), jnp.tile(sh, (1, 4)),
                            jnp.pad(jnp.tile(cr, (1, 2)), mla_pad) + nope_ones,
                            jnp.pad(jnp.tile(sr, (1, 2)), mla_pad)], axis=-1)


def _identity_rope_table(n):
    ones = jnp.ones((n, LANE), F32)
    zeros = jnp.zeros((n, LANE), F32)
    m = jnp.concatenate([jnp.ones((n, MLA_NOPE_DIM + MLA_ROPE_DIM), F32),
                         jnp.zeros((n, LANE - MLA_NOPE_DIM - MLA_ROPE_DIM), F32)], axis=-1)
    return jnp.concatenate([ones, zeros, m, zeros], axis=-1)


def _dft_tables(n):
    k = jnp.arange(n, dtype=jnp.int32)
    ks = (k[:, None] * k[None, :]) % n
    ang = ks.astype(F32) * (2.0 * math.pi / n)
    scale = 1.0 / math.sqrt(n)
    return (jnp.cos(ang) * scale).astype(BF16), (-jnp.sin(ang) * scale).astype(BF16)


def kernel(x, c, ctx, c_ctx, w_ada, b_ada, norm1_g, norm2_g, w_in, w_fourier, swa_sink, mla_q_norm, w_uq,
           mla_kv_norm, w_ukv, w_out, w_mlp1, w_mlp2, final_norm_g):
    b_, s_, _ = x.shape
    lc = ctx.shape[1]
    depth = w_ada.shape[0]
    rows = s_ // GRID_W

    mod_rows = 16
    cc = jnp.concatenate([c, c_ctx[None, :], jnp.zeros((mod_rows - b_ - 1, D_MODEL), F32)], axis=0)
    mod_all = _ada_call(cc, w_ada, b_ada)

    rope = _rope_table(rows)
    rope_c = _identity_rope_table(lc)
    to_feature_major = lambda t: jnp.concatenate([t[:, 2 * LANE:], t[:, :2 * LANE]], axis=1).T
    ropet = to_feature_major(rope)
    ropet_c = to_feature_major(rope_c)
    fft_tabs = _fft_tables(s_, GRID_W, FOURIER_WIDTH)
    ctab_c, stab_c = _dft_tables(lc)
    kk = jnp.arange(FOURIER_GROUP_DIM, dtype=jnp.int32)
    ang64 = ((kk[:, None] * kk[None, :]) % FOURIER_GROUP_DIM).astype(F32) * (2.0 * math.pi / FOURIER_GROUP_DIM)
    c64, s64 = jnp.cos(ang64), jnp.sin(ang64)

    h, hc = x, ctx
    for l in range(depth):
        last = l == depth - 1
        mod3 = mod_all[l].reshape(mod_rows, 1, 6 * D_MODEL)
        g1 = norm1_g[l].reshape(1, D_MODEL)
        g2 = norm2_g[l].reshape(1, D_MODEL)
        gf = final_norm_g.reshape(1, D_MODEL)
        w1, w1t = _pack_w_in(w_in[l])
        wuq, wuqr = _pack_w_uq(w_uq[l])
        wukvk, wukvv = _pack_w_ukv(w_ukv[l])
        wcs = _fw_call(c64, s64, w_fourier[l]).astype(BF16)
        qn = mla_q_norm[l].reshape(1, MLA_Q_RANK)
        kvn = mla_kv_norm[l].reshape(1, MLA_KV_RANK)
        wo = w_out[l].astype(BF16)
        wm1 = w_mlp1[l].astype(BF16)
        wm2 = w_mlp2[l].astype(BF16)
        sink = swa_sink[l]

        fu, fv, sq, sk, sv, mq, mk, mv = _proj_call(h, mod3, None, g1, w1, w1t, rope, ropet, wcs, qn, wuq, wuqr,
                                                    kvn, wukvk, wukvv, tm=MLA_TK)
        fuc, fvc, sqc, skc, svc, mqc, mkc, mvc = _proj_call(hc, mod3, b_, g1, w1, w1t, rope_c, ropet_c, wcs, qn,
                                                            wuq, wuqr, kvn, wukvk, wukvv, tm=lc)

        fo = _fft_call(fft_tabs, fu, fv)
        so = _swa_call(sink, sq, sk, sv, skc, svc)
        mo = _mla_call(mq, mk, mv, mkc, mvc, tq=512)
        h = _mlp_call(fo, so, mo, h, mod3, None, g2, wo, wm1, wm2, gf, tm=512, final=last)

        if not last:
            foc = _dft_call(ctab_c, stab_c, fuc, fvc, tk=lc)
            soc = _swa_call(sink, sqc, None, None, skc, svc)
            moc = _mla_call(mqc, None, None, mkc, mvc, tq=lc)
            hc = _mlp_call(foc, soc, moc, hc, mod3, b_, g2, wo, wm1, wm2, gf, tm=lc, final=False)
    return h
```

```python
import functools
import math

import jax
import jax.numpy as jnp
from jax import lax
from jax.experimental import pallas as pl
from jax.experimental.pallas import tpu as pltpu

D_MODEL = 1024
GRID_W = 64
HEAD_DIM = 64
FOURIER_GROUPS = 4
FOURIER_GROUP_DIM = 64
FOURIER_WIDTH = FOURIER_GROUPS * FOURIER_GROUP_DIM
SWA_HEADS = 6
SWA_KV_HEADS = 2
SWA_GROUP = SWA_HEADS // SWA_KV_HEADS
SWA_Q_WIDTH = SWA_HEADS * HEAD_DIM
SWA_KV_WIDTH = SWA_KV_HEADS * HEAD_DIM
WINDOW = 128
BLOCK = 128
MLA_HEADS = 6
MLA_NOPE_DIM = 64
MLA_ROPE_DIM = 32
MLA_V_DIM = 64
MLA_Q_RANK = 256
MLA_KV_RANK = 128
MLA_SCALE = (MLA_NOPE_DIM + MLA_ROPE_DIM) ** -0.5
SWA_SCALE = HEAD_DIM ** -0.5
D_MIX = FOURIER_WIDTH + SWA_Q_WIDTH + MLA_HEADS * MLA_V_DIM
OFF_SWA_Q = FOURIER_WIDTH
OFF_SWA_K = OFF_SWA_Q + SWA_Q_WIDTH
OFF_SWA_V = OFF_SWA_K + SWA_KV_WIDTH
OFF_MLA_CQ = OFF_SWA_V + SWA_KV_WIDTH
OFF_MLA_CKV = OFF_MLA_CQ + MLA_Q_RANK
OFF_MLA_KR = OFF_MLA_CKV + MLA_KV_RANK
D_FF = 4 * D_MODEL
ROPE_THETA = 10000.0
NORM_EPS = 1e-6
NEG_INF = -1e30

LANE = 128
MLA_PAD = LANE

C_F = 0
C_K = C_F + FOURIER_WIDTH
C_CQ = C_K + SWA_KV_WIDTH
C_CKV = C_CQ + MLA_Q_RANK
C_KR = C_CKV + MLA_KV_RANK
C_KROT = C_KR + LANE
C_KRROT = C_KROT + SWA_KV_WIDTH
W1_COLS = C_KRROT + LANE
R_Q = 0
R_V = R_Q + SWA_Q_WIDTH
W1T_ROWS = R_V + SWA_KV_WIDTH

BF16 = jnp.bfloat16
F32 = jnp.float32
LOG2E = math.log2(math.e)
_NT = (((1,), (1,)), ((), ()))
MLA_TK = 512
MLA_VT_ROWS = MLA_V_DIM + 16
MLA_S_SLOTS = 2
SWA_VT_ROWS = HEAD_DIM + 16
SWA_TQ = 256

VMEM_LIMIT = 56 * 1024 * 1024


def _cparams(sem):
    return pltpu.CompilerParams(dimension_semantics=sem, vmem_limit_bytes=VMEM_LIMIT)


def _rms(x, g):
    ms = jnp.mean(x * x, axis=-1, keepdims=True)
    return x * lax.rsqrt(ms + NORM_EPS) * g


def _ada_kernel(cc_ref, w_ref, b_ref, o_ref):
    cc = cc_ref[...]
    s = cc * jax.nn.sigmoid(cc)
    o_ref[0] = jnp.dot(s, w_ref[0], preferred_element_type=F32,
                       precision=lax.Precision.HIGHEST) + b_ref[0]


def _ada_call(cc, w_ada, b_ada):
    depth = w_ada.shape[0]
    rows = cc.shape[0]
    tn = 1536
    n_out = w_ada.shape[2]
    return pl.pallas_call(
        _ada_kernel,
        grid=(depth, n_out // tn),
        in_specs=[
            pl.BlockSpec((rows, D_MODEL), lambda l, j: (0, 0)),
            pl.BlockSpec((1, D_MODEL, tn), lambda l, j: (l, 0, j)),
            pl.BlockSpec((1, 1, tn), lambda l, j: (l, 0, j)),
        ],
        out_specs=pl.BlockSpec((1, rows, tn), lambda l, j: (l, 0, j)),
        out_shape=jax.ShapeDtypeStruct((depth, rows, n_out), F32),
        compiler_params=_cparams(("arbitrary", "arbitrary")),
        name="ada_mod",
    )(cc, w_ada, b_ada.reshape(depth, 1, n_out))


def _fw_kernel(c_ref, s_ref, w_ref, o_ref):
    o_ref[...] = jnp.zeros(o_ref.shape, o_ref.dtype)
    inv = 1.0 / math.sqrt(FOURIER_GROUP_DIM)
    for g in range(FOURIER_GROUPS):
        w = w_ref[g]
        a = jnp.dot(c_ref[...], w, preferred_element_type=F32, precision=lax.Precision.HIGHEST) * inv
        b = jnp.dot(s_ref[...], w, preferred_element_type=F32, precision=lax.Precision.HIGHEST) * inv
        lo = g * FOURIER_GROUP_DIM
        hi = lo + FOURIER_GROUP_DIM
        o_ref[lo:hi, lo:hi] = a
        o_ref[lo:hi, FOURIER_WIDTH + lo:FOURIER_WIDTH + hi] = b


def _fw_call(c64, s64, w_f):
    return pl.pallas_call(
        _fw_kernel,
        out_shape=jax.ShapeDtypeStruct((FOURIER_WIDTH, 2 * FOURIER_WIDTH), F32),
        name="fourier_w",
    )(c64, s64, w_f)


def _ones_row(rows, cols):
    return (lax.broadcasted_iota(jnp.int32, (rows, cols), 0) == 0).astype(BF16)


def _proj_kernel(h_ref, mod_ref, g1_ref, w1_ref, w1t_ref, rope_ref, ropet_ref, wcs_ref, qn_ref, wuq_ref,
                 kvn_ref, wukvk_ref, wukvv_ref,
                 u_o, v_o, sq_o, sk_o, sv_o, mq_o, mk_o, mv_o):
    x = h_ref[0]
    tm = x.shape[0]
    mod = mod_ref[0]
    sh = mod[:, 0:D_MODEL]
    sc = mod[:, D_MODEL:2 * D_MODEL]
    y = (_rms(x, g1_ref[...]) * (1.0 + sc) + sh).astype(BF16)
    u = jnp.dot(y, w1_ref[...], preferred_element_type=F32)
    ut = lax.dot_general(w1t_ref[...], y, _NT, preferred_element_type=F32)

    cos_h = rope_ref[:, 0:LANE]
    sin_h = rope_ref[:, LANE:2 * LANE]
    cos_m = rope_ref[:, 2 * LANE:3 * LANE]
    sin_m = rope_ref[:, 3 * LANE:4 * LANE]
    cos_mt = ropet_ref[0:LANE, :]
    sin_mt = ropet_ref[LANE:2 * LANE, :]
    cos_ht = ropet_ref[2 * LANE:3 * LANE, :]
    sin_ht = ropet_ref[3 * LANE:4 * LANE, :]

    f = u[:, C_F:C_F + FOURIER_WIDTH].astype(BF16)
    uv = jnp.dot(f, wcs_ref[...], preferred_element_type=F32).astype(BF16)
    u_o[0] = uv[:, :FOURIER_WIDTH]
    v_o[0] = uv[:, FOURIER_WIDTH:]

    def rope_rows(x1, x2, c, s, scale):
        return ((x1 * c - x2 * s) * scale).astype(BF16), ((x2 * c + x1 * s) * scale).astype(BF16)

    half = HEAD_DIM // 2
    c_h, s_h = cos_ht[0:half], sin_ht[0:half]
    for hh in range(SWA_HEADS):
        r0 = R_Q + HEAD_DIM * hh
        lo, hi = rope_rows(ut[r0:r0 + half], ut[r0 + half:r0 + HEAD_DIM], c_h, s_h, SWA_SCALE * LOG2E)
        sq_o[0, HEAD_DIM * hh:HEAD_DIM * hh + half, :] = lo
        sq_o[0, HEAD_DIM * hh + half:HEAD_DIM * (hh + 1), :] = hi
    kk = (u[:, C_K:C_K + LANE] * cos_h + u[:, C_KROT:C_KROT + LANE] * sin_h).astype(BF16)
    ones_swa = _ones_row(SWA_VT_ROWS - HEAD_DIM, tm)
    for kh in range(SWA_KV_HEADS):
        sk_o[0, kh] = kk[:, HEAD_DIM * kh:HEAD_DIM * (kh + 1)]
        sv_o[0, kh, 0:HEAD_DIM, :] = ut[R_V + HEAD_DIM * kh:R_V + HEAD_DIM * (kh + 1)].astype(BF16)
        sv_o[0, kh, HEAD_DIM:SWA_VT_ROWS, :] = ones_swa

    cqn = _rms(u[:, C_CQ:C_CQ + MLA_Q_RANK], qn_ref[...]).astype(BF16)
    qa = lax.dot_general(wuq_ref[...], cqn, _NT, preferred_element_type=F32)
    rh = MLA_ROPE_DIM // 2
    c_m, s_m = cos_mt[MLA_NOPE_DIM:MLA_NOPE_DIM + rh], sin_mt[MLA_NOPE_DIM:MLA_NOPE_DIM + rh]
    q_scale = MLA_SCALE * LOG2E
    for hh in range(MLA_HEADS):
        r0 = MLA_PAD * hh
        r1 = r0 + MLA_NOPE_DIM
        lo, hi = rope_rows(qa[r1:r1 + rh], qa[r1 + rh:r1 + 2 * rh], c_m, s_m, q_scale)
        mq_o[0, r0:r1, :] = (qa[r0:r1] * q_scale).astype(BF16)
        mq_o[0, r1:r1 + rh, :] = lo
        mq_o[0, r1 + rh:r1 + 2 * rh, :] = hi
        mq_o[0, r1 + 2 * rh:r0 + MLA_PAD, :] = jnp.zeros((MLA_PAD - MLA_NOPE_DIM - 2 * rh, tm), BF16)

    ckvn = _rms(u[:, C_CKV:C_CKV + MLA_KV_RANK], kvn_ref[...]).astype(BF16)
    kr = u[:, C_KR:C_KR + LANE] * cos_m + u[:, C_KRROT:C_KRROT + LANE] * sin_m
    kvk = jnp.dot(ckvn, wukvk_ref[...], preferred_element_type=F32)
    vt = lax.dot_general(wukvv_ref[...], ckvn, _NT, preferred_element_type=F32).astype(BF16)
    ones_mla = _ones_row(MLA_VT_ROWS - MLA_V_DIM, tm)
    for hh in range(MLA_HEADS):
        mk_o[0, hh] = (kvk[:, MLA_PAD * hh:MLA_PAD * (hh + 1)] + kr).astype(BF16)
        mv_o[0, hh, 0, 0:MLA_V_DIM, :] = vt[MLA_V_DIM * hh:MLA_V_DIM * (hh + 1)]
        mv_o[0, hh, 0, MLA_V_DIM:MLA_VT_ROWS, :] = ones_mla


def _proj_call(h, mod3, mod_row, g1, w1, w1t, rope, ropet, wcs, qn, wuq, kvn, wukvk, wukvv, tm):
    b_, s_, _ = h.shape
    const = lambda b, i: (0, 0)
    if mod_row is None:
        mod_map = lambda b, i: (b, 0, 0)
    else:
        mod_map = lambda b, i: (mod_row, 0, 0)
    out_shapes = (
        jax.ShapeDtypeStruct((b_, s_, FOURIER_WIDTH), BF16),
        jax.ShapeDtypeStruct((b_, s_, FOURIER_WIDTH), BF16),
        jax.ShapeDtypeStruct((b_, SWA_Q_WIDTH, s_), BF16),
        jax.ShapeDtypeStruct((b_, SWA_KV_HEADS, s_, HEAD_DIM), BF16),
        jax.ShapeDtypeStruct((b_, SWA_KV_HEADS, SWA_VT_ROWS, s_), BF16),
        jax.ShapeDtypeStruct((b_, MLA_HEADS * MLA_PAD, s_), BF16),
        jax.ShapeDtypeStruct((b_, MLA_HEADS, s_, MLA_PAD), BF16),
        jax.ShapeDtypeStruct((b_, MLA_HEADS, s_ // tm, MLA_VT_ROWS, tm), BF16),
    )
    out_specs = (
        pl.BlockSpec((1, tm, FOURIER_WIDTH), lambda b, i: (b, i, 0)),
        pl.BlockSpec((1, tm, FOURIER_WIDTH), lambda b, i: (b, i, 0)),
        pl.BlockSpec((1, SWA_Q_WIDTH, tm), lambda b, i: (b, 0, i)),
        pl.BlockSpec((1, SWA_KV_HEADS, tm, HEAD_DIM), lambda b, i: (b, 0, i, 0)),
        pl.BlockSpec((1, SWA_KV_HEADS, SWA_VT_ROWS, tm), lambda b, i: (b, 0, 0, i)),
        pl.BlockSpec((1, MLA_HEADS * MLA_PAD, tm), lambda b, i: (b, 0, i)),
        pl.BlockSpec((1, MLA_HEADS, tm, MLA_PAD), lambda b, i: (b, 0, i, 0)),
        pl.BlockSpec((1, MLA_HEADS, 1, MLA_VT_ROWS, tm), lambda b, i: (b, 0, i, 0, 0)),
    )
    return pl.pallas_call(
        _proj_kernel,
        grid=(b_, s_ // tm),
        in_specs=[
            pl.BlockSpec((1, tm, D_MODEL), lambda b, i: (b, i, 0)),
            pl.BlockSpec((1, 1, 6 * D_MODEL), mod_map),
            pl.BlockSpec((1, D_MODEL), const),
            pl.BlockSpec((D_MODEL, W1_COLS), const),
            pl.BlockSpec((W1T_ROWS, D_MODEL), const),
            pl.BlockSpec((tm, 4 * LANE), lambda b, i: (i, 0)),
            pl.BlockSpec((4 * LANE, tm), lambda b, i: (0, i)),
            pl.BlockSpec((FOURIER_WIDTH, 2 * FOURIER_WIDTH), const),
            pl.BlockSpec((1, MLA_Q_RANK), const),
            pl.BlockSpec((MLA_HEADS * MLA_PAD, MLA_Q_RANK), const),
            pl.BlockSpec((1, MLA_KV_RANK), const),
            pl.BlockSpec((MLA_KV_RANK, MLA_HEADS * MLA_PAD), const),
            pl.BlockSpec((MLA_HEADS * MLA_V_DIM, MLA_KV_RANK), const),
        ],
        out_specs=out_specs,
        out_shape=out_shapes,
        compiler_params=_cparams(("parallel", "parallel")),
        name="norm_in_proj",
    )(h, mod3, g1, w1, w1t, rope, ropet, wcs, qn, wuq, kvn, wukvk, wukvv)


def _dft_kernel(c_ref, s_ref, u_ref, v_ref, o_ref):
    o = jnp.dot(c_ref[...], u_ref[0], preferred_element_type=F32)
    o = o + jnp.dot(s_ref[...], v_ref[0], preferred_element_type=F32)
    o_ref[0] = o.astype(o_ref.dtype)


def _dft_call(ctab, stab, u, v, tk):
    b_, s_, _ = u.shape
    whole = pl.BlockSpec((1, s_, FOURIER_WIDTH), lambda i, b: (b, 0, 0))
    return pl.pallas_call(
        _dft_kernel,
        grid=(s_ // tk, b_),
        in_specs=[
            pl.BlockSpec((tk, s_), lambda i, b: (i, 0)),
            pl.BlockSpec((tk, s_), lambda i, b: (i, 0)),
            whole, whole,
        ],
        out_specs=pl.BlockSpec((1, tk, FOURIER_WIDTH), lambda i, b: (b, i, 0)),
        out_shape=jax.ShapeDtypeStruct((b_, s_, FOURIER_WIDTH), BF16),
        compiler_params=_cparams(("parallel", "arbitrary")),
        name="fourier_dft",
    )(ctab, stab, u, v)


def _fft1_kernel(m1_ref, m2_ref, tc_ref, ts_ref, u_ref, v_ref, bre_ref, bim_ref):
    n1 = tc_ref.shape[0]
    a = jnp.dot(m1_ref[...], u_ref[0], preferred_element_type=F32)
    a = a + jnp.dot(m2_ref[...], v_ref[0], preferred_element_type=F32)
    are, aim = a[0:n1], a[n1:2 * n1]
    tc, ts = tc_ref[...], ts_ref[...]
    bre_ref[0] = (are * tc + aim * ts).astype(bre_ref.dtype)
    bim_ref[0] = (aim * tc - are * ts).astype(bim_ref.dtype)


def _fft2_kernel(g1_ref, g2_ref, bre_ref, bim_ref, o_ref):
    n2 = g1_ref.shape[1]
    for j in range(bre_ref.shape[1] // n2):
        rows = slice(n2 * j, n2 * (j + 1))
        x = jnp.dot(g1_ref[...], bre_ref[0, rows, :], preferred_element_type=F32)
        x = x + jnp.dot(g2_ref[...], bim_ref[0, rows, :], preferred_element_type=F32)
        o_ref[0, :, FOURIER_WIDTH * j:FOURIER_WIDTH * (j + 1)] = x.astype(o_ref.dtype)


def _fft_call(tabs, u, v):
    m1, m2, tc, ts, g1, g2 = tabs
    b_, s_, w = u.shape
    n1 = tc.shape[0]
    n2 = s_ // n1
    tn = 16 * w
    u2 = u.reshape(b_, n1, n2 * w)
    v2 = v.reshape(b_, n1, n2 * w)
    tile = pl.BlockSpec((1, n1, tn), lambda t, b: (b, 0, t))
    tab = pl.BlockSpec((n1, tn), lambda t, b: (0, t))
    mat = pl.BlockSpec((2 * n1, n1), lambda t, b: (0, 0))
    bre, bim = pl.pallas_call(
        _fft1_kernel,
        grid=(n2 * w // tn, b_),
        in_specs=[mat, mat, tab, tab, tile, tile],
        out_specs=(tile, tile),
        out_shape=(jax.ShapeDtypeStruct((b_, n1, n2 * w), BF16),) * 2,
        compiler_params=_cparams(("parallel", "arbitrary")),
        name="fourier_fft1",
    )(m1, m2, tc, ts, u2, v2)
    k1_per_step = 16
    rows = k1_per_step * n2
    blk = pl.BlockSpec((1, rows, w), lambda b, t: (b, t, 0))
    mat2 = pl.BlockSpec((n2, n2), lambda b, t: (0, 0))
    out = pl.pallas_call(
        _fft2_kernel,
        grid=(b_, n1 // k1_per_step),
        in_specs=[mat2, mat2, blk, blk],
        out_specs=pl.BlockSpec((1, n2, k1_per_step * w), lambda b, t: (b, 0, t)),
        out_shape=jax.ShapeDtypeStruct((b_, n2, n1 * w), BF16),
        compiler_params=_cparams(("parallel", "parallel")),
        name="fourier_fft2",
    )(g1, g2, bre.reshape(b_, s_, w), bim.reshape(b_, s_, w))
    return out.reshape(b_, s_, w)


def _fft_tables(s_, n1, w):
    n2 = s_ // n1
    k1 = jnp.arange(n1, dtype=jnp.int32)
    a1 = ((k1[:, None] * k1[None, :]) % n1).astype(F32) * (2.0 * math.pi / n1)
    c1, s1 = jnp.cos(a1), jnp.sin(a1)
    m1 = jnp.concatenate([c1, -s1], axis=0).astype(BF16)
    m2 = jnp.concatenate([-s1, -c1], axis=0).astype(BF16)
    s2 = jnp.arange(n2, dtype=jnp.int32)
    th = ((k1[:, None] * s2[None, :]) % s_).astype(F32) * (2.0 * math.pi / s_)
    tc = jnp.repeat(jnp.cos(th), w, axis=1)
    ts = jnp.repeat(jnp.sin(th), w, axis=1)
    a2 = ((s2[:, None] * s2[None, :]) % n2).astype(F32) * (2.0 * math.pi / n2)
    scale = 1.0 / math.sqrt(s_)
    g1 = (jnp.cos(a2) * scale).astype(BF16)
    g2 = (jnp.sin(a2) * scale).astype(BF16)
    return m1, m2, tc, ts, g1, g2


def _mla_kernel(*refs, n_chunks, heads_per_step):
    hps = heads_per_step
    n_in = 6 if n_chunks else 4
    if n_chunks:
        zero_ref, qt_ref, k_ref, vt_ref, kc_ref, vct_ref = refs[:n_in]
    else:
        zero_ref, qt_ref, kc_ref, vct_ref = refs[:n_in]
    o_ref = refs[n_in]
    row0 = pl.multiple_of(zero_ref[0], MLA_TK)
    scratch = refs[n_in + 1:]
    s_refs = [scratch[MLA_S_SLOTS * hh:MLA_S_SLOTS * (hh + 1)] for hh in range(hps)]
    tq = qt_ref.shape[2]
    lc = kc_ref.shape[2]
    qts = [qt_ref[0, MLA_PAD * hh:MLA_PAD * (hh + 1), :] for hh in range(hps)]

    def logits(ks, idx, rows):
        cms = []
        for hh in range(hps):
            s = jnp.dot(ks(hh), qts[hh], preferred_element_type=F32)
            s_refs[hh][idx % MLA_S_SLOTS][0:rows, :] = s
            cms.append(jnp.max(s, axis=0, keepdims=True))
        return tuple(cms)

    def accumulate(idx, rows, vts, carries, cms):
        out = []
        for hh in range(hps):
            m, acc = carries[hh]
            m_new = jnp.maximum(m, cms[hh])
            alpha = jnp.exp2(m - m_new)
            s = s_refs[hh][idx % MLA_S_SLOTS][pl.ds(row0, rows), :]
            p = jnp.exp2(s - m_new).astype(BF16)
            acc = alpha * acc + jnp.dot(vts(hh), p, preferred_element_type=F32)
            out.append((m_new, acc))
        return tuple(out)

    def k_lat(c):
        return lambda hh: k_ref[0, hh, c * MLA_TK:(c + 1) * MLA_TK, :]

    def v_lat(c):
        return lambda hh: vt_ref[0, hh, c]

    k_ctx = lambda hh: kc_ref[0, hh]
    v_ctx = lambda hh: vct_ref[0, hh, 0]

    init = (jnp.full((1, tq), NEG_INF, F32), jnp.zeros((MLA_VT_ROWS, tq), F32))
    carries = tuple(init for _ in range(hps))
    cms = logits(k_ctx, 0, lc)
    if n_chunks:
        prev_cms, prev_rows, prev_v = cms, lc, v_ctx
        for c in range(n_chunks):
            new_cms = logits(k_lat(c), c + 1, MLA_TK)
            carries = accumulate(c, prev_rows, prev_v, carries, prev_cms)
            prev_cms, prev_rows, prev_v = new_cms, MLA_TK, v_lat(c)
        carries = accumulate(n_chunks, MLA_TK, prev_v, carries, prev_cms)
    else:
        carries = accumulate(0, lc, v_ctx, carries, cms)
    o = jnp.concatenate([acc[0:MLA_V_DIM] / acc[MLA_V_DIM:MLA_V_DIM + 1] for (_, acc) in carries],
                        axis=0)
    o_ref[0] = o.T.astype(o_ref.dtype)


def _mla_call(qt, k, vt, kc, vct, tq):
    b_, _, sq = qt.shape
    h_ = kc.shape[1]
    lc = kc.shape[2]
    hps = 2
    n_chunks = 0 if k is None else k.shape[2] // MLA_TK
    in_specs = [pl.BlockSpec(memory_space=pltpu.SMEM),
                pl.BlockSpec((1, hps * MLA_PAD, tq), lambda b, p, i: (b, p, i))]
    args = [jnp.zeros((1,), jnp.int32), qt]
    if k is not None:
        sk = k.shape[2]
        in_specs += [pl.BlockSpec((1, hps, sk, MLA_PAD), lambda b, p, i: (b, p, 0, 0)),
                     pl.BlockSpec((1, hps, n_chunks, MLA_VT_ROWS, MLA_TK), lambda b, p, i: (b, p, 0, 0, 0))]
        args += [k, vt]
    in_specs += [pl.BlockSpec((1, hps, lc, MLA_PAD), lambda b, p, i: (b, p, 0, 0)),
                 pl.BlockSpec((1, hps, 1, MLA_VT_ROWS, lc), lambda b, p, i: (b, p, 0, 0, 0))]
    args += [kc, vct]
    return pl.pallas_call(
        functools.partial(_mla_kernel, n_chunks=n_chunks, heads_per_step=hps),
        grid=(b_, h_ // hps, sq // tq),
        in_specs=in_specs,
        out_specs=pl.BlockSpec((1, tq, hps * MLA_V_DIM), lambda b, p, i: (b, i, p)),
        out_shape=jax.ShapeDtypeStruct((b_, sq, h_ * MLA_V_DIM), BF16),
        scratch_shapes=[pltpu.VMEM((MLA_TK if n_chunks else lc, tq), F32)] * (MLA_S_SLOTS * hps),
        compiler_params=_cparams(("parallel", "parallel", "arbitrary")),
        name="mla_attn",
    )(*args)


def _swa_kernel(*refs, n_local, s_len):
    zero_ref, sink_ref, qt_ref = refs[:3]
    k_refs = refs[3:3 + n_local]
    vt_refs = refs[3 + n_local:3 + 2 * n_local]
    kc_ref, vct_ref, o_ref = refs[3 + 2 * n_local:6 + 2 * n_local]
    s_refs = refs[6 + 2 * n_local:]
    row0 = pl.multiple_of(zero_ref[0], BLOCK)
    tq = qt_ref.shape[2]
    lc = kc_ref.shape[2]
    nk = n_local * BLOCK
    if n_local:
        n = pl.program_id(1)
        kpos = n * tq - BLOCK + lax.broadcasted_iota(jnp.int32, (nk, tq), 0)
        qpos = n * tq + lax.broadcasted_iota(jnp.int32, (nk, tq), 1)
        valid = (jnp.abs(kpos - qpos) <= WINDOW) & (kpos >= 0) & (kpos < s_len)
    sinks = [sink_ref[hh] * LOG2E for hh in range(SWA_HEADS)]

    def keys(kh):
        kc = kc_ref[0, kh]
        vct = vct_ref[0, kh]
        if not n_local:
            return kc, vct
        return (jnp.concatenate([kc] + [r[0, kh] for r in k_refs], axis=0),
                jnp.concatenate([vct] + [r[0, kh] for r in vt_refs], axis=1))

    def logits(hh, k_all):
        qt = qt_ref[0, HEAD_DIM * hh:HEAD_DIM * (hh + 1), :]
        s_ref = s_refs[hh % 2]
        s_c = jnp.dot(k_all[0:lc], qt, preferred_element_type=F32)
        s_ref[0:lc, :] = s_c
        m = jnp.maximum(jnp.max(s_c, axis=0, keepdims=True), sinks[hh])
        if n_local:
            s_l = jnp.where(valid, jnp.dot(k_all[lc:], qt, preferred_element_type=F32), NEG_INF)
            s_ref[lc:lc + nk, :] = s_l
            m = jnp.maximum(m, jnp.max(s_l, axis=0, keepdims=True))
        return m

    def accumulate(hh, vt_all, m):
        p = jnp.exp2(s_refs[hh % 2][pl.ds(row0, lc + nk), :] - m).astype(BF16)
        acc = jnp.dot(vt_all, p, preferred_element_type=F32)
        l = acc[HEAD_DIM:HEAD_DIM + 1] + jnp.exp2(sinks[hh] - m)
        return acc[0:HEAD_DIM] / l

    kv = [keys(kh) for kh in range(SWA_KV_HEADS)]
    outs = []
    m_prev = logits(0, kv[0][0])
    for hh in range(1, SWA_HEADS):
        m_next = logits(hh, kv[hh // SWA_GROUP][0])
        outs.append(accumulate(hh - 1, kv[(hh - 1) // SWA_GROUP][1], m_prev))
        m_prev = m_next
    outs.append(accumulate(SWA_HEADS - 1, kv[SWA_KV_HEADS - 1][1], m_prev))
    o = jnp.concatenate(outs, axis=0)
    o_ref[0] = o.T.astype(o_ref.dtype)


def _swa_call(sink, qt, k, vt, kc, vct):
    b_, _, sq = qt.shape
    lc = kc.shape[2]
    local = k is not None
    tq = SWA_TQ if local else sq
    nb = sq // tq
    n_local = tq // BLOCK + 2 if local else 0
    in_specs = [pl.BlockSpec(memory_space=pltpu.SMEM), pl.BlockSpec(memory_space=pltpu.SMEM),
                pl.BlockSpec((1, SWA_Q_WIDTH, tq), lambda b, n: (b, 0, n))]
    args = [jnp.zeros((1,), jnp.int32), sink, qt]
    if local:
        nkb = sq // BLOCK
        per = tq // BLOCK

        def blk_idx(n, j):
            return jnp.clip(n * per - 1 + j, 0, nkb - 1)

        in_specs += [pl.BlockSpec((1, SWA_KV_HEADS, BLOCK, HEAD_DIM),
                                  functools.partial(lambda b, n, j: (b, 0, blk_idx(n, j), 0), j=j))
                     for j in range(n_local)]
        in_specs += [pl.BlockSpec((1, SWA_KV_HEADS, SWA_VT_ROWS, BLOCK),
                                  functools.partial(lambda b, n, j: (b, 0, 0, blk_idx(n, j)), j=j))
                     for j in range(n_local)]
        args += [k] * n_local + [vt] * n_local
    in_specs += [pl.BlockSpec((1, SWA_KV_HEADS, lc, HEAD_DIM), lambda b, n: (b, 0, 0, 0)),
                 pl.BlockSpec((1, SWA_KV_HEADS, SWA_VT_ROWS, lc), lambda b, n: (b, 0, 0, 0))]
    args += [kc, vct]
    return pl.pallas_call(
        functools.partial(_swa_kernel, n_local=n_local, s_len=sq),
        grid=(b_, nb),
        in_specs=in_specs,
        out_specs=pl.BlockSpec((1, tq, SWA_Q_WIDTH), lambda b, n: (b, n, 0)),
        out_shape=jax.ShapeDtypeStruct((b_, sq, SWA_Q_WIDTH), BF16),
        scratch_shapes=[pltpu.VMEM((lc + n_local * BLOCK, tq), F32)] * 2,
        compiler_params=_cparams(("parallel", "arbitrary")),
        name="swa_attn",
    )(*args)


def _mlp_kernel(fo_ref, so_ref, mo_ref, h_ref, mod_ref, g2_ref, wo_ref, w1_ref, w2_ref, gf_ref, o_ref, *,
                final, ff_chunk):
    mod = mod_ref[0]
    g1 = mod[:, 2 * D_MODEL:3 * D_MODEL]
    sh2 = mod[:, 3 * D_MODEL:4 * D_MODEL]
    sc2 = mod[:, 4 * D_MODEL:5 * D_MODEL]
    g2 = mod[:, 5 * D_MODEL:6 * D_MODEL]
    mix = jnp.concatenate([fo_ref[0], so_ref[0], mo_ref[0]], axis=-1)
    h = h_ref[0] + g1 * jnp.dot(mix, wo_ref[...], preferred_element_type=F32)
    y = (_rms(h, g2_ref[...]) * (1.0 + sc2) + sh2).astype(BF16)
    acc = None
    for c in range(D_FF // ff_chunk):
        a = jnp.dot(y, w1_ref[:, c * ff_chunk:(c + 1) * ff_chunk], preferred_element_type=F32)
        a = jnp.maximum(a, 0.0)
        a = (a * a).astype(BF16)
        part = jnp.dot(a, w2_ref[c * ff_chunk:(c + 1) * ff_chunk, :], preferred_element_type=F32)
        acc = part if acc is None else acc + part
    h = h + g2 * acc
    if final:
        h = _rms(h, gf_ref[...])
    o_ref[0] = h


def _mlp_call(fo, so, mo, h, mod3, mod_row, g2, wo, w1, w2, gf, tm, final):
    b_, s_, _ = h.shape
    const = lambda b, i: (0, 0)
    tile = lambda b, i: (b, i, 0)
    if mod_row is None:
        mod_map = lambda b, i: (b, 0, 0)
    else:
        mod_map = lambda b, i: (mod_row, 0, 0)
    once = pl.Buffered(1)
    return pl.pallas_call(
        functools.partial(_mlp_kernel, final=final, ff_chunk=1024),
        grid=(b_, s_ // tm),
        in_specs=[
            pl.BlockSpec((1, tm, FOURIER_WIDTH), tile),
            pl.BlockSpec((1, tm, SWA_Q_WIDTH), tile),
            pl.BlockSpec((1, tm, MLA_HEADS * MLA_V_DIM), tile),
            pl.BlockSpec((1, tm, D_MODEL), tile),
            pl.BlockSpec((1, 1, 6 * D_MODEL), mod_map),
            pl.BlockSpec((1, D_MODEL), const),
            pl.BlockSpec((D_MIX, D_MODEL), const, pipeline_mode=once),
            pl.BlockSpec((D_MODEL, D_FF), const, pipeline_mode=once),
            pl.BlockSpec((D_FF, D_MODEL), const, pipeline_mode=once),
            pl.BlockSpec((1, D_MODEL), const),
        ],
        out_specs=pl.BlockSpec((1, tm, D_MODEL), tile),
        out_shape=jax.ShapeDtypeStruct((b_, s_, D_MODEL), F32),
        compiler_params=_cparams(("parallel", "parallel")),
        name="out_proj_mlp",
    )(fo, so, mo, h, mod3, g2, wo, w1, w2, gf)


def _rot_cols(w, heads, dim):
    w4 = w.reshape(w.shape[0], heads, 2, dim // 2)
    return jnp.stack([-w4[:, :, 1], w4[:, :, 0]], axis=2).reshape(w.shape[0], heads * dim)


def _pad_lanes(w, heads, before, after):
    w3 = w.reshape(w.shape[0], heads, w.shape[1] // heads)
    w3 = jnp.pad(w3, ((0, 0), (0, 0), (before, after)))
    return w3.reshape(w.shape[0], -1)


def _pack_w_in(w_in):
    wk = w_in[:, OFF_SWA_K:OFF_SWA_V]
    wkr = w_in[:, OFF_MLA_KR:OFF_MLA_KR + MLA_ROPE_DIM]
    pad_kr = lambda w: _pad_lanes(w, 1, MLA_NOPE_DIM, LANE - MLA_NOPE_DIM - MLA_ROPE_DIM)
    cols = [w_in[:, :OFF_SWA_Q], wk, w_in[:, OFF_MLA_CQ:OFF_MLA_KR], pad_kr(wkr),
            _rot_cols(wk, SWA_KV_HEADS, HEAD_DIM), pad_kr(_rot_cols(wkr, 1, MLA_ROPE_DIM))]
    rows = w_in[:, OFF_SWA_Q:OFF_SWA_K], w_in[:, OFF_SWA_V:OFF_MLA_CQ]
    return jnp.concatenate(cols, axis=1).astype(BF16), jnp.concatenate(rows, axis=1).T.astype(BF16)


def _pack_w_uq(w_uq):
    per = MLA_NOPE_DIM + MLA_ROPE_DIM
    return _pad_lanes(w_uq, MLA_HEADS, 0, MLA_PAD - per).T.astype(BF16)


def _pack_w_ukv(w_ukv):
    w3 = w_ukv.reshape(MLA_KV_RANK, MLA_HEADS, MLA_NOPE_DIM + MLA_V_DIM)
    wk = w3[:, :, :MLA_NOPE_DIM].reshape(MLA_KV_RANK, MLA_HEADS * MLA_NOPE_DIM)
    wv = w3[:, :, MLA_NOPE_DIM:].reshape(MLA_KV_RANK, MLA_HEADS * MLA_V_DIM)
    return _pad_lanes(wk, MLA_HEADS, 0, MLA_PAD - MLA_NOPE_DIM).astype(BF16), wv.T.astype(BF16)


def _rope_table(rows):
    r, col = jnp.meshgrid(jnp.arange(rows, dtype=F32), jnp.arange(GRID_W, dtype=F32), indexing="ij")
    r = r.reshape(-1)
    col = col.reshape(-1)

    def tables(dim):
        n_freq = dim // 4
        inv = ROPE_THETA ** (-jnp.arange(n_freq, dtype=F32) / n_freq)
        ang = jnp.concatenate([r[:, None] * inv[None, :], col[:, None] * inv[None, :]], axis=-1)
        return jnp.cos(ang), jnp.sin(ang)

    ch, sh = tables(HEAD_DIM)
    cr, sr = tables(MLA_ROPE_DIM)
    mla_pad = ((0, 0), (MLA_NOPE_DIM, LANE - MLA_NOPE_DIM - MLA_ROPE_DIM))
    nope_ones = (jnp.arange(LANE) < MLA_NOPE_DIM).astype(F32)[None, :]
    return jnp.concatenate([jnp.tile(ch, (1, 4)), jnp.tile(sh, (1, 4)),
                            jnp.pad(jnp.tile(cr, (1, 2)), mla_pad) + nope_ones,
                            jnp.pad(jnp.tile(sr, (1, 2)), mla_pad)], axis=-1)


def _identity_rope_table(n):
    ones = jnp.ones((n, LANE), F32)
    zeros = jnp.zeros((n, LANE), F32)
    m = jnp.concatenate([jnp.ones((n, MLA_NOPE_DIM + MLA_ROPE_DIM), F32),
                         jnp.zeros((n, LANE - MLA_NOPE_DIM - MLA_ROPE_DIM), F32)], axis=-1)
    return jnp.concatenate([ones, zeros, m, zeros], axis=-1)


def _dft_tables(n):
    k = jnp.arange(n, dtype=jnp.int32)
    ks = (k[:, None] * k[None, :]) % n
    ang = ks.astype(F32) * (2.0 * math.pi / n)
    scale = 1.0 / math.sqrt(n)
    return (jnp.cos(ang) * scale).astype(BF16), (-jnp.sin(ang) * scale).astype(BF16)


def kernel(x, c, ctx, c_ctx, w_ada, b_ada, norm1_g, norm2_g, w_in, w_fourier, swa_sink, mla_q_norm, w_uq,
           mla_kv_norm, w_ukv, w_out, w_mlp1, w_mlp2, final_norm_g):
    b_, s_, _ = x.shape
    lc = ctx.shape[1]
    depth = w_ada.shape[0]
    rows = s_ // GRID_W

    mod_rows = 16
    cc = jnp.concatenate([c, c_ctx[None, :], jnp.zeros((mod_rows - b_ - 1, D_MODEL), F32)], axis=0)
    mod_all = _ada_call(cc, w_ada, b_ada)

    rope = _rope_table(rows)
    rope_c = _identity_rope_table(lc)
    to_feature_major = lambda t: jnp.concatenate([t[:, 2 * LANE:], t[:, :2 * LANE]], axis=1).T
    ropet = to_feature_major(rope)
    ropet_c = to_feature_major(rope_c)
    fft_tabs = _fft_tables(s_, GRID_W, FOURIER_WIDTH)
    ctab_c, stab_c = _dft_tables(lc)
    kk = jnp.arange(FOURIER_GROUP_DIM, dtype=jnp.int32)
    ang64 = ((kk[:, None] * kk[None, :]) % FOURIER_GROUP_DIM).astype(F32) * (2.0 * math.pi / FOURIER_GROUP_DIM)
    c64, s64 = jnp.cos(ang64), jnp.sin(ang64)

    h, hc = x, ctx
    for l in range(depth):
        last = l == depth - 1
        mod3 = mod_all[l].reshape(mod_rows, 1, 6 * D_MODEL)
        g1 = norm1_g[l].reshape(1, D_MODEL)
        g2 = norm2_g[l].reshape(1, D_MODEL)
        gf = final_norm_g.reshape(1, D_MODEL)
        w1, w1t = _pack_w_in(w_in[l])
        wuq = _pack_w_uq(w_uq[l])
        wukvk, wukvv = _pack_w_ukv(w_ukv[l])
        wcs = _fw_call(c64, s64, w_fourier[l]).astype(BF16)
        qn = mla_q_norm[l].reshape(1, MLA_Q_RANK)
        kvn = mla_kv_norm[l].reshape(1, MLA_KV_RANK)
        wo = w_out[l].astype(BF16)
        wm1 = w_mlp1[l].astype(BF16)
        wm2 = w_mlp2[l].astype(BF16)
        sink = swa_sink[l]

        fu, fv, sq, sk, sv, mq, mk, mv = _proj_call(h, mod3, None, g1, w1, w1t, rope, ropet, wcs, qn, wuq,
                                                    kvn, wukvk, wukvv, tm=MLA_TK)
        fuc, fvc, sqc, skc, svc, mqc, mkc, mvc = _proj_call(hc, mod3, b_, g1, w1, w1t, rope_c, ropet_c, wcs, qn,
                                                            wuq, kvn, wukvk, wukvv, tm=lc)

        fo = _fft_call(fft_tabs, fu, fv)
        so = _swa_call(sink, sq, sk, sv, skc, svc)
        mo = _mla_call(mq, mk, mv, mkc, mvc, tq=512)
        h = _mlp_call(fo, so, mo, h, mod3, None, g2, wo, wm1, wm2, gf, tm=512, final=last)

        if not last:
            foc = _dft_call(ctab_c, stab_c, fuc, fvc, tk=lc)
            soc = _swa_call(sink, sqc, None, None, skc, svc)
            moc = _mla_call(mqc, None, None, mkc, mvc, tq=lc)
            hc = _mlp_call(foc, soc, moc, hc, mod3, b_, g2, wo, wm1, wm2, gf, tm=lc, final=False)
    return h
```

```python
import functools
import math

import jax
import jax.numpy as jnp
from jax import lax
from jax.experimental import pallas as pl
from jax.experimental.pallas import tpu as pltpu

D_MODEL = 1024
GRID_W = 64
HEAD_DIM = 64
FOURIER_GROUPS = 4
FOURIER_GROUP_DIM = 64
FOURIER_WIDTH = FOURIER_GROUPS * FOURIER_GROUP_DIM
SWA_HEADS = 6
SWA_KV_HEADS = 2
SWA_GROUP = SWA_HEADS // SWA_KV_HEADS
SWA_Q_WIDTH = SWA_HEADS * HEAD_DIM
SWA_KV_WIDTH = SWA_KV_HEADS * HEAD_DIM
WINDOW = 128
BLOCK = 128
MLA_HEADS = 6
MLA_NOPE_DIM = 64
MLA_ROPE_DIM = 32
MLA_V_DIM = 64
MLA_Q_RANK = 256
MLA_KV_RANK = 128
MLA_SCALE = (MLA_NOPE_DIM + MLA_ROPE_DIM) ** -0.5
SWA_SCALE = HEAD_DIM ** -0.5
D_MIX = FOURIER_WIDTH + SWA_Q_WIDTH + MLA_HEADS * MLA_V_DIM
OFF_SWA_Q = FOURIER_WIDTH
OFF_SWA_K = OFF_SWA_Q + SWA_Q_WIDTH
OFF_SWA_V = OFF_SWA_K + SWA_KV_WIDTH
OFF_MLA_CQ = OFF_SWA_V + SWA_KV_WIDTH
OFF_MLA_CKV = OFF_MLA_CQ + MLA_Q_RANK
OFF_MLA_KR = OFF_MLA_CKV + MLA_KV_RANK
D_FF = 4 * D_MODEL
ROPE_THETA = 10000.0
NORM_EPS = 1e-6
NEG_INF = -1e30

LANE = 128
MLA_PAD = LANE

C_F = 0
C_K = C_F + FOURIER_WIDTH
C_CQ = C_K + SWA_KV_WIDTH
C_CKV = C_CQ + MLA_Q_RANK
C_KR = C_CKV + MLA_KV_RANK
C_KROT = C_KR + LANE
C_KRROT = C_KROT + SWA_KV_WIDTH
W1_COLS = C_KRROT + LANE
R_Q = 0
R_V = R_Q + SWA_Q_WIDTH
W1T_ROWS = R_V + SWA_KV_WIDTH

BF16 = jnp.bfloat16
F32 = jnp.float32
LOG2E = math.log2(math.e)
_NT = (((1,), (1,)), ((), ()))
MLA_TK = 256
MLA_VT_ROWS = MLA_V_DIM + 16
MLA_S_SLOTS = 2
SWA_VT_ROWS = HEAD_DIM + 16
SWA_TQ = 256

VMEM_LIMIT = 56 * 1024 * 1024


def _cparams(sem):
    return pltpu.CompilerParams(dimension_semantics=sem, vmem_limit_bytes=VMEM_LIMIT)


def _rms(x, g):
    ms = jnp.mean(x * x, axis=-1, keepdims=True)
    return x * lax.rsqrt(ms + NORM_EPS) * g


def _ada_kernel(cc_ref, w_ref, b_ref, o_ref):
    cc = cc_ref[...]
    s = cc * jax.nn.sigmoid(cc)
    o_ref[0] = jnp.dot(s, w_ref[0], preferred_element_type=F32,
                       precision=lax.Precision.HIGHEST) + b_ref[0]


def _ada_call(cc, w_ada, b_ada):
    depth = w_ada.shape[0]
    rows = cc.shape[0]
    tn = 1536
    n_out = w_ada.shape[2]
    return pl.pallas_call(
        _ada_kernel,
        grid=(depth, n_out // tn),
        in_specs=[
            pl.BlockSpec((rows, D_MODEL), lambda l, j: (0, 0)),
            pl.BlockSpec((1, D_MODEL, tn), lambda l, j: (l, 0, j)),
            pl.BlockSpec((1, 1, tn), lambda l, j: (l, 0, j)),
        ],
        out_specs=pl.BlockSpec((1, rows, tn), lambda l, j: (l, 0, j)),
        out_shape=jax.ShapeDtypeStruct((depth, rows, n_out), F32),
        compiler_params=_cparams(("arbitrary", "arbitrary")),
        name="ada_mod",
    )(cc, w_ada, b_ada.reshape(depth, 1, n_out))


def _fw_kernel(c_ref, s_ref, w_ref, o_ref):
    o_ref[...] = jnp.zeros(o_ref.shape, o_ref.dtype)
    inv = 1.0 / math.sqrt(FOURIER_GROUP_DIM)
    for g in range(FOURIER_GROUPS):
        w = w_ref[g]
        a = jnp.dot(c_ref[...], w, preferred_element_type=F32, precision=lax.Precision.HIGHEST) * inv
        b = jnp.dot(s_ref[...], w, preferred_element_type=F32, precision=lax.Precision.HIGHEST) * inv
        lo = g * FOURIER_GROUP_DIM
        hi = lo + FOURIER_GROUP_DIM
        o_ref[lo:hi, lo:hi] = a
        o_ref[lo:hi, FOURIER_WIDTH + lo:FOURIER_WIDTH + hi] = b


def _fw_call(c64, s64, w_f):
    return pl.pallas_call(
        _fw_kernel,
        out_shape=jax.ShapeDtypeStruct((FOURIER_WIDTH, 2 * FOURIER_WIDTH), F32),
        name="fourier_w",
    )(c64, s64, w_f)


def _ones_row(rows, cols):
    return (lax.broadcasted_iota(jnp.int32, (rows, cols), 0) == 0).astype(BF16)


def _proj_kernel(h_ref, mod_ref, g1_ref, w1_ref, w1t_ref, rope_ref, ropet_ref, wcs_ref, qn_ref, wuq_ref,
                 kvn_ref, wukvk_ref, wukvv_ref,
                 u_o, v_o, sq_o, sk_o, sv_o, mq_o, mk_o, mv_o):
    x = h_ref[0]
    tm = x.shape[0]
    mod = mod_ref[0]
    sh = mod[:, 0:D_MODEL]
    sc = mod[:, D_MODEL:2 * D_MODEL]
    y = (_rms(x, g1_ref[...]) * (1.0 + sc) + sh).astype(BF16)
    u = jnp.dot(y, w1_ref[...], preferred_element_type=F32)
    ut = lax.dot_general(w1t_ref[...], y, _NT, preferred_element_type=F32)

    cos_h = rope_ref[:, 0:LANE]
    sin_h = rope_ref[:, LANE:2 * LANE]
    cos_m = rope_ref[:, 2 * LANE:3 * LANE]
    sin_m = rope_ref[:, 3 * LANE:4 * LANE]
    cos_mt = ropet_ref[0:LANE, :]
    sin_mt = ropet_ref[LANE:2 * LANE, :]
    cos_ht = ropet_ref[2 * LANE:3 * LANE, :]
    sin_ht = ropet_ref[3 * LANE:4 * LANE, :]

    f = u[:, C_F:C_F + FOURIER_WIDTH].astype(BF16)
    uv = jnp.dot(f, wcs_ref[...], preferred_element_type=F32)
    u_o[0] = uv[:, :FOURIER_WIDTH]
    v_o[0] = uv[:, FOURIER_WIDTH:]

    def rope_rows(x1, x2, c, s, scale):
        return ((x1 * c - x2 * s) * scale).astype(BF16), ((x2 * c + x1 * s) * scale).astype(BF16)

    half = HEAD_DIM // 2
    c_h, s_h = cos_ht[0:half], sin_ht[0:half]
    for hh in range(SWA_HEADS):
        r0 = R_Q + HEAD_DIM * hh
        lo, hi = rope_rows(ut[r0:r0 + half], ut[r0 + half:r0 + HEAD_DIM], c_h, s_h, SWA_SCALE * LOG2E)
        sq_o[0, HEAD_DIM * hh:HEAD_DIM * hh + half, :] = lo
        sq_o[0, HEAD_DIM * hh + half:HEAD_DIM * (hh + 1), :] = hi
    kk = (u[:, C_K:C_K + LANE] * cos_h + u[:, C_KROT:C_KROT + LANE] * sin_h).astype(BF16)
    ones_swa = _ones_row(SWA_VT_ROWS - HEAD_DIM, tm)
    for kh in range(SWA_KV_HEADS):
        sk_o[0, kh] = kk[:, HEAD_DIM * kh:HEAD_DIM * (kh + 1)]
        sv_o[0, kh, 0:HEAD_DIM, :] = ut[R_V + HEAD_DIM * kh:R_V + HEAD_DIM * (kh + 1)].astype(BF16)
        sv_o[0, kh, HEAD_DIM:SWA_VT_ROWS, :] = ones_swa

    cqn = _rms(u[:, C_CQ:C_CQ + MLA_Q_RANK], qn_ref[...]).astype(BF16)
    qa = lax.dot_general(wuq_ref[...], cqn, _NT, preferred_element_type=F32)
    rh = MLA_ROPE_DIM // 2
    c_m, s_m = cos_mt[MLA_NOPE_DIM:MLA_NOPE_DIM + rh], sin_mt[MLA_NOPE_DIM:MLA_NOPE_DIM + rh]
    q_scale = MLA_SCALE * LOG2E
    for hh in range(MLA_HEADS):
        r0 = MLA_PAD * hh
        r1 = r0 + MLA_NOPE_DIM
        lo, hi = rope_rows(qa[r1:r1 + rh], qa[r1 + rh:r1 + 2 * rh], c_m, s_m, q_scale)
        mq_o[0, r0:r1, :] = (qa[r0:r1] * q_scale).astype(BF16)
        mq_o[0, r1:r1 + rh, :] = lo
        mq_o[0, r1 + rh:r1 + 2 * rh, :] = hi
        mq_o[0, r1 + 2 * rh:r0 + MLA_PAD, :] = jnp.zeros((MLA_PAD - MLA_NOPE_DIM - 2 * rh, tm), BF16)

    ckvn = _rms(u[:, C_CKV:C_CKV + MLA_KV_RANK], kvn_ref[...]).astype(BF16)
    kr = u[:, C_KR:C_KR + LANE] * cos_m + u[:, C_KRROT:C_KRROT + LANE] * sin_m
    kvk = jnp.dot(ckvn, wukvk_ref[...], preferred_element_type=F32)
    vt = lax.dot_general(wukvv_ref[...], ckvn, _NT, preferred_element_type=F32).astype(BF16)
    chunk = mv_o.shape[4]
    ones_mla = _ones_row(MLA_VT_ROWS - MLA_V_DIM, chunk)
    for hh in range(MLA_HEADS):
        mk_o[0, hh] = (kvk[:, MLA_PAD * hh:MLA_PAD * (hh + 1)] + kr).astype(BF16)
        for c in range(tm // chunk):
            mv_o[0, hh, c, 0:MLA_V_DIM, :] = vt[MLA_V_DIM * hh:MLA_V_DIM * (hh + 1), chunk * c:chunk * (c + 1)]
            mv_o[0, hh, c, MLA_V_DIM:MLA_VT_ROWS, :] = ones_mla


def _proj_call(h, mod3, mod_row, g1, w1, w1t, rope, ropet, wcs, qn, wuq, kvn, wukvk, wukvv, tm):
    b_, s_, _ = h.shape
    chunk = min(MLA_TK, tm)
    const = lambda b, i: (0, 0)
    if mod_row is None:
        mod_map = lambda b, i: (b, 0, 0)
    else:
        mod_map = lambda b, i: (mod_row, 0, 0)
    out_shapes = (
        jax.ShapeDtypeStruct((b_, s_, FOURIER_WIDTH), F32),
        jax.ShapeDtypeStruct((b_, s_, FOURIER_WIDTH), F32),
        jax.ShapeDtypeStruct((b_, SWA_Q_WIDTH, s_), BF16),
        jax.ShapeDtypeStruct((b_, SWA_KV_HEADS, s_, HEAD_DIM), BF16),
        jax.ShapeDtypeStruct((b_, SWA_KV_HEADS, SWA_VT_ROWS, s_), BF16),
        jax.ShapeDtypeStruct((b_, MLA_HEADS * MLA_PAD, s_), BF16),
        jax.ShapeDtypeStruct((b_, MLA_HEADS, s_, MLA_PAD), BF16),
        jax.ShapeDtypeStruct((b_, MLA_HEADS, s_ // chunk, MLA_VT_ROWS, chunk), BF16),
    )
    out_specs = (
        pl.BlockSpec((1, tm, FOURIER_WIDTH), lambda b, i: (b, i, 0)),
        pl.BlockSpec((1, tm, FOURIER_WIDTH), lambda b, i: (b, i, 0)),
        pl.BlockSpec((1, SWA_Q_WIDTH, tm), lambda b, i: (b, 0, i)),
        pl.BlockSpec((1, SWA_KV_HEADS, tm, HEAD_DIM), lambda b, i: (b, 0, i, 0)),
        pl.BlockSpec((1, SWA_KV_HEADS, SWA_VT_ROWS, tm), lambda b, i: (b, 0, 0, i)),
        pl.BlockSpec((1, MLA_HEADS * MLA_PAD, tm), lambda b, i: (b, 0, i)),
        pl.BlockSpec((1, MLA_HEADS, tm, MLA_PAD), lambda b, i: (b, 0, i, 0)),
        pl.BlockSpec((1, MLA_HEADS, tm // chunk, MLA_VT_ROWS, chunk), lambda b, i: (b, 0, i, 0, 0)),
    )
    return pl.pallas_call(
        _proj_kernel,
        grid=(b_, s_ // tm),
        in_specs=[
            pl.BlockSpec((1, tm, D_MODEL), lambda b, i: (b, i, 0)),
            pl.BlockSpec((1, 1, 6 * D_MODEL), mod_map),
            pl.BlockSpec((1, D_MODEL), const),
            pl.BlockSpec((D_MODEL, W1_COLS), const),
            pl.BlockSpec((W1T_ROWS, D_MODEL), const),
            pl.BlockSpec((tm, 4 * LANE), lambda b, i: (i, 0)),
            pl.BlockSpec((4 * LANE, tm), lambda b, i: (0, i)),
            pl.BlockSpec((FOURIER_WIDTH, 2 * FOURIER_WIDTH), const),
            pl.BlockSpec((1, MLA_Q_RANK), const),
            pl.BlockSpec((MLA_HEADS * MLA_PAD, MLA_Q_RANK), const),
            pl.BlockSpec((1, MLA_KV_RANK), const),
            pl.BlockSpec((MLA_KV_RANK, MLA_HEADS * MLA_PAD), const),
            pl.BlockSpec((MLA_HEADS * MLA_V_DIM, MLA_KV_RANK), const),
        ],
        out_specs=out_specs,
        out_shape=out_shapes,
        compiler_params=_cparams(("parallel", "parallel")),
        name="norm_in_proj",
    )(h, mod3, g1, w1, w1t, rope, ropet, wcs, qn, wuq, kvn, wukvk, wukvv)


def _dft_kernel(c_ref, s_ref, u_ref, v_ref, o_ref):
    o = jnp.dot(c_ref[...], u_ref[0].astype(BF16), preferred_element_type=F32)
    o = o + jnp.dot(s_ref[...], v_ref[0].astype(BF16), preferred_element_type=F32)
    o_ref[0] = o


def _dft_call(ctab, stab, u, v, tk):
    b_, s_, _ = u.shape
    whole = pl.BlockSpec((1, s_, FOURIER_WIDTH), lambda i, b: (b, 0, 0))
    return pl.pallas_call(
        _dft_kernel,
        grid=(s_ // tk, b_),
        in_specs=[
            pl.BlockSpec((tk, s_), lambda i, b: (i, 0)),
            pl.BlockSpec((tk, s_), lambda i, b: (i, 0)),
            whole, whole,
        ],
        out_specs=pl.BlockSpec((1, tk, FOURIER_WIDTH), lambda i, b: (b, i, 0)),
        out_shape=jax.ShapeDtypeStruct((b_, s_, FOURIER_WIDTH), F32),
        compiler_params=_cparams(("parallel", "arbitrary")),
        name="fourier_dft",
    )(ctab, stab, u, v)


FFT_SUB = 8


def _fft1_kernel(m1_ref, m2_ref, tc_ref, ts_ref, u_ref, v_ref, bre_ref, bim_ref):
    n1, sub, w = u_ref.shape[1:]
    xu = u_ref[0].reshape(n1 * sub, w).astype(BF16)
    xv = v_ref[0].reshape(n1 * sub, w).astype(BF16)
    a = jnp.dot(m1_ref[...], xu, preferred_element_type=F32)
    a = a + jnp.dot(m2_ref[...], xv, preferred_element_type=F32)
    are, aim = a[0:n1 * sub], a[n1 * sub:]
    tc, ts = tc_ref[0], ts_ref[0]
    bre_ref[0] = (are * tc + aim * ts).reshape(n1, sub, w)
    bim_ref[0] = (aim * tc - are * ts).reshape(n1, sub, w)


def _fft2_kernel(g1_ref, g2_ref, bre_ref, bim_ref, o_ref):
    n2 = g1_ref.shape[1]
    for j in range(FFT_SUB):
        rows = slice(n2 * j, n2 * (j + 1))
        x = jnp.dot(g1_ref[...], bre_ref[0, rows, :].astype(BF16), preferred_element_type=F32)
        x = x + jnp.dot(g2_ref[...], bim_ref[0, rows, :].astype(BF16), preferred_element_type=F32)
        o_ref[0, :, j, :] = x


def _fft_call(tabs, u, v):
    m1, m2, tc, ts, g1, g2 = tabs
    b_, s_, w = u.shape
    n1 = m1.shape[1] // FFT_SUB
    n2 = s_ // n1
    u4 = u.reshape(b_, n1, n2, w)
    v4 = v.reshape(b_, n1, n2, w)
    tile = pl.BlockSpec((1, n1, FFT_SUB, w), lambda t, b: (b, 0, t, 0))
    tab = pl.BlockSpec((1, n1 * FFT_SUB, w), lambda t, b: (t, 0, 0))
    mat = pl.BlockSpec((2 * n1 * FFT_SUB, n1 * FFT_SUB), lambda t, b: (0, 0))
    bre, bim = pl.pallas_call(
        _fft1_kernel,
        grid=(n2 // FFT_SUB, b_),
        in_specs=[mat, mat, tab, tab, tile, tile],
        out_specs=(tile, tile),
        out_shape=(jax.ShapeDtypeStruct((b_, n1, n2, w), F32),) * 2,
        compiler_params=_cparams(("parallel", "arbitrary")),
        name="fourier_fft1",
    )(m1, m2, tc, ts, u4, v4)
    blk = pl.BlockSpec((1, FFT_SUB * n2, w), lambda b, t: (b, t, 0))
    mat2 = pl.BlockSpec((n2, n2), lambda b, t: (0, 0))
    out = pl.pallas_call(
        _fft2_kernel,
        grid=(b_, n1 // FFT_SUB),
        in_specs=[mat2, mat2, blk, blk],
        out_specs=pl.BlockSpec((1, n2, FFT_SUB, w), lambda b, t: (b, 0, t, 0)),
        out_shape=jax.ShapeDtypeStruct((b_, n2, n1, w), F32),
        compiler_params=_cparams(("parallel", "parallel")),
        name="fourier_fft2",
    )(g1, g2, bre.reshape(b_, s_, w), bim.reshape(b_, s_, w))
    return out.reshape(b_, s_, w)


def _fft_tables(s_, n1, w):
    n2 = s_ // n1
    k1 = jnp.arange(n1, dtype=jnp.int32)
    a1 = ((k1[:, None] * k1[None, :]) % n1).astype(F32) * (2.0 * math.pi / n1)
    c1, s1 = jnp.cos(a1), jnp.sin(a1)
    eye = jnp.eye(FFT_SUB, dtype=F32)
    m1 = jnp.kron(jnp.concatenate([c1, -s1], axis=0), eye).astype(BF16)
    m2 = jnp.kron(jnp.concatenate([-s1, -c1], axis=0), eye).astype(BF16)
    s2 = jnp.arange(n2, dtype=jnp.int32)
    th = ((k1[:, None] * s2[None, :]) % s_).astype(F32) * (2.0 * math.pi / s_)

    def bcast(t):
        t = t.reshape(n1, n2 // FFT_SUB, FFT_SUB).transpose(1, 0, 2).reshape(n2 // FFT_SUB, n1 * FFT_SUB, 1)
        return jnp.broadcast_to(t, (n2 // FFT_SUB, n1 * FFT_SUB, w))

    tc = bcast(jnp.cos(th))
    ts = bcast(jnp.sin(th))
    a2 = ((s2[:, None] * s2[None, :]) % n2).astype(F32) * (2.0 * math.pi / n2)
    scale = 1.0 / math.sqrt(s_)
    g1 = (jnp.cos(a2) * scale).astype(BF16)
    g2 = (jnp.sin(a2) * scale).astype(BF16)
    return m1, m2, tc, ts, g1, g2


def _mla_kernel(*refs, n_chunks, heads_per_step):
    hps = heads_per_step
    n_in = 6 if n_chunks else 4
    if n_chunks:
        zero_ref, qt_ref, k_ref, vt_ref, kc_ref, vct_ref = refs[:n_in]
    else:
        zero_ref, qt_ref, kc_ref, vct_ref = refs[:n_in]
    o_ref = refs[n_in]
    row0 = pl.multiple_of(zero_ref[0], MLA_TK)
    scratch = refs[n_in + 1:]
    s_refs = [scratch[MLA_S_SLOTS * hh:MLA_S_SLOTS * (hh + 1)] for hh in range(hps)]
    tq = qt_ref.shape[2]
    lc = kc_ref.shape[2]
    qts = [qt_ref[0, MLA_PAD * hh:MLA_PAD * (hh + 1), :] for hh in range(hps)]

    def logits(ks, idx, rows):
        cms = []
        for hh in range(hps):
            s = jnp.dot(ks(hh), qts[hh], preferred_element_type=F32)
            s_refs[hh][idx % MLA_S_SLOTS][0:rows, :] = s
            cms.append(jnp.max(s, axis=0, keepdims=True))
        return tuple(cms)

    def accumulate(idx, rows, vts, carries, cms):
        out = []
        for hh in range(hps):
            m, acc = carries[hh]
            m_new = jnp.maximum(m, cms[hh])
            alpha = jnp.exp2(m - m_new)
            s = s_refs[hh][idx % MLA_S_SLOTS][pl.ds(row0, rows), :]
            p = jnp.exp2(s - m_new).astype(BF16)
            acc = alpha * acc + jnp.dot(vts(hh), p, preferred_element_type=F32)
            out.append((m_new, acc))
        return tuple(out)

    def k_lat(c):
        return lambda hh: k_ref[0, hh, c * MLA_TK:(c + 1) * MLA_TK, :]

    def v_lat(c):
        return lambda hh: vt_ref[0, hh, c]

    k_ctx = lambda hh: kc_ref[0, hh]
    v_ctx = lambda hh: vct_ref[0, hh, 0]

    init = (jnp.full((1, tq), NEG_INF, F32), jnp.zeros((MLA_VT_ROWS, tq), F32))
    carries = tuple(init for _ in range(hps))
    cms = logits(k_ctx, 0, lc)
    if n_chunks:
        prev_cms, prev_rows, prev_v = cms, lc, v_ctx
        for c in range(n_chunks):
            new_cms = logits(k_lat(c), c + 1, MLA_TK)
            carries = accumulate(c, prev_rows, prev_v, carries, prev_cms)
            prev_cms, prev_rows, prev_v = new_cms, MLA_TK, v_lat(c)
        carries = accumulate(n_chunks, MLA_TK, prev_v, carries, prev_cms)
    else:
        carries = accumulate(0, lc, v_ctx, carries, cms)
    o = jnp.concatenate([acc[0:MLA_V_DIM] / acc[MLA_V_DIM:MLA_V_DIM + 1] for (_, acc) in carries],
                        axis=0)
    o_ref[0] = o.T.astype(o_ref.dtype)


def _mla_call(qt, k, vt, kc, vct, tq):
    b_, _, sq = qt.shape
    h_ = kc.shape[1]
    lc = kc.shape[2]
    hps = 2
    n_chunks = 0 if k is None else k.shape[2] // MLA_TK
    assert n_chunks == 0 or lc <= MLA_TK
    in_specs = [pl.BlockSpec(memory_space=pltpu.SMEM),
                pl.BlockSpec((1, hps * MLA_PAD, tq), lambda b, p, i: (b, p, i))]
    args = [jnp.zeros((1,), jnp.int32), qt]
    if k is not None:
        sk = k.shape[2]
        in_specs += [pl.BlockSpec((1, hps, sk, MLA_PAD), lambda b, p, i: (b, p, 0, 0)),
                     pl.BlockSpec((1, hps, n_chunks, MLA_VT_ROWS, MLA_TK), lambda b, p, i: (b, p, 0, 0, 0))]
        args += [k, vt]
    in_specs += [pl.BlockSpec((1, hps, lc, MLA_PAD), lambda b, p, i: (b, p, 0, 0)),
                 pl.BlockSpec((1, hps, 1, MLA_VT_ROWS, lc), lambda b, p, i: (b, p, 0, 0, 0))]
    args += [kc, vct]
    return pl.pallas_call(
        functools.partial(_mla_kernel, n_chunks=n_chunks, heads_per_step=hps),
        grid=(b_, h_ // hps, sq // tq),
        in_specs=in_specs,
        out_specs=pl.BlockSpec((1, tq, hps * MLA_V_DIM), lambda b, p, i: (b, i, p)),
        out_shape=jax.ShapeDtypeStruct((b_, sq, h_ * MLA_V_DIM), BF16),
        scratch_shapes=[pltpu.VMEM((MLA_TK if n_chunks else lc, tq), F32)] * (MLA_S_SLOTS * hps),
        compiler_params=_cparams(("parallel", "parallel", "arbitrary")),
        name="mla_attn",
    )(*args)


def _swa_kernel(*refs, n_local, s_len):
    zero_ref, sink_ref, qt_ref = refs[:3]
    k_refs = refs[3:3 + n_local]
    vt_refs = refs[3 + n_local:3 + 2 * n_local]
    kc_ref, vct_ref, o_ref = refs[3 + 2 * n_local:6 + 2 * n_local]
    s_refs = refs[6 + 2 * n_local:]
    row0 = pl.multiple_of(zero_ref[0], BLOCK)
    tq = qt_ref.shape[2]
    lc = kc_ref.shape[2]
    nk = n_local * BLOCK
    if n_local:
        n = pl.program_id(1)
        kpos = n * tq - BLOCK + lax.broadcasted_iota(jnp.int32, (nk, tq), 0)
        qpos = n * tq + lax.broadcasted_iota(jnp.int32, (nk, tq), 1)
        valid = (jnp.abs(kpos - qpos) <= WINDOW) & (kpos >= 0) & (kpos < s_len)
        mask_bias = jnp.where(valid, 0.0, NEG_INF).astype(F32)
    sinks = [sink_ref[hh] * LOG2E for hh in range(SWA_HEADS)]

    def keys(kh):
        kc = kc_ref[0, kh]
        vct = vct_ref[0, kh]
        if not n_local:
            return kc, vct
        return (jnp.concatenate([kc] + [r[0, kh] for r in k_refs], axis=0),
                jnp.concatenate([vct] + [r[0, kh] for r in vt_refs], axis=1))

    def logits(hh, k_all):
        qt = qt_ref[0, HEAD_DIM * hh:HEAD_DIM * (hh + 1), :]
        s_ref = s_refs[hh % 2]
        s_c = jnp.dot(k_all[0:lc], qt, preferred_element_type=F32)
        s_ref[0:lc, :] = s_c
        m = jnp.maximum(jnp.max(s_c, axis=0, keepdims=True), sinks[hh])
        if n_local:
            s_l = jnp.dot(k_all[lc:], qt, preferred_element_type=F32) + mask_bias
            s_ref[lc:lc + nk, :] = s_l
            m = jnp.maximum(m, jnp.max(s_l, axis=0, keepdims=True))
        return m

    def accumulate(hh, vt_all, m):
        p = jnp.exp2(s_refs[hh % 2][pl.ds(row0, lc + nk), :] - m).astype(BF16)
        acc = jnp.dot(vt_all, p, preferred_element_type=F32)
        l = acc[HEAD_DIM:HEAD_DIM + 1] + jnp.exp2(sinks[hh] - m)
        return acc[0:HEAD_DIM] / l

    kv = [keys(kh) for kh in range(SWA_KV_HEADS)]
    outs = []
    m_prev = logits(0, kv[0][0])
    for hh in range(1, SWA_HEADS):
        m_next = logits(hh, kv[hh // SWA_GROUP][0])
        outs.append(accumulate(hh - 1, kv[(hh - 1) // SWA_GROUP][1], m_prev))
        m_prev = m_next
    outs.append(accumulate(SWA_HEADS - 1, kv[SWA_KV_HEADS - 1][1], m_prev))
    o = jnp.concatenate(outs, axis=0)
    o_ref[0] = o.T.astype(o_ref.dtype)


def _swa_call(sink, qt, k, vt, kc, vct):
    b_, _, sq = qt.shape
    lc = kc.shape[2]
    local = k is not None
    tq = SWA_TQ if local else sq
    nb = sq // tq
    n_local = tq // BLOCK + 2 if local else 0
    in_specs = [pl.BlockSpec(memory_space=pltpu.SMEM), pl.BlockSpec(memory_space=pltpu.SMEM),
                pl.BlockSpec((1, SWA_Q_WIDTH, tq), lambda b, n: (b, 0, n))]
    args = [jnp.zeros((1,), jnp.int32), sink, qt]
    if local:
        nkb = sq // BLOCK
        per = tq // BLOCK

        def blk_idx(n, j):
            return jnp.clip(n * per - 1 + j, 0, nkb - 1)

        in_specs += [pl.BlockSpec((1, SWA_KV_HEADS, BLOCK, HEAD_DIM),
                                  functools.partial(lambda b, n, j: (b, 0, blk_idx(n, j), 0), j=j))
                     for j in range(n_local)]
        in_specs += [pl.BlockSpec((1, SWA_KV_HEADS, SWA_VT_ROWS, BLOCK),
                                  functools.partial(lambda b, n, j: (b, 0, 0, blk_idx(n, j)), j=j))
                     for j in range(n_local)]
        args += [k] * n_local + [vt] * n_local
    in_specs += [pl.BlockSpec((1, SWA_KV_HEADS, lc, HEAD_DIM), lambda b, n: (b, 0, 0, 0)),
                 pl.BlockSpec((1, SWA_KV_HEADS, SWA_VT_ROWS, lc), lambda b, n: (b, 0, 0, 0))]
    args += [kc, vct]
    return pl.pallas_call(
        functools.partial(_swa_kernel, n_local=n_local, s_len=sq),
        grid=(b_, nb),
        in_specs=in_specs,
        out_specs=pl.BlockSpec((1, tq, SWA_Q_WIDTH), lambda b, n: (b, n, 0)),
        out_shape=jax.ShapeDtypeStruct((b_, sq, SWA_Q_WIDTH), BF16),
        scratch_shapes=[pltpu.VMEM((lc + n_local * BLOCK, tq), F32)] * 2,
        compiler_params=_cparams(("parallel", "arbitrary")),
        name="swa_attn",
    )(*args)


def _mlp_kernel(fo_ref, so_ref, mo_ref, h_ref, mod_ref, g2_ref, wo_ref, w1_ref, w2_ref, gf_ref, o_ref, *,
                final, ff_chunk):
    mod = mod_ref[0]
    g1 = mod[:, 2 * D_MODEL:3 * D_MODEL]
    sh2 = mod[:, 3 * D_MODEL:4 * D_MODEL]
    sc2 = mod[:, 4 * D_MODEL:5 * D_MODEL]
    g2 = mod[:, 5 * D_MODEL:6 * D_MODEL]
    mix = jnp.concatenate([fo_ref[0].astype(BF16), so_ref[0], mo_ref[0]], axis=-1)
    h = h_ref[0] + g1 * jnp.dot(mix, wo_ref[...], preferred_element_type=F32)
    y = (_rms(h, g2_ref[...]) * (1.0 + sc2) + sh2).astype(BF16)
    acc = None
    for c in range(D_FF // ff_chunk):
        a = jnp.dot(y, w1_ref[:, c * ff_chunk:(c + 1) * ff_chunk], preferred_element_type=F32)
        a = jnp.maximum(a, 0.0)
        a = (a * a).astype(BF16)
        part = jnp.dot(a, w2_ref[c * ff_chunk:(c + 1) * ff_chunk, :], preferred_element_type=F32)
        acc = part if acc is None else acc + part
    h = h + g2 * acc
    if final:
        h = _rms(h, gf_ref[...])
    o_ref[0] = h


def _mlp_call(fo, so, mo, h, mod3, mod_row, g2, wo, w1, w2, gf, tm, final):
    b_, s_, _ = h.shape
    const = lambda b, i: (0, 0)
    tile = lambda b, i: (b, i, 0)
    if mod_row is None:
        mod_map = lambda b, i: (b, 0, 0)
    else:
        mod_map = lambda b, i: (mod_row, 0, 0)
    once = pl.Buffered(1)
    return pl.pallas_call(
        functools.partial(_mlp_kernel, final=final, ff_chunk=1024),
        grid=(b_, s_ // tm),
        in_specs=[
            pl.BlockSpec((1, tm, FOURIER_WIDTH), tile),
            pl.BlockSpec((1, tm, SWA_Q_WIDTH), tile),
            pl.BlockSpec((1, tm, MLA_HEADS * MLA_V_DIM), tile),
            pl.BlockSpec((1, tm, D_MODEL), tile),
            pl.BlockSpec((1, 1, 6 * D_MODEL), mod_map),
            pl.BlockSpec((1, D_MODEL), const),
            pl.BlockSpec((D_MIX, D_MODEL), const, pipeline_mode=once),
            pl.BlockSpec((D_MODEL, D_FF), const, pipeline_mode=once),
            pl.BlockSpec((D_FF, D_MODEL), const, pipeline_mode=once),
            pl.BlockSpec((1, D_MODEL), const),
        ],
        out_specs=pl.BlockSpec((1, tm, D_MODEL), tile),
        out_shape=jax.ShapeDtypeStruct((b_, s_, D_MODEL), F32),
        compiler_params=_cparams(("parallel", "parallel")),
        name="out_proj_mlp",
    )(fo, so, mo, h, mod3, g2, wo, w1, w2, gf)


def _rot_cols(w, heads, dim):
    w4 = w.reshape(w.shape[0], heads, 2, dim // 2)
    return jnp.stack([-w4[:, :, 1], w4[:, :, 0]], axis=2).reshape(w.shape[0], heads * dim)


def _pad_lanes(w, heads, before, after):
    w3 = w.reshape(w.shape[0], heads, w.shape[1] // heads)
    w3 = jnp.pad(w3, ((0, 0), (0, 0), (before, after)))
    return w3.reshape(w.shape[0], -1)


def _pack_w_in(w_in):
    wk = w_in[:, OFF_SWA_K:OFF_SWA_V]
    wkr = w_in[:, OFF_MLA_KR:OFF_MLA_KR + MLA_ROPE_DIM]
    pad_kr = lambda w: _pad_lanes(w, 1, MLA_NOPE_DIM, LANE - MLA_NOPE_DIM - MLA_ROPE_DIM)
    cols = [w_in[:, :OFF_SWA_Q], wk, w_in[:, OFF_MLA_CQ:OFF_MLA_KR], pad_kr(wkr),
            _rot_cols(wk, SWA_KV_HEADS, HEAD_DIM), pad_kr(_rot_cols(wkr, 1, MLA_ROPE_DIM))]
    rows = w_in[:, OFF_SWA_Q:OFF_SWA_K], w_in[:, OFF_SWA_V:OFF_MLA_CQ]
    return jnp.concatenate(cols, axis=1).astype(BF16), jnp.concatenate(rows, axis=1).T.astype(BF16)


def _pack_w_uq(w_uq):
    per = MLA_NOPE_DIM + MLA_ROPE_DIM
    return _pad_lanes(w_uq, MLA_HEADS, 0, MLA_PAD - per).T.astype(BF16)


def _pack_w_ukv(w_ukv):
    w3 = w_ukv.reshape(MLA_KV_RANK, MLA_HEADS, MLA_NOPE_DIM + MLA_V_DIM)
    wk = w3[:, :, :MLA_NOPE_DIM].reshape(MLA_KV_RANK, MLA_HEADS * MLA_NOPE_DIM)
    wv = w3[:, :, MLA_NOPE_DIM:].reshape(MLA_KV_RANK, MLA_HEADS * MLA_V_DIM)
    return _pad_lanes(wk, MLA_HEADS, 0, MLA_PAD - MLA_NOPE_DIM).astype(BF16), wv.T.astype(BF16)


def _rope_table(rows):
    r, col = jnp.meshgrid(jnp.arange(rows, dtype=F32), jnp.arange(GRID_W, dtype=F32), indexing="ij")
    r = r.reshape(-1)
    col = col.reshape(-1)

    def tables(dim):
        n_freq = dim // 4
        inv = ROPE_THETA ** (-jnp.arange(n_freq, dtype=F32) / n_freq)
        ang = jnp.concatenate([r[:, None] * inv[None, :], col[:, None] * inv[None, :]], axis=-1)
        return jnp.cos(ang), jnp.sin(ang)

    ch, sh = tables(HEAD_DIM)
    cr, sr = tables(MLA_ROPE_DIM)
    mla_pad = ((0, 0), (MLA_NOPE_DIM, LANE - MLA_NOPE_DIM - MLA_ROPE_DIM))
    nope_ones = (jnp.arange(LANE) < MLA_NOPE_DIM).astype(F32)[None, :]
    return jnp.concatenate([jnp.tile(ch, (1, 4)), jnp.tile(sh, (1, 4)),
                            jnp.pad(jnp.tile(cr, (1, 2)), mla_pad) + nope_ones,
                            jnp.pad(jnp.tile(sr, (1, 2)), mla_pad)], axis=-1)


def _identity_rope_table(n):
    ones = jnp.ones((n, LANE), F32)
    zeros = jnp.zeros((n, LANE), F32)
    m = jnp.concatenate([jnp.ones((n, MLA_NOPE_DIM + MLA_ROPE_DIM), F32),
                         jnp.zeros((n, LANE - MLA_NOPE_DIM - MLA_ROPE_DIM), F32)], axis=-1)
    return jnp.concatenate([ones, zeros, m, zeros], axis=-1)


def _dft_tables(n):
    k = jnp.arange(n, dtype=jnp.int32)
    ks = (k[:, None] * k[None, :]) % n
    ang = ks.astype(F32) * (2.0 * math.pi / n)
    scale = 1.0 / math.sqrt(n)
    return (jnp.cos(ang) * scale).astype(BF16), (-jnp.sin(ang) * scale).astype(BF16)


def kernel(x, c, ctx, c_ctx, w_ada, b_ada, norm1_g, norm2_g, w_in, w_fourier, swa_sink, mla_q_norm, w_uq,
           mla_kv_norm, w_ukv, w_out, w_mlp1, w_mlp2, final_norm_g):
    b_, s_, _ = x.shape
    lc = ctx.shape[1]
    depth = w_ada.shape[0]
    rows = s_ // GRID_W

    mod_rows = 16
    cc = jnp.concatenate([c, c_ctx[None, :], jnp.zeros((mod_rows - b_ - 1, D_MODEL), F32)], axis=0)
    mod_all = _ada_call(cc, w_ada, b_ada)

    rope = _rope_table(rows)
    rope_c = _identity_rope_table(lc)
    to_feature_major = lambda t: jnp.concatenate([t[:, 2 * LANE:], t[:, :2 * LANE]], axis=1).T
    ropet = to_feature_major(rope)
    ropet_c = to_feature_major(rope_c)
    fft_tabs = _fft_tables(s_, GRID_W, FOURIER_WIDTH)
    ctab_c, stab_c = _dft_tables(lc)
    kk = jnp.arange(FOURIER_GROUP_DIM, dtype=jnp.int32)
    ang64 = ((kk[:, None] * kk[None, :]) % FOURIER_GROUP_DIM).astype(F32) * (2.0 * math.pi / FOURIER_GROUP_DIM)
    c64, s64 = jnp.cos(ang64), jnp.sin(ang64)

    h, hc = x, ctx
    for l in range(depth):
        last = l == depth - 1
        mod3 = mod_all[l].reshape(mod_rows, 1, 6 * D_MODEL)
        g1 = norm1_g[l].reshape(1, D_MODEL)
        g2 = norm2_g[l].reshape(1, D_MODEL)
        gf = final_norm_g.reshape(1, D_MODEL)
        w1, w1t = _pack_w_in(w_in[l])
        wuq = _pack_w_uq(w_uq[l])
        wukvk, wukvv = _pack_w_ukv(w_ukv[l])
        wcs = _fw_call(c64, s64, w_fourier[l]).astype(BF16)
        qn = mla_q_norm[l].reshape(1, MLA_Q_RANK)
        kvn = mla_kv_norm[l].reshape(1, MLA_KV_RANK)
        wo = w_out[l].astype(BF16)
        wm1 = w_mlp1[l].astype(BF16)
        wm2 = w_mlp2[l].astype(BF16)
        sink = swa_sink[l]

        fu, fv, sq, sk, sv, mq, mk, mv = _proj_call(h, mod3, None, g1, w1, w1t, rope, ropet, wcs, qn, wuq,
                                                    kvn, wukvk, wukvv, tm=512)
        fuc, fvc, sqc, skc, svc, mqc, mkc, mvc = _proj_call(hc, mod3, b_, g1, w1, w1t, rope_c, ropet_c, wcs, qn,
                                                            wuq, kvn, wukvk, wukvv, tm=lc)

        fo = _fft_call(fft_tabs, fu, fv)
        so = _swa_call(sink, sq, sk, sv, skc, svc)
        mo = _mla_call(mq, mk, mv, mkc, mvc, tq=512)
        h = _mlp_call(fo, so, mo, h, mod3, None, g2, wo, wm1, wm2, gf, tm=512, final=last)

        if not last:
            foc = _dft_call(ctab_c, stab_c, fuc, fvc, tk=lc)
            soc = _swa_call(sink, sqc, None, None, skc, svc)
            moc = _mla_call(mqc, None, None, mkc, mvc, tq=lc)
            hc = _mlp_call(foc, soc, moc, hc, mod3, b_, g2, wo, wm1, wm2, gf, tm=lc, final=False)
    return h
```

```python
import functools
import math

import jax
import jax.numpy as jnp
from jax import lax
from jax.experimental import pallas as pl
from jax.experimental.pallas import tpu as pltpu

D_MODEL = 1024
GRID_W = 64
HEAD_DIM = 64
FOURIER_GROUPS = 4
FOURIER_GROUP_DIM = 64
FOURIER_WIDTH = FOURIER_GROUPS * FOURIER_GROUP_DIM
SWA_HEADS = 6
SWA_KV_HEADS = 2
SWA_GROUP = SWA_HEADS // SWA_KV_HEADS
SWA_Q_WIDTH = SWA_HEADS * HEAD_DIM
SWA_KV_WIDTH = SWA_KV_HEADS * HEAD_DIM
WINDOW = 128
BLOCK = 128
MLA_HEADS = 6
MLA_NOPE_DIM = 64
MLA_ROPE_DIM = 32
MLA_V_DIM = 64
MLA_Q_RANK = 256
MLA_KV_RANK = 128
MLA_SCALE = (MLA_NOPE_DIM + MLA_ROPE_DIM) ** -0.5
SWA_SCALE = HEAD_DIM ** -0.5
D_MIX = FOURIER_WIDTH + SWA_Q_WIDTH + MLA_HEADS * MLA_V_DIM
OFF_SWA_Q = FOURIER_WIDTH
OFF_SWA_K = OFF_SWA_Q + SWA_Q_WIDTH
OFF_SWA_V = OFF_SWA_K + SWA_KV_WIDTH
OFF_MLA_CQ = OFF_SWA_V + SWA_KV_WIDTH
OFF_MLA_CKV = OFF_MLA_CQ + MLA_Q_RANK
OFF_MLA_KR = OFF_MLA_CKV + MLA_KV_RANK
D_FF = 4 * D_MODEL
ROPE_THETA = 10000.0
NORM_EPS = 1e-6
NEG_INF = -1e30

LANE = 128
MLA_PAD = LANE

C_F = 0
C_K = C_F + FOURIER_WIDTH
C_CQ = C_K + SWA_KV_WIDTH
C_CKV = C_CQ + MLA_Q_RANK
C_KR = C_CKV + MLA_KV_RANK
C_KROT = C_KR + LANE
C_KRROT = C_KROT + SWA_KV_WIDTH
W1_COLS = C_KRROT + LANE
R_Q = 0
R_V = R_Q + SWA_Q_WIDTH
W1T_ROWS = R_V + SWA_KV_WIDTH

BF16 = jnp.bfloat16
F32 = jnp.float32
LOG2E = math.log2(math.e)
_NT = (((1,), (1,)), ((), ()))
MLA_TK = 256
MLA_VT_ROWS = MLA_V_DIM + 16
MLA_S_SLOTS = 2
SWA_VT_ROWS = HEAD_DIM + 16
SWA_TQ = 256
SWA_SUB_TILES = 2

VMEM_LIMIT = 56 * 1024 * 1024


def _cparams(sem):
    return pltpu.CompilerParams(dimension_semantics=sem, vmem_limit_bytes=VMEM_LIMIT)


def _rms(x, g):
    ms = jnp.mean(x * x, axis=-1, keepdims=True)
    return x * lax.rsqrt(ms + NORM_EPS) * g


def _ada_kernel(cc_ref, w_ref, b_ref, o_ref):
    cc = cc_ref[...]
    s = cc * jax.nn.sigmoid(cc)
    o_ref[0] = jnp.dot(s, w_ref[0], preferred_element_type=F32,
                       precision=lax.Precision.HIGHEST) + b_ref[0]


def _ada_call(cc, w_ada, b_ada):
    depth = w_ada.shape[0]
    rows = cc.shape[0]
    tn = 1536
    n_out = w_ada.shape[2]
    return pl.pallas_call(
        _ada_kernel,
        grid=(depth, n_out // tn),
        in_specs=[
            pl.BlockSpec((rows, D_MODEL), lambda l, j: (0, 0)),
            pl.BlockSpec((1, D_MODEL, tn), lambda l, j: (l, 0, j)),
            pl.BlockSpec((1, 1, tn), lambda l, j: (l, 0, j)),
        ],
        out_specs=pl.BlockSpec((1, rows, tn), lambda l, j: (l, 0, j)),
        out_shape=jax.ShapeDtypeStruct((depth, rows, n_out), F32),
        compiler_params=_cparams(("arbitrary", "arbitrary")),
        name="ada_mod",
    )(cc, w_ada, b_ada.reshape(depth, 1, n_out))


def _fw_kernel(c_ref, s_ref, w_ref, o_ref):
    o_ref[...] = jnp.zeros(o_ref.shape, o_ref.dtype)
    inv = 1.0 / math.sqrt(FOURIER_GROUP_DIM)
    for g in range(FOURIER_GROUPS):
        w = w_ref[g]
        a = jnp.dot(c_ref[...], w, preferred_element_type=F32, precision=lax.Precision.HIGHEST) * inv
        b = jnp.dot(s_ref[...], w, preferred_element_type=F32, precision=lax.Precision.HIGHEST) * inv
        lo = g * FOURIER_GROUP_DIM
        hi = lo + FOURIER_GROUP_DIM
        o_ref[lo:hi, lo:hi] = a
        o_ref[lo:hi, FOURIER_WIDTH + lo:FOURIER_WIDTH + hi] = b


def _fw_call(c64, s64, w_f):
    return pl.pallas_call(
        _fw_kernel,
        out_shape=jax.ShapeDtypeStruct((FOURIER_WIDTH, 2 * FOURIER_WIDTH), F32),
        name="fourier_w",
    )(c64, s64, w_f)


def _ones_row(rows, cols):
    return (lax.broadcasted_iota(jnp.int32, (rows, cols), 0) == 0).astype(BF16)


def _proj_kernel(h_ref, mod_ref, g1_ref, w1_ref, w1t_ref, rope_ref, ropet_ref, wcs_ref, qn_ref, wuq_ref,
                 kvn_ref, wukvk_ref, wukvv_ref,
                 u_o, v_o, sq_o, sk_o, sv_o, mq_o, mk_o, mv_o):
    x = h_ref[0]
    tm = x.shape[0]
    mod = mod_ref[0]
    sh = mod[:, 0:D_MODEL]
    sc = mod[:, D_MODEL:2 * D_MODEL]
    y = (_rms(x, g1_ref[...]) * (1.0 + sc) + sh).astype(BF16)
    u = jnp.dot(y, w1_ref[...], preferred_element_type=F32)
    ut = lax.dot_general(w1t_ref[...], y, _NT, preferred_element_type=F32)

    cos_h = rope_ref[:, 0:LANE]
    sin_h = rope_ref[:, LANE:2 * LANE]
    cos_m = rope_ref[:, 2 * LANE:3 * LANE]
    sin_m = rope_ref[:, 3 * LANE:4 * LANE]
    cos_mt = ropet_ref[0:LANE, :]
    sin_mt = ropet_ref[LANE:2 * LANE, :]
    cos_ht = ropet_ref[2 * LANE:3 * LANE, :]
    sin_ht = ropet_ref[3 * LANE:4 * LANE, :]

    f = u[:, C_F:C_F + FOURIER_WIDTH].astype(BF16)
    uv = jnp.dot(f, wcs_ref[...], preferred_element_type=F32)
    u_o[0] = uv[:, :FOURIER_WIDTH]
    v_o[0] = uv[:, FOURIER_WIDTH:]

    def rope_rows(x1, x2, c, s, scale):
        return ((x1 * c - x2 * s) * scale).astype(BF16), ((x2 * c + x1 * s) * scale).astype(BF16)

    half = HEAD_DIM // 2
    c_h, s_h = cos_ht[0:half], sin_ht[0:half]
    for hh in range(SWA_HEADS):
        r0 = R_Q + HEAD_DIM * hh
        lo, hi = rope_rows(ut[r0:r0 + half], ut[r0 + half:r0 + HEAD_DIM], c_h, s_h, SWA_SCALE * LOG2E)
        sq_o[0, HEAD_DIM * hh:HEAD_DIM * hh + half, :] = lo
        sq_o[0, HEAD_DIM * hh + half:HEAD_DIM * (hh + 1), :] = hi
    kk = (u[:, C_K:C_K + LANE] * cos_h + u[:, C_KROT:C_KROT + LANE] * sin_h).astype(BF16)
    ones_swa = _ones_row(SWA_VT_ROWS - HEAD_DIM, tm)
    for kh in range(SWA_KV_HEADS):
        sk_o[0, kh] = kk[:, HEAD_DIM * kh:HEAD_DIM * (kh + 1)]
        sv_o[0, kh, 0:HEAD_DIM, :] = ut[R_V + HEAD_DIM * kh:R_V + HEAD_DIM * (kh + 1)].astype(BF16)
        sv_o[0, kh, HEAD_DIM:SWA_VT_ROWS, :] = ones_swa

    cqn = _rms(u[:, C_CQ:C_CQ + MLA_Q_RANK], qn_ref[...]).astype(BF16)
    qa = lax.dot_general(wuq_ref[...], cqn, _NT, preferred_element_type=F32)
    rh = MLA_ROPE_DIM // 2
    c_m, s_m = cos_mt[MLA_NOPE_DIM:MLA_NOPE_DIM + rh], sin_mt[MLA_NOPE_DIM:MLA_NOPE_DIM + rh]
    q_scale = MLA_SCALE * LOG2E
    for hh in range(MLA_HEADS):
        r0 = MLA_PAD * hh
        r1 = r0 + MLA_NOPE_DIM
        lo, hi = rope_rows(qa[r1:r1 + rh], qa[r1 + rh:r1 + 2 * rh], c_m, s_m, q_scale)
        mq_o[0, r0:r1, :] = (qa[r0:r1] * q_scale).astype(BF16)
        mq_o[0, r1:r1 + rh, :] = lo
        mq_o[0, r1 + rh:r1 + 2 * rh, :] = hi
        mq_o[0, r1 + 2 * rh:r0 + MLA_PAD, :] = jnp.zeros((MLA_PAD - MLA_NOPE_DIM - 2 * rh, tm), BF16)

    ckvn = _rms(u[:, C_CKV:C_CKV + MLA_KV_RANK], kvn_ref[...]).astype(BF16)
    kr = u[:, C_KR:C_KR + LANE] * cos_m + u[:, C_KRROT:C_KRROT + LANE] * sin_m
    kvk = jnp.dot(ckvn, wukvk_ref[...], preferred_element_type=F32)
    vt = lax.dot_general(wukvv_ref[...], ckvn, _NT, preferred_element_type=F32).astype(BF16)
    chunk = mv_o.shape[4]
    ones_mla = _ones_row(MLA_VT_ROWS - MLA_V_DIM, chunk)
    for hh in range(MLA_HEADS):
        mk_o[0, hh] = (kvk[:, MLA_PAD * hh:MLA_PAD * (hh + 1)] + kr).astype(BF16)
        for c in range(tm // chunk):
            mv_o[0, hh, c, 0:MLA_V_DIM, :] = vt[MLA_V_DIM * hh:MLA_V_DIM * (hh + 1), chunk * c:chunk * (c + 1)]
            mv_o[0, hh, c, MLA_V_DIM:MLA_VT_ROWS, :] = ones_mla


def _proj_call(h, mod3, mod_row, g1, w1, w1t, rope, ropet, wcs, qn, wuq, kvn, wukvk, wukvv, tm):
    b_, s_, _ = h.shape
    chunk = min(MLA_TK, tm)
    const = lambda b, i: (0, 0)
    if mod_row is None:
        mod_map = lambda b, i: (b, 0, 0)
    else:
        mod_map = lambda b, i: (mod_row, 0, 0)
    out_shapes = (
        jax.ShapeDtypeStruct((b_, s_, FOURIER_WIDTH), F32),
        jax.ShapeDtypeStruct((b_, s_, FOURIER_WIDTH), F32),
        jax.ShapeDtypeStruct((b_, SWA_Q_WIDTH, s_), BF16),
        jax.ShapeDtypeStruct((b_, SWA_KV_HEADS, s_, HEAD_DIM), BF16),
        jax.ShapeDtypeStruct((b_, SWA_KV_HEADS, SWA_VT_ROWS, s_), BF16),
        jax.ShapeDtypeStruct((b_, MLA_HEADS * MLA_PAD, s_), BF16),
        jax.ShapeDtypeStruct((b_, MLA_HEADS, s_, MLA_PAD), BF16),
        jax.ShapeDtypeStruct((b_, MLA_HEADS, s_ // chunk, MLA_VT_ROWS, chunk), BF16),
    )
    out_specs = (
        pl.BlockSpec((1, tm, FOURIER_WIDTH), lambda b, i: (b, i, 0)),
        pl.BlockSpec((1, tm, FOURIER_WIDTH), lambda b, i: (b, i, 0)),
        pl.BlockSpec((1, SWA_Q_WIDTH, tm), lambda b, i: (b, 0, i)),
        pl.BlockSpec((1, SWA_KV_HEADS, tm, HEAD_DIM), lambda b, i: (b, 0, i, 0)),
        pl.BlockSpec((1, SWA_KV_HEADS, SWA_VT_ROWS, tm), lambda b, i: (b, 0, 0, i)),
        pl.BlockSpec((1, MLA_HEADS * MLA_PAD, tm), lambda b, i: (b, 0, i)),
        pl.BlockSpec((1, MLA_HEADS, tm, MLA_PAD), lambda b, i: (b, 0, i, 0)),
        pl.BlockSpec((1, MLA_HEADS, tm // chunk, MLA_VT_ROWS, chunk), lambda b, i: (b, 0, i, 0, 0)),
    )
    return pl.pallas_call(
        _proj_kernel,
        grid=(b_, s_ // tm),
        in_specs=[
            pl.BlockSpec((1, tm, D_MODEL), lambda b, i: (b, i, 0)),
            pl.BlockSpec((1, 1, 6 * D_MODEL), mod_map),
            pl.BlockSpec((1, D_MODEL), const),
            pl.BlockSpec((D_MODEL, W1_COLS), const),
            pl.BlockSpec((W1T_ROWS, D_MODEL), const),
            pl.BlockSpec((tm, 4 * LANE), lambda b, i: (i, 0)),
            pl.BlockSpec((4 * LANE, tm), lambda b, i: (0, i)),
            pl.BlockSpec((FOURIER_WIDTH, 2 * FOURIER_WIDTH), const),
            pl.BlockSpec((1, MLA_Q_RANK), const),
            pl.BlockSpec((MLA_HEADS * MLA_PAD, MLA_Q_RANK), const),
            pl.BlockSpec((1, MLA_KV_RANK), const),
            pl.BlockSpec((MLA_KV_RANK, MLA_HEADS * MLA_PAD), const),
            pl.BlockSpec((MLA_HEADS * MLA_V_DIM, MLA_KV_RANK), const),
        ],
        out_specs=out_specs,
        out_shape=out_shapes,
        compiler_params=_cparams(("parallel", "parallel")),
        name="norm_in_proj",
    )(h, mod3, g1, w1, w1t, rope, ropet, wcs, qn, wuq, kvn, wukvk, wukvv)


def _dft_kernel(c_ref, s_ref, u_ref, v_ref, o_ref):
    o = jnp.dot(c_ref[...], u_ref[0].astype(BF16), preferred_element_type=F32)
    o = o + jnp.dot(s_ref[...], v_ref[0].astype(BF16), preferred_element_type=F32)
    o_ref[0] = o


def _dft_call(ctab, stab, u, v, tk):
    b_, s_, _ = u.shape
    whole = pl.BlockSpec((1, s_, FOURIER_WIDTH), lambda i, b: (b, 0, 0))
    return pl.pallas_call(
        _dft_kernel,
        grid=(s_ // tk, b_),
        in_specs=[
            pl.BlockSpec((tk, s_), lambda i, b: (i, 0)),
            pl.BlockSpec((tk, s_), lambda i, b: (i, 0)),
            whole, whole,
        ],
        out_specs=pl.BlockSpec((1, tk, FOURIER_WIDTH), lambda i, b: (b, i, 0)),
        out_shape=jax.ShapeDtypeStruct((b_, s_, FOURIER_WIDTH), F32),
        compiler_params=_cparams(("parallel", "arbitrary")),
        name="fourier_dft",
    )(ctab, stab, u, v)


FFT_SUB = 8
FFT_STEP = 2


def _fft1_kernel(m1_ref, m2_ref, tc_ref, ts_ref, u_ref, v_ref, bre_ref, bim_ref):
    n1, _, w = u_ref.shape[1:]
    sub = FFT_SUB
    for h in range(u_ref.shape[2] // sub):
        js = slice(sub * h, sub * (h + 1))
        xu = u_ref[0, :, js, :].reshape(n1 * sub, w).astype(BF16)
        xv = v_ref[0, :, js, :].reshape(n1 * sub, w).astype(BF16)
        a = jnp.dot(m1_ref[...], xu, preferred_element_type=F32)
        a = a + jnp.dot(m2_ref[...], xv, preferred_element_type=F32)
        are, aim = a[0:n1 * sub], a[n1 * sub:]
        tc, ts = tc_ref[h], ts_ref[h]
        bre_ref[0, :, js, :] = (are * tc + aim * ts).reshape(n1, sub, w)
        bim_ref[0, :, js, :] = (aim * tc - are * ts).reshape(n1, sub, w)


def _fft2_kernel(g1_ref, g2_ref, bre_ref, bim_ref, o_ref):
    n2 = g1_ref.shape[1]
    for j in range(o_ref.shape[2]):
        rows = slice(n2 * j, n2 * (j + 1))
        x = jnp.dot(g1_ref[...], bre_ref[0, rows, :].astype(BF16), preferred_element_type=F32)
        x = x + jnp.dot(g2_ref[...], bim_ref[0, rows, :].astype(BF16), preferred_element_type=F32)
        o_ref[0, :, j, :] = x


def _fft_call(tabs, u, v):
    m1, m2, tc, ts, g1, g2 = tabs
    b_, s_, w = u.shape
    n1 = m1.shape[1] // FFT_SUB
    n2 = s_ // n1
    u4 = u.reshape(b_, n1, n2, w)
    v4 = v.reshape(b_, n1, n2, w)
    per_step = FFT_STEP * FFT_SUB
    tile = pl.BlockSpec((1, n1, per_step, w), lambda t, b: (b, 0, t, 0))
    tab = pl.BlockSpec((FFT_STEP, n1 * FFT_SUB, w), lambda t, b: (t, 0, 0))
    mat = pl.BlockSpec((2 * n1 * FFT_SUB, n1 * FFT_SUB), lambda t, b: (0, 0))
    bre, bim = pl.pallas_call(
        _fft1_kernel,
        grid=(n2 // per_step, b_),
        in_specs=[mat, mat, tab, tab, tile, tile],
        out_specs=(tile, tile),
        out_shape=(jax.ShapeDtypeStruct((b_, n1, n2, w), F32),) * 2,
        compiler_params=_cparams(("parallel", "arbitrary")),
        name="fourier_fft1",
    )(m1, m2, tc, ts, u4, v4)
    blk = pl.BlockSpec((1, per_step * n2, w), lambda b, t: (b, t, 0))
    mat2 = pl.BlockSpec((n2, n2), lambda b, t: (0, 0))
    out = pl.pallas_call(
        _fft2_kernel,
        grid=(b_, n1 // per_step),
        in_specs=[mat2, mat2, blk, blk],
        out_specs=pl.BlockSpec((1, n2, per_step, w), lambda b, t: (b, 0, t, 0)),
        out_shape=jax.ShapeDtypeStruct((b_, n2, n1, w), F32),
        compiler_params=_cparams(("parallel", "parallel")),
        name="fourier_fft2",
    )(g1, g2, bre.reshape(b_, s_, w), bim.reshape(b_, s_, w))
    return out.reshape(b_, s_, w)


def _fft_tables(s_, n1, w):
    n2 = s_ // n1
    k1 = jnp.arange(n1, dtype=jnp.int32)
    a1 = ((k1[:, None] * k1[None, :]) % n1).astype(F32) * (2.0 * math.pi / n1)
    c1, s1 = jnp.cos(a1), jnp.sin(a1)
    eye = jnp.eye(FFT_SUB, dtype=F32)
    m1 = jnp.kron(jnp.concatenate([c1, -s1], axis=0), eye).astype(BF16)
    m2 = jnp.kron(jnp.concatenate([-s1, -c1], axis=0), eye).astype(BF16)
    s2 = jnp.arange(n2, dtype=jnp.int32)
    n_t = n2 // FFT_SUB
    row = jnp.arange(n1 * FFT_SUB, dtype=jnp.int32)
    s2_of = FFT_SUB * jnp.arange(n_t, dtype=jnp.int32)[:, None] + (row % FFT_SUB)[None, :]
    th = (((row // FFT_SUB)[None, :] * s2_of) % s_).astype(F32) * (2.0 * math.pi / s_)
    tc = jnp.broadcast_to(jnp.cos(th)[:, :, None], (n_t, n1 * FFT_SUB, w))
    ts = jnp.broadcast_to(jnp.sin(th)[:, :, None], (n_t, n1 * FFT_SUB, w))
    a2 = ((s2[:, None] * s2[None, :]) % n2).astype(F32) * (2.0 * math.pi / n2)
    scale = 1.0 / math.sqrt(s_)
    g1 = (jnp.cos(a2) * scale).astype(BF16)
    g2 = (jnp.sin(a2) * scale).astype(BF16)
    return m1, m2, tc, ts, g1, g2


def _mla_kernel(*refs, n_chunks, heads_per_step):
    hps = heads_per_step
    n_in = 6 if n_chunks else 4
    if n_chunks:
        zero_ref, qt_ref, k_ref, vt_ref, kc_ref, vct_ref = refs[:n_in]
    else:
        zero_ref, qt_ref, kc_ref, vct_ref = refs[:n_in]
    o_ref = refs[n_in]
    row0 = pl.multiple_of(zero_ref[0], MLA_TK)
    scratch = refs[n_in + 1:]
    tq = scratch[0].shape[1]
    n_sub = qt_ref.shape[2] // tq
    lc = kc_ref.shape[2]
    s_refs = [[scratch[MLA_S_SLOTS * (hps * u + hh):MLA_S_SLOTS * (hps * u + hh + 1)] for hh in range(hps)]
              for u in range(n_sub)]

    def keys(idx):
        if idx == 0:
            return (lambda hh: kc_ref[0, hh]), lc
        return (lambda hh: k_ref[0, hh, (idx - 1) * MLA_TK:idx * MLA_TK, :]), MLA_TK

    def values(idx):
        if idx == 0:
            return lambda hh: vct_ref[0, hh, 0]
        return lambda hh: vt_ref[0, hh, idx - 1]

    def logits(u, idx):
        ks, rows = keys(idx)
        cms = []
        for hh in range(hps):
            qt = qt_ref[0, MLA_PAD * hh:MLA_PAD * (hh + 1), tq * u:tq * (u + 1)]
            s = jnp.dot(ks(hh), qt, preferred_element_type=F32)
            s_refs[u][hh][idx % MLA_S_SLOTS][0:rows, :] = s
            cms.append(jnp.max(s, axis=0, keepdims=True))
        return tuple(cms)

    def accumulate(u, idx, carries, cms):
        _, rows = keys(idx)
        vts = values(idx)
        out = []
        for hh in range(hps):
            m, acc = carries[hh]
            m_new = jnp.maximum(m, cms[hh])
            alpha = jnp.exp2(m - m_new)
            s = s_refs[u][hh][idx % MLA_S_SLOTS][pl.ds(row0, rows), :]
            p = jnp.exp2(s - m_new).astype(BF16)
            acc = alpha * acc + jnp.dot(vts(hh), p, preferred_element_type=F32)
            out.append((m_new, acc))
        return tuple(out)

    init = (jnp.full((1, tq), NEG_INF, F32), jnp.zeros((MLA_VT_ROWS, tq), F32))
    carries = [tuple(init for _ in range(hps)) for _ in range(n_sub)]
    pending = [None] * n_sub
    n_idx = n_chunks + 1
    for t in range(n_idx + n_sub):
        for u in range(n_sub):
            idx = t - u
            new_cms = logits(u, idx) if 0 <= idx < n_idx else None
            if 1 <= idx <= n_idx:
                carries[u] = accumulate(u, idx - 1, carries[u], pending[u])
            pending[u] = new_cms
    for u in range(n_sub):
        o = jnp.concatenate([acc[0:MLA_V_DIM] / acc[MLA_V_DIM:MLA_V_DIM + 1] for (_, acc) in carries[u]],
                            axis=0)
        o_ref[0, tq * u:tq * (u + 1), :] = o.T.astype(o_ref.dtype)


def _mla_call(qt, k, vt, kc, vct, tq, n_sub):
    b_, _, sq = qt.shape
    h_ = kc.shape[1]
    lc = kc.shape[2]
    hps = 2
    n_chunks = 0 if k is None else k.shape[2] // MLA_TK
    assert n_chunks == 0 or lc <= MLA_TK
    in_specs = [pl.BlockSpec(memory_space=pltpu.SMEM),
                pl.BlockSpec((1, hps * MLA_PAD, tq), lambda b, p, i: (b, p, i))]
    args = [jnp.zeros((1,), jnp.int32), qt]
    if k is not None:
        sk = k.shape[2]
        in_specs += [pl.BlockSpec((1, hps, sk, MLA_PAD), lambda b, p, i: (b, p, 0, 0)),
                     pl.BlockSpec((1, hps, n_chunks, MLA_VT_ROWS, MLA_TK), lambda b, p, i: (b, p, 0, 0, 0))]
        args += [k, vt]
    in_specs += [pl.BlockSpec((1, hps, lc, MLA_PAD), lambda b, p, i: (b, p, 0, 0)),
                 pl.BlockSpec((1, hps, 1, MLA_VT_ROWS, lc), lambda b, p, i: (b, p, 0, 0, 0))]
    args += [kc, vct]
    return pl.pallas_call(
        functools.partial(_mla_kernel, n_chunks=n_chunks, heads_per_step=hps),
        grid=(b_, h_ // hps, sq // tq),
        in_specs=in_specs,
        out_specs=pl.BlockSpec((1, tq, hps * MLA_V_DIM), lambda b, p, i: (b, i, p)),
        out_shape=jax.ShapeDtypeStruct((b_, sq, h_ * MLA_V_DIM), BF16),
        scratch_shapes=[pltpu.VMEM((MLA_TK if n_chunks else lc, tq // n_sub), F32)] * (MLA_S_SLOTS * hps * n_sub),
        compiler_params=_cparams(("parallel", "parallel", "arbitrary")),
        name="mla_attn",
    )(*args)


def _swa_kernel(*refs, n_local, s_len):
    zero_ref, sink_ref, qt_ref = refs[:3]
    k_refs = refs[3:3 + n_local]
    vt_refs = refs[3 + n_local:3 + 2 * n_local]
    kc_ref, vct_ref, o_ref = refs[3 + 2 * n_local:6 + 2 * n_local]
    s_refs = refs[6 + 2 * n_local:]
    row0 = pl.multiple_of(zero_ref[0], BLOCK)
    tq = s_refs[0].shape[1]
    step_tq = qt_ref.shape[2]
    n_sub = step_tq // tq
    lc = kc_ref.shape[2]
    sub_blocks = tq // BLOCK + 2 if n_local else 0
    nk = sub_blocks * BLOCK
    sinks = [sink_ref[hh] * LOG2E for hh in range(SWA_HEADS)]

    def mask_bias(u):
        q0 = pl.program_id(1) * step_tq + u * tq
        kpos = q0 - BLOCK + lax.broadcasted_iota(jnp.int32, (nk, tq), 0)
        qpos = q0 + lax.broadcasted_iota(jnp.int32, (nk, tq), 1)
        valid = (jnp.abs(kpos - qpos) <= WINDOW) & (kpos >= 0) & (kpos < s_len)
        return jnp.where(valid, 0.0, NEG_INF).astype(F32)

    def keys(u, kh):
        kc = kc_ref[0, kh]
        vct = vct_ref[0, kh]
        if not n_local:
            return kc, vct
        first = u * (tq // BLOCK)
        blocks = range(first, first + sub_blocks)
        return (jnp.concatenate([kc] + [k_refs[j][0, kh] for j in blocks], axis=0),
                jnp.concatenate([vct] + [vt_refs[j][0, kh] for j in blocks], axis=1))

    def logits(unit, u, hh, k_all, bias):
        qt = qt_ref[0, HEAD_DIM * hh:HEAD_DIM * (hh + 1), tq * u:tq * (u + 1)]
        s_ref = s_refs[unit % 2]
        s_c = jnp.dot(k_all[0:lc], qt, preferred_element_type=F32)
        s_ref[0:lc, :] = s_c
        m = jnp.maximum(jnp.max(s_c, axis=0, keepdims=True), sinks[hh])
        if n_local:
            s_l = jnp.dot(k_all[lc:], qt, preferred_element_type=F32) + bias
            s_ref[lc:lc + nk, :] = s_l
            m = jnp.maximum(m, jnp.max(s_l, axis=0, keepdims=True))
        return m

    def accumulate(unit, hh, vt_all, m):
        p = jnp.exp2(s_refs[unit % 2][pl.ds(row0, lc + nk), :] - m).astype(BF16)
        acc = jnp.dot(vt_all, p, preferred_element_type=F32)
        l = acc[HEAD_DIM:HEAD_DIM + 1] + jnp.exp2(sinks[hh] - m)
        return acc[0:HEAD_DIM] / l

    units = [(u, hh) for u in range(n_sub) for hh in range(SWA_HEADS)]
    kv = {(u, kh): keys(u, kh) for u in range(n_sub) for kh in range(SWA_KV_HEADS)}
    biases = [mask_bias(u) if n_local else None for u in range(n_sub)]
    outs = [[] for _ in range(n_sub)]
    m_prev = None
    for unit in range(len(units) + 1):
        m_next = None
        if unit < len(units):
            u, hh = units[unit]
            m_next = logits(unit, u, hh, kv[(u, hh // SWA_GROUP)][0], biases[u])
        if unit > 0:
            u, hh = units[unit - 1]
            outs[u].append(accumulate(unit - 1, hh, kv[(u, hh // SWA_GROUP)][1], m_prev))
        m_prev = m_next
    for u in range(n_sub):
        o = jnp.concatenate(outs[u], axis=0)
        o_ref[0, tq * u:tq * (u + 1), :] = o.T.astype(o_ref.dtype)


def _swa_call(sink, qt, k, vt, kc, vct):
    b_, _, sq = qt.shape
    lc = kc.shape[2]
    local = k is not None
    sub_tq = SWA_TQ if local else sq
    tq = sub_tq * SWA_SUB_TILES if local else sq
    nb = sq // tq
    n_local = tq // BLOCK + 2 if local else 0
    sub_keys = lc + (sub_tq // BLOCK + 2) * BLOCK if local else lc
    in_specs = [pl.BlockSpec(memory_space=pltpu.SMEM), pl.BlockSpec(memory_space=pltpu.SMEM),
                pl.BlockSpec((1, SWA_Q_WIDTH, tq), lambda b, n: (b, 0, n))]
    args = [jnp.zeros((1,), jnp.int32), sink, qt]
    if local:
        nkb = sq // BLOCK
        per = tq // BLOCK

        def blk_idx(n, j):
            return jnp.clip(n * per - 1 + j, 0, nkb - 1)

        in_specs += [pl.BlockSpec((1, SWA_KV_HEADS, BLOCK, HEAD_DIM),
                                  functools.partial(lambda b, n, j: (b, 0, blk_idx(n, j), 0), j=j))
                     for j in range(n_local)]
        in_specs += [pl.BlockSpec((1, SWA_KV_HEADS, SWA_VT_ROWS, BLOCK),
                                  functools.partial(lambda b, n, j: (b, 0, 0, blk_idx(n, j)), j=j))
                     for j in range(n_local)]
        args += [k] * n_local + [vt] * n_local
    in_specs += [pl.BlockSpec((1, SWA_KV_HEADS, lc, HEAD_DIM), lambda b, n: (b, 0, 0, 0)),
                 pl.BlockSpec((1, SWA_KV_HEADS, SWA_VT_ROWS, lc), lambda b, n: (b, 0, 0, 0))]
    args += [kc, vct]
    return pl.pallas_call(
        functools.partial(_swa_kernel, n_local=n_local, s_len=sq),
        grid=(b_, nb),
        in_specs=in_specs,
        out_specs=pl.BlockSpec((1, tq, SWA_Q_WIDTH), lambda b, n: (b, n, 0)),
        out_shape=jax.ShapeDtypeStruct((b_, sq, SWA_Q_WIDTH), BF16),
        scratch_shapes=[pltpu.VMEM((sub_keys, sub_tq), F32)] * 2,
        compiler_params=_cparams(("parallel", "arbitrary")),
        name="swa_attn",
    )(*args)


def _mlp_kernel(fo_ref, so_ref, mo_ref, h_ref, mod_ref, g2_ref, wo_ref, w1_ref, w2_ref, gf_ref, o_ref, *,
                final, ff_chunk):
    mod = mod_ref[0]
    g1 = mod[:, 2 * D_MODEL:3 * D_MODEL]
    sh2 = mod[:, 3 * D_MODEL:4 * D_MODEL]
    sc2 = mod[:, 4 * D_MODEL:5 * D_MODEL]
    g2 = mod[:, 5 * D_MODEL:6 * D_MODEL]
    mix = jnp.concatenate([fo_ref[0].astype(BF16), so_ref[0], mo_ref[0]], axis=-1)
    h = h_ref[0] + g1 * jnp.dot(mix, wo_ref[...], preferred_element_type=F32)
    y = (_rms(h, g2_ref[...]) * (1.0 + sc2) + sh2).astype(BF16)
    acc = None
    for c in range(D_FF // ff_chunk):
        a = jnp.dot(y, w1_ref[:, c * ff_chunk:(c + 1) * ff_chunk], preferred_element_type=F32)
        a = jnp.maximum(a, 0.0)
        a = (a * a).astype(BF16)
        part = jnp.dot(a, w2_ref[c * ff_chunk:(c + 1) * ff_chunk, :], preferred_element_type=F32)
        acc = part if acc is None else acc + part
    h = h + g2 * acc
    if final:
        h = _rms(h, gf_ref[...])
    o_ref[0] = h


def _mlp_call(fo, so, mo, h, mod3, mod_row, g2, wo, w1, w2, gf, tm, final):
    b_, s_, _ = h.shape
    const = lambda b, i: (0, 0)
    tile = lambda b, i: (b, i, 0)
    if mod_row is None:
        mod_map = lambda b, i: (b, 0, 0)
    else:
        mod_map = lambda b, i: (mod_row, 0, 0)
    once = pl.Buffered(1)
    return pl.pallas_call(
        functools.partial(_mlp_kernel, final=final, ff_chunk=1024),
        grid=(b_, s_ // tm),
        in_specs=[
            pl.BlockSpec((1, tm, FOURIER_WIDTH), tile),
            pl.BlockSpec((1, tm, SWA_Q_WIDTH), tile),
            pl.BlockSpec((1, tm, MLA_HEADS * MLA_V_DIM), tile),
            pl.BlockSpec((1, tm, D_MODEL), tile),
            pl.BlockSpec((1, 1, 6 * D_MODEL), mod_map),
            pl.BlockSpec((1, D_MODEL), const),
            pl.BlockSpec((D_MIX, D_MODEL), const, pipeline_mode=once),
            pl.BlockSpec((D_MODEL, D_FF), const, pipeline_mode=once),
            pl.BlockSpec((D_FF, D_MODEL), const, pipeline_mode=once),
            pl.BlockSpec((1, D_MODEL), const),
        ],
        out_specs=pl.BlockSpec((1, tm, D_MODEL), tile),
        out_shape=jax.ShapeDtypeStruct((b_, s_, D_MODEL), F32),
        compiler_params=_cparams(("parallel", "parallel")),
        name="out_proj_mlp",
    )(fo, so, mo, h, mod3, g2, wo, w1, w2, gf)


def _rot_cols(w, heads, dim):
    w4 = w.reshape(w.shape[0], heads, 2, dim // 2)
    return jnp.stack([-w4[:, :, 1], w4[:, :, 0]], axis=2).reshape(w.shape[0], heads * dim)


def _pad_lanes(w, heads, before, after):
    w3 = w.reshape(w.shape[0], heads, w.shape[1] // heads)
    w3 = jnp.pad(w3, ((0, 0), (0, 0), (before, after)))
    return w3.reshape(w.shape[0], -1)


def _pack_w_in(w_in):
    wk = w_in[:, OFF_SWA_K:OFF_SWA_V]
    wkr = w_in[:, OFF_MLA_KR:OFF_MLA_KR + MLA_ROPE_DIM]
    pad_kr = lambda w: _pad_lanes(w, 1, MLA_NOPE_DIM, LANE - MLA_NOPE_DIM - MLA_ROPE_DIM)
    cols = [w_in[:, :OFF_SWA_Q], wk, w_in[:, OFF_MLA_CQ:OFF_MLA_KR], pad_kr(wkr),
            _rot_cols(wk, SWA_KV_HEADS, HEAD_DIM), pad_kr(_rot_cols(wkr, 1, MLA_ROPE_DIM))]
    rows = w_in[:, OFF_SWA_Q:OFF_SWA_K], w_in[:, OFF_SWA_V:OFF_MLA_CQ]
    return jnp.concatenate(cols, axis=1).astype(BF16), jnp.concatenate(rows, axis=1).T.astype(BF16)


def _pack_w_uq(w_uq):
    per = MLA_NOPE_DIM + MLA_ROPE_DIM
    return _pad_lanes(w_uq, MLA_HEADS, 0, MLA_PAD - per).T.astype(BF16)


def _pack_w_ukv(w_ukv):
    w3 = w_ukv.reshape(MLA_KV_RANK, MLA_HEADS, MLA_NOPE_DIM + MLA_V_DIM)
    wk = w3[:, :, :MLA_NOPE_DIM].reshape(MLA_KV_RANK, MLA_HEADS * MLA_NOPE_DIM)
    wv = w3[:, :, MLA_NOPE_DIM:].reshape(MLA_KV_RANK, MLA_HEADS * MLA_V_DIM)
    return _pad_lanes(wk, MLA_HEADS, 0, MLA_PAD - MLA_NOPE_DIM).astype(BF16), wv.T.astype(BF16)


def _rope_table(rows):
    r, col = jnp.meshgrid(jnp.arange(rows, dtype=F32), jnp.arange(GRID_W, dtype=F32), indexing="ij")
    r = r.reshape(-1)
    col = col.reshape(-1)

    def tables(dim):
        n_freq = dim // 4
        inv = ROPE_THETA ** (-jnp.arange(n_freq, dtype=F32) / n_freq)
        ang = jnp.concatenate([r[:, None] * inv[None, :], col[:, None] * inv[None, :]], axis=-1)
        return jnp.cos(ang), jnp.sin(ang)

    ch, sh = tables(HEAD_DIM)
    cr, sr = tables(MLA_ROPE_DIM)
    mla_pad = ((0, 0), (MLA_NOPE_DIM, LANE - MLA_NOPE_DIM - MLA_ROPE_DIM))
    nope_ones = (jnp.arange(LANE) < MLA_NOPE_DIM).astype(F32)[None, :]
    return jnp.concatenate([jnp.tile(ch, (1, 4)), jnp.tile(sh, (1, 4)),
                            jnp.pad(jnp.tile(cr, (1, 2)), mla_pad) + nope_ones,
                            jnp.pad(jnp.tile(sr, (1, 2)), mla_pad)], axis=-1)


def _identity_rope_table(n):
    ones = jnp.ones((n, LANE), F32)
    zeros = jnp.zeros((n, LANE), F32)
    m = jnp.concatenate([jnp.ones((n, MLA_NOPE_DIM + MLA_ROPE_DIM), F32),
                         jnp.zeros((n, LANE - MLA_NOPE_DIM - MLA_ROPE_DIM), F32)], axis=-1)
    return jnp.concatenate([ones, zeros, m, zeros], axis=-1)


def _dft_tables(n):
    k = jnp.arange(n, dtype=jnp.int32)
    ks = (k[:, None] * k[None, :]) % n
    ang = ks.astype(F32) * (2.0 * math.pi / n)
    scale = 1.0 / math.sqrt(n)
    return (jnp.cos(ang) * scale).astype(BF16), (-jnp.sin(ang) * scale).astype(BF16)


def kernel(x, c, ctx, c_ctx, w_ada, b_ada, norm1_g, norm2_g, w_in, w_fourier, swa_sink, mla_q_norm, w_uq,
           mla_kv_norm, w_ukv, w_out, w_mlp1, w_mlp2, final_norm_g):
    b_, s_, _ = x.shape
    lc = ctx.shape[1]
    depth = w_ada.shape[0]
    rows = s_ // GRID_W

    mod_rows = 16
    cc = jnp.concatenate([c, c_ctx[None, :], jnp.zeros((mod_rows - b_ - 1, D_MODEL), F32)], axis=0)
    mod_all = _ada_call(cc, w_ada, b_ada)

    rope = _rope_table(rows)
    rope_c = _identity_rope_table(lc)
    to_feature_major = lambda t: jnp.concatenate([t[:, 2 * LANE:], t[:, :2 * LANE]], axis=1).T
    ropet = to_feature_major(rope)
    ropet_c = to_feature_major(rope_c)
    fft_tabs = _fft_tables(s_, GRID_W, FOURIER_WIDTH)
    ctab_c, stab_c = _dft_tables(lc)
    kk = jnp.arange(FOURIER_GROUP_DIM, dtype=jnp.int32)
    ang64 = ((kk[:, None] * kk[None, :]) % FOURIER_GROUP_DIM).astype(F32) * (2.0 * math.pi / FOURIER_GROUP_DIM)
    c64, s64 = jnp.cos(ang64), jnp.sin(ang64)

    h, hc = x, ctx
    for l in range(depth):
        last = l == depth - 1
        mod3 = mod_all[l].reshape(mod_rows, 1, 6 * D_MODEL)
        g1 = norm1_g[l].reshape(1, D_MODEL)
        g2 = norm2_g[l].reshape(1, D_MODEL)
        gf = final_norm_g.reshape(1, D_MODEL)
        w1, w1t = _pack_w_in(w_in[l])
        wuq = _pack_w_uq(w_uq[l])
        wukvk, wukvv = _pack_w_ukv(w_ukv[l])
        wcs = _fw_call(c64, s64, w_fourier[l]).astype(BF16)
        qn = mla_q_norm[l].reshape(1, MLA_Q_RANK)
        kvn = mla_kv_norm[l].reshape(1, MLA_KV_RANK)
        wo = w_out[l].astype(BF16)
        wm1 = w_mlp1[l].astype(BF16)
        wm2 = w_mlp2[l].astype(BF16)
        sink = swa_sink[l]

        fu, fv, sq, sk, sv, mq, mk, mv = _proj_call(h, mod3, None, g1, w1, w1t, rope, ropet, wcs, qn, wuq,
                                                    kvn, wukvk, wukvv, tm=512)
        fuc, fvc, sqc, skc, svc, mqc, mkc, mvc = _proj_call(hc, mod3, b_, g1, w1, w1t, rope_c, ropet_c, wcs, qn,
                                                            wuq, kvn, wukvk, wukvv, tm=lc)

        fo = _fft_call(fft_tabs, fu, fv)
        so = _swa_call(sink, sq, sk, sv, skc, svc)
        mo = _mla_call(mq, mk, mv, mkc, mvc, tq=1024, n_sub=2)
        h = _mlp_call(fo, so, mo, h, mod3, None, g2, wo, wm1, wm2, gf, tm=512, final=last)

        if not last:
            foc = _dft_call(ctab_c, stab_c, fuc, fvc, tk=lc)
            soc = _swa_call(sink, sqc, None, None, skc, svc)
            moc = _mla_call(mqc, None, None, mkc, mvc, tq=lc, n_sub=1)
            hc = _mlp_call(foc, soc, moc, hc, mod3, b_, g2, wo, wm1, wm2, gf, tm=lc, final=False)
    return h
```

```python
import functools
import math

import jax
import jax.numpy as jnp
import numpy as np
from jax import lax
from jax.experimental import pallas as pl
from jax.experimental.pallas import tpu as pltpu

D_MODEL = 1024
GRID_W = 64
HEAD_DIM = 64
FOURIER_GROUPS = 4
FOURIER_GROUP_DIM = 64
FOURIER_WIDTH = FOURIER_GROUPS * FOURIER_GROUP_DIM
SWA_HEADS = 6
SWA_KV_HEADS = 2
SWA_GROUP = SWA_HEADS // SWA_KV_HEADS
SWA_Q_WIDTH = SWA_HEADS * HEAD_DIM
SWA_KV_WIDTH = SWA_KV_HEADS * HEAD_DIM
WINDOW = 128
BLOCK = 128
MLA_HEADS = 6
MLA_NOPE_DIM = 64
MLA_ROPE_DIM = 32
MLA_V_DIM = 64
MLA_Q_RANK = 256
MLA_KV_RANK = 128
MLA_SCALE = (MLA_NOPE_DIM + MLA_ROPE_DIM) ** -0.5
SWA_SCALE = HEAD_DIM ** -0.5
D_MIX = FOURIER_WIDTH + SWA_Q_WIDTH + MLA_HEADS * MLA_V_DIM
OFF_SWA_Q = FOURIER_WIDTH
OFF_SWA_K = OFF_SWA_Q + SWA_Q_WIDTH
OFF_SWA_V = OFF_SWA_K + SWA_KV_WIDTH
OFF_MLA_CQ = OFF_SWA_V + SWA_KV_WIDTH
OFF_MLA_CKV = OFF_MLA_CQ + MLA_Q_RANK
OFF_MLA_KR = OFF_MLA_CKV + MLA_KV_RANK
D_FF = 4 * D_MODEL
ROPE_THETA = 10000.0
NORM_EPS = 1e-6
NEG_INF = -1e30

LANE = 128
MLA_PAD = LANE

C_F = 0
C_K = C_F + FOURIER_WIDTH
C_CQ = C_K + SWA_KV_WIDTH
C_CKV = C_CQ + MLA_Q_RANK
C_KR = C_CKV + MLA_KV_RANK
C_KROT = C_KR + LANE
C_KRROT = C_KROT + SWA_KV_WIDTH
W1_COLS = C_KRROT + LANE
R_Q = 0
R_V = R_Q + SWA_Q_WIDTH
W1T_ROWS = R_V + SWA_KV_WIDTH

BF16 = jnp.bfloat16
F32 = jnp.float32
LOG2E = math.log2(math.e)
_NT = (((1,), (1,)), ((), ()))
MLA_TK = 256
MLA_VT_ROWS = MLA_V_DIM + 16
MLA_S_SLOTS = 2
SWA_VT_ROWS = HEAD_DIM + 16
SWA_TQ = 256
SWA_SUB_TILES = 4

VMEM_LIMIT = 56 * 1024 * 1024


def _cparams(sem):
    return pltpu.CompilerParams(dimension_semantics=sem, vmem_limit_bytes=VMEM_LIMIT)


def _rms(x, g):
    ms = jnp.mean(x * x, axis=-1, keepdims=True)
    return x * lax.rsqrt(ms + NORM_EPS) * g


def _ada_kernel(cc_ref, w_ref, b_ref, o_ref):
    cc = cc_ref[...]
    s = cc * jax.nn.sigmoid(cc)
    o_ref[0] = jnp.dot(s, w_ref[0], preferred_element_type=F32,
                       precision=lax.Precision.HIGHEST) + b_ref[0]


def _ada_call(cc, w_ada, b_ada):
    depth = w_ada.shape[0]
    rows = cc.shape[0]
    tn = 1536
    n_out = w_ada.shape[2]
    return pl.pallas_call(
        _ada_kernel,
        grid=(depth, n_out // tn),
        in_specs=[
            pl.BlockSpec((rows, D_MODEL), lambda l, j: (0, 0)),
            pl.BlockSpec((1, D_MODEL, tn), lambda l, j: (l, 0, j)),
            pl.BlockSpec((1, 1, tn), lambda l, j: (l, 0, j)),
        ],
        out_specs=pl.BlockSpec((1, rows, tn), lambda l, j: (l, 0, j)),
        out_shape=jax.ShapeDtypeStruct((depth, rows, n_out), F32),
        compiler_params=_cparams(("arbitrary", "arbitrary")),
        name="ada_mod",
    )(cc, w_ada, b_ada.reshape(depth, 1, n_out))


def _fw_kernel(c_ref, s_ref, w_ref, o_ref):
    o_ref[...] = jnp.zeros(o_ref.shape, o_ref.dtype)
    inv = 1.0 / math.sqrt(FOURIER_GROUP_DIM)
    for g in range(FOURIER_GROUPS):
        w = w_ref[g]
        a = jnp.dot(c_ref[...], w, preferred_element_type=F32, precision=lax.Precision.HIGHEST) * inv
        b = jnp.dot(s_ref[...], w, preferred_element_type=F32, precision=lax.Precision.HIGHEST) * inv
        lo = g * FOURIER_GROUP_DIM
        hi = lo + FOURIER_GROUP_DIM
        o_ref[lo:hi, lo:hi] = a
        o_ref[lo:hi, FOURIER_WIDTH + lo:FOURIER_WIDTH + hi] = b


def _fw_call(c64, s64, w_f):
    return pl.pallas_call(
        _fw_kernel,
        out_shape=jax.ShapeDtypeStruct((FOURIER_WIDTH, 2 * FOURIER_WIDTH), F32),
        name="fourier_w",
    )(c64, s64, w_f)


def _ones_row(rows, cols):
    return (lax.broadcasted_iota(jnp.int32, (rows, cols), 0) == 0).astype(BF16)


def _proj_kernel(h_ref, mod_ref, g1_ref, w1_ref, w1t_ref, rope_ref, ropet_ref, wcs_ref, qn_ref, wuq_ref,
                 kvn_ref, wukvk_ref, wukvv_ref,
                 u_o, v_o, sq_o, sk_o, sv_o, mq_o, mk_o, mv_o):
    x = h_ref[0]
    tm = x.shape[0]
    mod = mod_ref[0]
    sh = mod[:, 0:D_MODEL]
    sc = mod[:, D_MODEL:2 * D_MODEL]
    y = (_rms(x, g1_ref[...]) * (1.0 + sc) + sh).astype(BF16)
    u = jnp.dot(y, w1_ref[...], preferred_element_type=F32)
    ut = lax.dot_general(w1t_ref[...], y, _NT, preferred_element_type=F32)

    cos_h = rope_ref[:, 0:LANE]
    sin_h = rope_ref[:, LANE:2 * LANE]
    cos_m = rope_ref[:, 2 * LANE:3 * LANE]
    sin_m = rope_ref[:, 3 * LANE:4 * LANE]
    cos_mt = ropet_ref[0:LANE, :]
    sin_mt = ropet_ref[LANE:2 * LANE, :]
    cos_ht = ropet_ref[2 * LANE:3 * LANE, :]
    sin_ht = ropet_ref[3 * LANE:4 * LANE, :]

    f = u[:, C_F:C_F + FOURIER_WIDTH].astype(BF16)
    uv = jnp.dot(f, wcs_ref[...], preferred_element_type=F32)
    u_o[0] = uv[:, :FOURIER_WIDTH]
    v_o[0] = uv[:, FOURIER_WIDTH:]

    def rope_rows(x1, x2, c, s, scale):
        return ((x1 * c - x2 * s) * scale).astype(BF16), ((x2 * c + x1 * s) * scale).astype(BF16)

    half = HEAD_DIM // 2
    c_h, s_h = cos_ht[0:half], sin_ht[0:half]
    for hh in range(SWA_HEADS):
        r0 = R_Q + HEAD_DIM * hh
        lo, hi = rope_rows(ut[r0:r0 + half], ut[r0 + half:r0 + HEAD_DIM], c_h, s_h, SWA_SCALE * LOG2E)
        sq_o[0, HEAD_DIM * hh:HEAD_DIM * hh + half, :] = lo
        sq_o[0, HEAD_DIM * hh + half:HEAD_DIM * (hh + 1), :] = hi
    kk = (u[:, C_K:C_K + LANE] * cos_h + u[:, C_KROT:C_KROT + LANE] * sin_h).astype(BF16)
    ones_swa = _ones_row(SWA_VT_ROWS - HEAD_DIM, tm)
    for kh in range(SWA_KV_HEADS):
        sk_o[0, kh] = kk[:, HEAD_DIM * kh:HEAD_DIM * (kh + 1)]
        sv_o[0, kh, 0:HEAD_DIM, :] = ut[R_V + HEAD_DIM * kh:R_V + HEAD_DIM * (kh + 1)].astype(BF16)
        sv_o[0, kh, HEAD_DIM:SWA_VT_ROWS, :] = ones_swa

    cqn = _rms(u[:, C_CQ:C_CQ + MLA_Q_RANK], qn_ref[...]).astype(BF16)
    qa = lax.dot_general(wuq_ref[...], cqn, _NT, preferred_element_type=F32)
    rh = MLA_ROPE_DIM // 2
    c_m, s_m = cos_mt[MLA_NOPE_DIM:MLA_NOPE_DIM + rh], sin_mt[MLA_NOPE_DIM:MLA_NOPE_DIM + rh]
    q_scale = MLA_SCALE * LOG2E
    for hh in range(MLA_HEADS):
        r0 = MLA_PAD * hh
        r1 = r0 + MLA_NOPE_DIM
        lo, hi = rope_rows(qa[r1:r1 + rh], qa[r1 + rh:r1 + 2 * rh], c_m, s_m, q_scale)
        mq_o[0, r0:r1, :] = (qa[r0:r1] * q_scale).astype(BF16)
        mq_o[0, r1:r1 + rh, :] = lo
        mq_o[0, r1 + rh:r1 + 2 * rh, :] = hi
        mq_o[0, r1 + 2 * rh:r0 + MLA_PAD, :] = jnp.zeros((MLA_PAD - MLA_NOPE_DIM - 2 * rh, tm), BF16)

    ckvn = _rms(u[:, C_CKV:C_CKV + MLA_KV_RANK], kvn_ref[...]).astype(BF16)
    kr = u[:, C_KR:C_KR + LANE] * cos_m + u[:, C_KRROT:C_KRROT + LANE] * sin_m
    kvk = jnp.dot(ckvn, wukvk_ref[...], preferred_element_type=F32)
    vt = lax.dot_general(wukvv_ref[...], ckvn, _NT, preferred_element_type=F32).astype(BF16)
    chunk = mv_o.shape[4]
    ones_mla = _ones_row(MLA_VT_ROWS - MLA_V_DIM, chunk)
    for hh in range(MLA_HEADS):
        mk_o[0, hh] = (kvk[:, MLA_PAD * hh:MLA_PAD * (hh + 1)] + kr).astype(BF16)
        for c in range(tm // chunk):
            mv_o[0, hh, c, 0:MLA_V_DIM, :] = vt[MLA_V_DIM * hh:MLA_V_DIM * (hh + 1), chunk * c:chunk * (c + 1)]
            mv_o[0, hh, c, MLA_V_DIM:MLA_VT_ROWS, :] = ones_mla


def _proj_call(h, mod3, mod_row, g1, w1, w1t, rope, ropet, wcs, qn, wuq, kvn, wukvk, wukvv, tm):
    b_, s_, _ = h.shape
    chunk = min(MLA_TK, tm)
    const = lambda b, i: (0, 0)
    if mod_row is None:
        mod_map = lambda b, i: (b, 0, 0)
    else:
        mod_map = lambda b, i: (mod_row, 0, 0)
    out_shapes = (
        jax.ShapeDtypeStruct((b_, s_, FOURIER_WIDTH), F32),
        jax.ShapeDtypeStruct((b_, s_, FOURIER_WIDTH), F32),
        jax.ShapeDtypeStruct((b_, SWA_Q_WIDTH, s_), BF16),
        jax.ShapeDtypeStruct((b_, SWA_KV_HEADS, s_, HEAD_DIM), BF16),
        jax.ShapeDtypeStruct((b_, SWA_KV_HEADS, SWA_VT_ROWS, s_), BF16),
        jax.ShapeDtypeStruct((b_, MLA_HEADS * MLA_PAD, s_), BF16),
        jax.ShapeDtypeStruct((b_, MLA_HEADS, s_, MLA_PAD), BF16),
        jax.ShapeDtypeStruct((b_, MLA_HEADS, s_ // chunk, MLA_VT_ROWS, chunk), BF16),
    )
    out_specs = (
        pl.BlockSpec((1, tm, FOURIER_WIDTH), lambda b, i: (b, i, 0)),
        pl.BlockSpec((1, tm, FOURIER_WIDTH), lambda b, i: (b, i, 0)),
        pl.BlockSpec((1, SWA_Q_WIDTH, tm), lambda b, i: (b, 0, i)),
        pl.BlockSpec((1, SWA_KV_HEADS, tm, HEAD_DIM), lambda b, i: (b, 0, i, 0)),
        pl.BlockSpec((1, SWA_KV_HEADS, SWA_VT_ROWS, tm), lambda b, i: (b, 0, 0, i)),
        pl.BlockSpec((1, MLA_HEADS * MLA_PAD, tm), lambda b, i: (b, 0, i)),
        pl.BlockSpec((1, MLA_HEADS, tm, MLA_PAD), lambda b, i: (b, 0, i, 0)),
        pl.BlockSpec((1, MLA_HEADS, tm // chunk, MLA_VT_ROWS, chunk), lambda b, i: (b, 0, i, 0, 0)),
    )
    return pl.pallas_call(
        _proj_kernel,
        grid=(b_, s_ // tm),
        in_specs=[
            pl.BlockSpec((1, tm, D_MODEL), lambda b, i: (b, i, 0)),
            pl.BlockSpec((1, 1, 6 * D_MODEL), mod_map),
            pl.BlockSpec((1, D_MODEL), const),
            pl.BlockSpec((D_MODEL, W1_COLS), const),
            pl.BlockSpec((W1T_ROWS, D_MODEL), const),
            pl.BlockSpec((tm, 4 * LANE), lambda b, i: (i, 0)),
            pl.BlockSpec((4 * LANE, tm), lambda b, i: (0, i)),
            pl.BlockSpec((FOURIER_WIDTH, 2 * FOURIER_WIDTH), const),
            pl.BlockSpec((1, MLA_Q_RANK), const),
            pl.BlockSpec((MLA_HEADS * MLA_PAD, MLA_Q_RANK), const),
            pl.BlockSpec((1, MLA_KV_RANK), const),
            pl.BlockSpec((MLA_KV_RANK, MLA_HEADS * MLA_PAD), const),
            pl.BlockSpec((MLA_HEADS * MLA_V_DIM, MLA_KV_RANK), const),
        ],
        out_specs=out_specs,
        out_shape=out_shapes,
        compiler_params=_cparams(("parallel", "parallel")),
        name="norm_in_proj",
    )(h, mod3, g1, w1, w1t, rope, ropet, wcs, qn, wuq, kvn, wukvk, wukvv)


def _dft_kernel(c_ref, s_ref, u_ref, v_ref, o_ref):
    o = jnp.dot(c_ref[...], u_ref[0].astype(BF16), preferred_element_type=F32)
    o = o + jnp.dot(s_ref[...], v_ref[0].astype(BF16), preferred_element_type=F32)
    o_ref[0] = o


def _dft_call(ctab, stab, u, v, tk):
    b_, s_, _ = u.shape
    whole = pl.BlockSpec((1, s_, FOURIER_WIDTH), lambda i, b: (b, 0, 0))
    return pl.pallas_call(
        _dft_kernel,
        grid=(s_ // tk, b_),
        in_specs=[
            pl.BlockSpec((tk, s_), lambda i, b: (i, 0)),
            pl.BlockSpec((tk, s_), lambda i, b: (i, 0)),
            whole, whole,
        ],
        out_specs=pl.BlockSpec((1, tk, FOURIER_WIDTH), lambda i, b: (b, i, 0)),
        out_shape=jax.ShapeDtypeStruct((b_, s_, FOURIER_WIDTH), F32),
        compiler_params=_cparams(("parallel", "arbitrary")),
        name="fourier_dft",
    )(ctab, stab, u, v)


FFT_SUB = 8
FFT_STEP = 2


def _fft1_kernel(m1_ref, m2_ref, tc_ref, ts_ref, u_ref, v_ref, bre_ref, bim_ref):
    n1, _, w = u_ref.shape[1:]
    sub = FFT_SUB
    for h in range(u_ref.shape[2] // sub):
        js = slice(sub * h, sub * (h + 1))
        xu = u_ref[0, :, js, :].reshape(n1 * sub, w).astype(BF16)
        xv = v_ref[0, :, js, :].reshape(n1 * sub, w).astype(BF16)
        a = jnp.dot(m1_ref[...], xu, preferred_element_type=F32)
        a = a + jnp.dot(m2_ref[...], xv, preferred_element_type=F32)
        are, aim = a[0:n1 * sub], a[n1 * sub:]
        tc, ts = tc_ref[h], ts_ref[h]
        bre_ref[0, :, js, :] = (are * tc + aim * ts).reshape(n1, sub, w)
        bim_ref[0, :, js, :] = (aim * tc - are * ts).reshape(n1, sub, w)


def _fft2_kernel(g1_ref, g2_ref, bre_ref, bim_ref, o_ref):
    n2 = g1_ref.shape[1]
    for j in range(o_ref.shape[2]):
        rows = slice(n2 * j, n2 * (j + 1))
        x = jnp.dot(g1_ref[...], bre_ref[0, rows, :].astype(BF16), preferred_element_type=F32)
        x = x + jnp.dot(g2_ref[...], bim_ref[0, rows, :].astype(BF16), preferred_element_type=F32)
        o_ref[0, :, j, :] = x


def _fft_call(tabs, u, v):
    m1, m2, tc, ts, g1, g2 = tabs
    b_, s_, w = u.shape
    n1 = m1.shape[1] // FFT_SUB
    n2 = s_ // n1
    u4 = u.reshape(b_, n1, n2, w)
    v4 = v.reshape(b_, n1, n2, w)
    per_step = FFT_STEP * FFT_SUB
    tile = pl.BlockSpec((1, n1, per_step, w), lambda t, b: (b, 0, t, 0))
    tab = pl.BlockSpec((FFT_STEP, n1 * FFT_SUB, w), lambda t, b: (t, 0, 0))
    mat = pl.BlockSpec((2 * n1 * FFT_SUB, n1 * FFT_SUB), lambda t, b: (0, 0))
    bre, bim = pl.pallas_call(
        _fft1_kernel,
        grid=(n2 // per_step, b_),
        in_specs=[mat, mat, tab, tab, tile, tile],
        out_specs=(tile, tile),
        out_shape=(jax.ShapeDtypeStruct((b_, n1, n2, w), F32),) * 2,
        compiler_params=_cparams(("parallel", "arbitrary")),
        name="fourier_fft1",
    )(m1, m2, tc, ts, u4, v4)
    blk = pl.BlockSpec((1, per_step * n2, w), lambda b, t: (b, t, 0))
    mat2 = pl.BlockSpec((n2, n2), lambda b, t: (0, 0))
    out = pl.pallas_call(
        _fft2_kernel,
        grid=(b_, n1 // per_step),
        in_specs=[mat2, mat2, blk, blk],
        out_specs=pl.BlockSpec((1, n2, per_step, w), lambda b, t: (b, 0, t, 0)),
        out_shape=jax.ShapeDtypeStruct((b_, n2, n1, w), F32),
        compiler_params=_cparams(("parallel", "parallel")),
        name="fourier_fft2",
    )(g1, g2, bre.reshape(b_, s_, w), bim.reshape(b_, s_, w))
    return out.reshape(b_, s_, w)


def _fft_tables(s_, n1, w):
    n2 = s_ // n1
    k1 = np.arange(n1)
    a1 = ((k1[:, None] * k1[None, :]) % n1).astype(np.float64) * (2.0 * math.pi / n1)
    c1, s1 = np.cos(a1), np.sin(a1)
    eye = np.eye(FFT_SUB)
    m1 = jnp.asarray(np.kron(np.concatenate([c1, -s1], axis=0), eye), dtype=BF16)
    m2 = jnp.asarray(np.kron(np.concatenate([-s1, -c1], axis=0), eye), dtype=BF16)
    s2 = jnp.arange(n2, dtype=jnp.int32)
    n_t = n2 // FFT_SUB
    row = jnp.arange(n1 * FFT_SUB, dtype=jnp.int32)
    s2_of = FFT_SUB * jnp.arange(n_t, dtype=jnp.int32)[:, None] + (row % FFT_SUB)[None, :]
    th = (((row // FFT_SUB)[None, :] * s2_of) % s_).astype(F32) * (2.0 * math.pi / s_)
    tc = jnp.broadcast_to(jnp.cos(th)[:, :, None], (n_t, n1 * FFT_SUB, w))
    ts = jnp.broadcast_to(jnp.sin(th)[:, :, None], (n_t, n1 * FFT_SUB, w))
    a2 = ((s2[:, None] * s2[None, :]) % n2).astype(F32) * (2.0 * math.pi / n2)
    scale = 1.0 / math.sqrt(s_)
    g1 = (jnp.cos(a2) * scale).astype(BF16)
    g2 = (jnp.sin(a2) * scale).astype(BF16)
    return m1, m2, tc, ts, g1, g2


def _mla_kernel(*refs, n_chunks, heads_per_step):
    hps = heads_per_step
    n_in = 6 if n_chunks else 4
    if n_chunks:
        zero_ref, qt_ref, k_ref, vt_ref, kc_ref, vct_ref = refs[:n_in]
    else:
        zero_ref, qt_ref, kc_ref, vct_ref = refs[:n_in]
    o_ref = refs[n_in]
    row0 = pl.multiple_of(zero_ref[0], MLA_TK)
    scratch = refs[n_in + 1:]
    tq = scratch[0].shape[1]
    n_sub = qt_ref.shape[2] // tq
    lc = kc_ref.shape[2]
    s_refs = [[scratch[MLA_S_SLOTS * (hps * u + hh):MLA_S_SLOTS * (hps * u + hh + 1)] for hh in range(hps)]
              for u in range(n_sub)]

    def keys(idx):
        if idx == 0:
            return (lambda hh: kc_ref[0, hh]), lc
        return (lambda hh: k_ref[0, hh, (idx - 1) * MLA_TK:idx * MLA_TK, :]), MLA_TK

    def values(idx):
        if idx == 0:
            return lambda hh: vct_ref[0, hh, 0]
        return lambda hh: vt_ref[0, hh, idx - 1]

    def logits(u, idx):
        ks, rows = keys(idx)
        cms = []
        for hh in range(hps):
            qt = qt_ref[0, MLA_PAD * hh:MLA_PAD * (hh + 1), tq * u:tq * (u + 1)]
            s = jnp.dot(ks(hh), qt, preferred_element_type=F32)
            s_refs[u][hh][idx % MLA_S_SLOTS][0:rows, :] = s
            cms.append(jnp.max(s, axis=0, keepdims=True))
        return tuple(cms)

    def accumulate(u, idx, carries, cms):
        _, rows = keys(idx)
        vts = values(idx)
        out = []
        for hh in range(hps):
            m, acc = carries[hh]
            m_new = jnp.maximum(m, cms[hh])
            alpha = jnp.exp2(m - m_new)
            s = s_refs[u][hh][idx % MLA_S_SLOTS][pl.ds(row0, rows), :]
            p = jnp.exp2(s - m_new).astype(BF16)
            acc = alpha * acc + jnp.dot(vts(hh), p, preferred_element_type=F32)
            out.append((m_new, acc))
        return tuple(out)

    init = (jnp.full((1, tq), NEG_INF, F32), jnp.zeros((MLA_VT_ROWS, tq), F32))
    carries = [tuple(init for _ in range(hps)) for _ in range(n_sub)]
    pending = [None] * n_sub
    n_idx = n_chunks + 1
    for t in range(n_idx + n_sub):
        for u in range(n_sub):
            idx = t - u
            new_cms = logits(u, idx) if 0 <= idx < n_idx else None
            if 1 <= idx <= n_idx:
                carries[u] = accumulate(u, idx - 1, carries[u], pending[u])
            pending[u] = new_cms
    for u in range(n_sub):
        o = jnp.concatenate([acc[0:MLA_V_DIM] / acc[MLA_V_DIM:MLA_V_DIM + 1] for (_, acc) in carries[u]],
                            axis=0)
        o_ref[0, tq * u:tq * (u + 1), :] = o.T.astype(o_ref.dtype)


def _mla_call(qt, k, vt, kc, vct, tq, n_sub):
    b_, _, sq = qt.shape
    h_ = kc.shape[1]
    lc = kc.shape[2]
    hps = 2
    n_chunks = 0 if k is None else k.shape[2] // MLA_TK
    assert n_chunks == 0 or lc <= MLA_TK
    in_specs = [pl.BlockSpec(memory_space=pltpu.SMEM),
                pl.BlockSpec((1, hps * MLA_PAD, tq), lambda b, p, i: (b, p, i))]
    args = [jnp.zeros((1,), jnp.int32), qt]
    if k is not None:
        sk = k.shape[2]
        in_specs += [pl.BlockSpec((1, hps, sk, MLA_PAD), lambda b, p, i: (b, p, 0, 0)),
                     pl.BlockSpec((1, hps, n_chunks, MLA_VT_ROWS, MLA_TK), lambda b, p, i: (b, p, 0, 0, 0))]
        args += [k, vt]
    in_specs += [pl.BlockSpec((1, hps, lc, MLA_PAD), lambda b, p, i: (b, p, 0, 0)),
                 pl.BlockSpec((1, hps, 1, MLA_VT_ROWS, lc), lambda b, p, i: (b, p, 0, 0, 0))]
    args += [kc, vct]
    return pl.pallas_call(
        functools.partial(_mla_kernel, n_chunks=n_chunks, heads_per_step=hps),
        grid=(b_, h_ // hps, sq // tq),
        in_specs=in_specs,
        out_specs=pl.BlockSpec((1, tq, hps * MLA_V_DIM), lambda b, p, i: (b, i, p)),
        out_shape=jax.ShapeDtypeStruct((b_, sq, h_ * MLA_V_DIM), BF16),
        scratch_shapes=[pltpu.VMEM((MLA_TK if n_chunks else lc, tq // n_sub), F32)] * (MLA_S_SLOTS * hps * n_sub),
        compiler_params=_cparams(("parallel", "parallel", "arbitrary")),
        name="mla_attn",
    )(*args)


def _swa_kernel(*refs, n_local, s_len):
    zero_ref, sink_ref, qt_ref = refs[:3]
    k_refs = refs[3:3 + n_local]
    vt_refs = refs[3 + n_local:3 + 2 * n_local]
    kc_ref, vct_ref, o_ref = refs[3 + 2 * n_local:6 + 2 * n_local]
    s_refs = refs[6 + 2 * n_local:]
    row0 = pl.multiple_of(zero_ref[0], BLOCK)
    tq = s_refs[0].shape[1]
    step_tq = qt_ref.shape[2]
    n_sub = step_tq // tq
    lc = kc_ref.shape[2]
    sub_blocks = tq // BLOCK + 2 if n_local else 0
    nk = sub_blocks * BLOCK
    sinks = [sink_ref[hh] * LOG2E for hh in range(SWA_HEADS)]

    def mask_bias(u):
        q0 = pl.program_id(1) * step_tq + u * tq
        kpos = q0 - BLOCK + lax.broadcasted_iota(jnp.int32, (nk, tq), 0)
        qpos = q0 + lax.broadcasted_iota(jnp.int32, (nk, tq), 1)
        valid = (jnp.abs(kpos - qpos) <= WINDOW) & (kpos >= 0) & (kpos < s_len)
        return jnp.where(valid, 0.0, NEG_INF).astype(F32)

    def keys(u, kh):
        kc = kc_ref[0, kh]
        vct = vct_ref[0, kh]
        if not n_local:
            return kc, vct
        first = u * (tq // BLOCK)
        blocks = range(first, first + sub_blocks)
        return (jnp.concatenate([kc] + [k_refs[j][0, kh] for j in blocks], axis=0),
                jnp.concatenate([vct] + [vt_refs[j][0, kh] for j in blocks], axis=1))

    def logits(unit, u, hh, k_all, bias):
        qt = qt_ref[0, HEAD_DIM * hh:HEAD_DIM * (hh + 1), tq * u:tq * (u + 1)]
        s_ref = s_refs[unit % 2]
        s_c = jnp.dot(k_all[0:lc], qt, preferred_element_type=F32)
        s_ref[0:lc, :] = s_c
        m = jnp.maximum(jnp.max(s_c, axis=0, keepdims=True), sinks[hh])
        if n_local:
            s_l = jnp.dot(k_all[lc:], qt, preferred_element_type=F32) + bias
            s_ref[lc:lc + nk, :] = s_l
            m = jnp.maximum(m, jnp.max(s_l, axis=0, keepdims=True))
        return m

    def accumulate(unit, hh, vt_all, m):
        p = jnp.exp2(s_refs[unit % 2][pl.ds(row0, lc + nk), :] - m).astype(BF16)
        acc = jnp.dot(vt_all, p, preferred_element_type=F32)
        l = acc[HEAD_DIM:HEAD_DIM + 1] + jnp.exp2(sinks[hh] - m)
        return acc[0:HEAD_DIM] / l

    units = [(u, hh) for u in range(n_sub) for hh in range(SWA_HEADS)]
    kv = {(u, kh): keys(u, kh) for u in range(n_sub) for kh in range(SWA_KV_HEADS)}
    biases = [mask_bias(u) if n_local else None for u in range(n_sub)]
    outs = [[] for _ in range(n_sub)]
    m_prev = None
    for unit in range(len(units) + 1):
        m_next = None
        if unit < len(units):
            u, hh = units[unit]
            m_next = logits(unit, u, hh, kv[(u, hh // SWA_GROUP)][0], biases[u])
        if unit > 0:
            u, hh = units[unit - 1]
            outs[u].append(accumulate(unit - 1, hh, kv[(u, hh // SWA_GROUP)][1], m_prev))
        m_prev = m_next
    for u in range(n_sub):
        o = jnp.concatenate(outs[u], axis=0)
        o_ref[0, tq * u:tq * (u + 1), :] = o.T.astype(o_ref.dtype)


def _swa_call(sink, qt, k, vt, kc, vct):
    b_, _, sq = qt.shape
    lc = kc.shape[2]
    local = k is not None
    sub_tq = SWA_TQ if local else sq
    tq = sub_tq * SWA_SUB_TILES if local else sq
    nb = sq // tq
    n_local = tq // BLOCK + 2 if local else 0
    sub_keys = lc + (sub_tq // BLOCK + 2) * BLOCK if local else lc
    in_specs = [pl.BlockSpec(memory_space=pltpu.SMEM), pl.BlockSpec(memory_space=pltpu.SMEM),
                pl.BlockSpec((1, SWA_Q_WIDTH, tq), lambda b, n: (b, 0, n))]
    args = [jnp.zeros((1,), jnp.int32), sink, qt]
    if local:
        nkb = sq // BLOCK
        per = tq // BLOCK

        def blk_idx(n, j):
            return jnp.clip(n * per - 1 + j, 0, nkb - 1)

        in_specs += [pl.BlockSpec((1, SWA_KV_HEADS, BLOCK, HEAD_DIM),
                                  functools.partial(lambda b, n, j: (b, 0, blk_idx(n, j), 0), j=j))
                     for j in range(n_local)]
        in_specs += [pl.BlockSpec((1, SWA_KV_HEADS, SWA_VT_ROWS, BLOCK),
                                  functools.partial(lambda b, n, j: (b, 0, 0, blk_idx(n, j)), j=j))
                     for j in range(n_local)]
        args += [k] * n_local + [vt] * n_local
    in_specs += [pl.BlockSpec((1, SWA_KV_HEADS, lc, HEAD_DIM), lambda b, n: (b, 0, 0, 0)),
                 pl.BlockSpec((1, SWA_KV_HEADS, SWA_VT_ROWS, lc), lambda b, n: (b, 0, 0, 0))]
    args += [kc, vct]
    return pl.pallas_call(
        functools.partial(_swa_kernel, n_local=n_local, s_len=sq),
        grid=(b_, nb),
        in_specs=in_specs,
        out_specs=pl.BlockSpec((1, tq, SWA_Q_WIDTH), lambda b, n: (b, n, 0)),
        out_shape=jax.ShapeDtypeStruct((b_, sq, SWA_Q_WIDTH), BF16),
        scratch_shapes=[pltpu.VMEM((sub_keys, sub_tq), F32)] * 2,
        compiler_params=_cparams(("parallel", "arbitrary")),
        name="swa_attn",
    )(*args)


def _mlp_kernel(fo_ref, so_ref, mo_ref, h_ref, mod_ref, g2_ref, wo_ref, w1_ref, w2_ref, gf_ref, o_ref, *,
                final, ff_chunk):
    mod = mod_ref[0]
    g1 = mod[:, 2 * D_MODEL:3 * D_MODEL]
    sh2 = mod[:, 3 * D_MODEL:4 * D_MODEL]
    sc2 = mod[:, 4 * D_MODEL:5 * D_MODEL]
    g2 = mod[:, 5 * D_MODEL:6 * D_MODEL]
    mix = jnp.concatenate([fo_ref[0].astype(BF16), so_ref[0], mo_ref[0]], axis=-1)
    h = h_ref[0] + g1 * jnp.dot(mix, wo_ref[...], preferred_element_type=F32)
    y = (_rms(h, g2_ref[...]) * (1.0 + sc2) + sh2).astype(BF16)
    acc = None
    for c in range(D_FF // ff_chunk):
        a = jnp.dot(y, w1_ref[:, c * ff_chunk:(c + 1) * ff_chunk], preferred_element_type=F32)
        a = jnp.maximum(a, 0.0)
        a = (a * a).astype(BF16)
        part = jnp.dot(a, w2_ref[c * ff_chunk:(c + 1) * ff_chunk, :], preferred_element_type=F32)
        acc = part if acc is None else acc + part
    h = h + g2 * acc
    if final:
        h = _rms(h, gf_ref[...])
    o_ref[0] = h


def _mlp_call(fo, so, mo, h, mod3, mod_row, g2, wo, w1, w2, gf, tm, final):
    b_, s_, _ = h.shape
    const = lambda b, i: (0, 0)
    tile = lambda b, i: (b, i, 0)
    if mod_row is None:
        mod_map = lambda b, i: (b, 0, 0)
    else:
        mod_map = lambda b, i: (mod_row, 0, 0)
    once = pl.Buffered(1)
    return pl.pallas_call(
        functools.partial(_mlp_kernel, final=final, ff_chunk=1024),
        grid=(b_, s_ // tm),
        in_specs=[
            pl.BlockSpec((1, tm, FOURIER_WIDTH), tile),
            pl.BlockSpec((1, tm, SWA_Q_WIDTH), tile),
            pl.BlockSpec((1, tm, MLA_HEADS * MLA_V_DIM), tile),
            pl.BlockSpec((1, tm, D_MODEL), tile),
            pl.BlockSpec((1, 1, 6 * D_MODEL), mod_map),
            pl.BlockSpec((1, D_MODEL), const),
            pl.BlockSpec((D_MIX, D_MODEL), const, pipeline_mode=once),
            pl.BlockSpec((D_MODEL, D_FF), const, pipeline_mode=once),
            pl.BlockSpec((D_FF, D_MODEL), const, pipeline_mode=once),
            pl.BlockSpec((1, D_MODEL), const),
        ],
        out_specs=pl.BlockSpec((1, tm, D_MODEL), tile),
        out_shape=jax.ShapeDtypeStruct((b_, s_, D_MODEL), F32),
        compiler_params=_cparams(("parallel", "parallel")),
        name="out_proj_mlp",
    )(fo, so, mo, h, mod3, g2, wo, w1, w2, gf)


def _rot_cols(w, heads, dim):
    w4 = w.reshape(w.shape[0], heads, 2, dim // 2)
    return jnp.stack([-w4[:, :, 1], w4[:, :, 0]], axis=2).reshape(w.shape[0], heads * dim)


def _pad_lanes(w, heads, before, after):
    w3 = w.reshape(w.shape[0], heads, w.shape[1] // heads)
    w3 = jnp.pad(w3, ((0, 0), (0, 0), (before, after)))
    return w3.reshape(w.shape[0], -1)


def _pack_w_in(w_in):
    wk = w_in[:, OFF_SWA_K:OFF_SWA_V]
    wkr = w_in[:, OFF_MLA_KR:OFF_MLA_KR + MLA_ROPE_DIM]
    pad_kr = lambda w: _pad_lanes(w, 1, MLA_NOPE_DIM, LANE - MLA_NOPE_DIM - MLA_ROPE_DIM)
    cols = [w_in[:, :OFF_SWA_Q], wk, w_in[:, OFF_MLA_CQ:OFF_MLA_KR], pad_kr(wkr),
            _rot_cols(wk, SWA_KV_HEADS, HEAD_DIM), pad_kr(_rot_cols(wkr, 1, MLA_ROPE_DIM))]
    rows = w_in[:, OFF_SWA_Q:OFF_SWA_K], w_in[:, OFF_SWA_V:OFF_MLA_CQ]
    return jnp.concatenate(cols, axis=1).astype(BF16), jnp.concatenate(rows, axis=1).T.astype(BF16)


def _pack_w_uq(w_uq):
    per = MLA_NOPE_DIM + MLA_ROPE_DIM
    return _pad_lanes(w_uq, MLA_HEADS, 0, MLA_PAD - per).T.astype(BF16)


def _pack_w_ukv(w_ukv):
    w3 = w_ukv.reshape(MLA_KV_RANK, MLA_HEADS, MLA_NOPE_DIM + MLA_V_DIM)
    wk = w3[:, :, :MLA_NOPE_DIM].reshape(MLA_KV_RANK, MLA_HEADS * MLA_NOPE_DIM)
    wv = w3[:, :, MLA_NOPE_DIM:].reshape(MLA_KV_RANK, MLA_HEADS * MLA_V_DIM)
    return _pad_lanes(wk, MLA_HEADS, 0, MLA_PAD - MLA_NOPE_DIM).astype(BF16), wv.T.astype(BF16)


def _rope_table(rows):
    r, col = jnp.meshgrid(jnp.arange(rows, dtype=F32), jnp.arange(GRID_W, dtype=F32), indexing="ij")
    r = r.reshape(-1)
    col = col.reshape(-1)

    def tables(dim):
        n_freq = dim // 4
        inv = ROPE_THETA ** (-jnp.arange(n_freq, dtype=F32) / n_freq)
        ang = jnp.concatenate([r[:, None] * inv[None, :], col[:, None] * inv[None, :]], axis=-1)
        return jnp.cos(ang), jnp.sin(ang)

    ch, sh = tables(HEAD_DIM)
    cr, sr = tables(MLA_ROPE_DIM)
    mla_pad = ((0, 0), (MLA_NOPE_DIM, LANE - MLA_NOPE_DIM - MLA_ROPE_DIM))
    nope_ones = (jnp.arange(LANE) < MLA_NOPE_DIM).astype(F32)[None, :]
    return jnp.concatenate([jnp.tile(ch, (1, 4)), jnp.tile(sh, (1, 4)),
                            jnp.pad(jnp.tile(cr, (1, 2)), mla_pad) + nope_ones,
                            jnp.pad(jnp.tile(sr, (1, 2)), mla_pad)], axis=-1)


def _identity_rope_table(n):
    ones = jnp.ones((n, LANE), F32)
    zeros = jnp.zeros((n, LANE), F32)
    m = jnp.concatenate([jnp.ones((n, MLA_NOPE_DIM + MLA_ROPE_DIM), F32),
                         jnp.zeros((n, LANE - MLA_NOPE_DIM - MLA_ROPE_DIM), F32)], axis=-1)
    return jnp.concatenate([ones, zeros, m, zeros], axis=-1)


def _dft_tables(n):
    k = jnp.arange(n, dtype=jnp.int32)
    ks = (k[:, None] * k[None, :]) % n
    ang = ks.astype(F32) * (2.0 * math.pi / n)
    scale = 1.0 / math.sqrt(n)
    return (jnp.cos(ang) * scale).astype(BF16), (-jnp.sin(ang) * scale).astype(BF16)


def kernel(x, c, ctx, c_ctx, w_ada, b_ada, norm1_g, norm2_g, w_in, w_fourier, swa_sink, mla_q_norm, w_uq,
           mla_kv_norm, w_ukv, w_out, w_mlp1, w_mlp2, final_norm_g):
    b_, s_, _ = x.shape
    lc = ctx.shape[1]
    depth = w_ada.shape[0]
    rows = s_ // GRID_W

    mod_rows = 16
    cc = jnp.concatenate([c, c_ctx[None, :], jnp.zeros((mod_rows - b_ - 1, D_MODEL), F32)], axis=0)
    mod_all = _ada_call(cc, w_ada, b_ada)

    rope = _rope_table(rows)
    rope_c = _identity_rope_table(lc)
    to_feature_major = lambda t: jnp.concatenate([t[:, 2 * LANE:], t[:, :2 * LANE]], axis=1).T
    ropet = to_feature_major(rope)
    ropet_c = to_feature_major(rope_c)
    fft_tabs = _fft_tables(s_, GRID_W, FOURIER_WIDTH)
    ctab_c, stab_c = _dft_tables(lc)
    kk = jnp.arange(FOURIER_GROUP_DIM, dtype=jnp.int32)
    ang64 = ((kk[:, None] * kk[None, :]) % FOURIER_GROUP_DIM).astype(F32) * (2.0 * math.pi / FOURIER_GROUP_DIM)
    c64, s64 = jnp.cos(ang64), jnp.sin(ang64)

    h, hc = x, ctx
    for l in range(depth):
        last = l == depth - 1
        mod3 = mod_all[l].reshape(mod_rows, 1, 6 * D_MODEL)
        g1 = norm1_g[l].reshape(1, D_MODEL)
        g2 = norm2_g[l].reshape(1, D_MODEL)
        gf = final_norm_g.reshape(1, D_MODEL)
        w1, w1t = _pack_w_in(w_in[l])
        wuq = _pack_w_uq(w_uq[l])
        wukvk, wukvv = _pack_w_ukv(w_ukv[l])
        wcs = _fw_call(c64, s64, w_fourier[l]).astype(BF16)
        qn = mla_q_norm[l].reshape(1, MLA_Q_RANK)
        kvn = mla_kv_norm[l].reshape(1, MLA_KV_RANK)
        wo = w_out[l].astype(BF16)
        wm1 = w_mlp1[l].astype(BF16)
        wm2 = w_mlp2[l].astype(BF16)
        sink = swa_sink[l]

        fu, fv, sq, sk, sv, mq, mk, mv = _proj_call(h, mod3, None, g1, w1, w1t, rope, ropet, wcs, qn, wuq,
                                                    kvn, wukvk, wukvv, tm=1024)
        fuc, fvc, sqc, skc, svc, mqc, mkc, mvc = _proj_call(hc, mod3, b_, g1, w1, w1t, rope_c, ropet_c, wcs, qn,
                                                            wuq, kvn, wukvk, wukvv, tm=lc)

        fo = _fft_call(fft_tabs, fu, fv)
        so = _swa_call(sink, sq, sk, sv, skc, svc)
        mo = _mla_call(mq, mk, mv, mkc, mvc, tq=1024, n_sub=2)
        h = _mlp_call(fo, so, mo, h, mod3, None, g2, wo, wm1, wm2, gf, tm=512, final=last)

        if not last:
            foc = _dft_call(ctab_c, stab_c, fuc, fvc, tk=lc)
            soc = _swa_call(sink, sqc, None, None, skc, svc)
            moc = _mla_call(mqc, None, None, mkc, mvc, tq=lc, n_sub=1)
            hc = _mlp_call(foc, soc, moc, hc, mod3, b_, g2, wo, wm1, wm2, gf, tm=lc, final=False)
    return h
```

```python
import functools
import math

import jax
import jax.numpy as jnp
import numpy as np
from jax import lax
from jax.experimental import pallas as pl
from jax.experimental.pallas import tpu as pltpu

D_MODEL = 1024
GRID_W = 64
HEAD_DIM = 64
FOURIER_GROUPS = 4
FOURIER_GROUP_DIM = 64
FOURIER_WIDTH = FOURIER_GROUPS * FOURIER_GROUP_DIM
SWA_HEADS = 6
SWA_KV_HEADS = 2
SWA_GROUP = SWA_HEADS // SWA_KV_HEADS
SWA_Q_WIDTH = SWA_HEADS * HEAD_DIM
SWA_KV_WIDTH = SWA_KV_HEADS * HEAD_DIM
WINDOW = 128
BLOCK = 128
MLA_HEADS = 6
MLA_NOPE_DIM = 64
MLA_ROPE_DIM = 32
MLA_V_DIM = 64
MLA_Q_RANK = 256
MLA_KV_RANK = 128
MLA_SCALE = (MLA_NOPE_DIM + MLA_ROPE_DIM) ** -0.5
SWA_SCALE = HEAD_DIM ** -0.5
D_MIX = FOURIER_WIDTH + SWA_Q_WIDTH + MLA_HEADS * MLA_V_DIM
OFF_SWA_Q = FOURIER_WIDTH
OFF_SWA_K = OFF_SWA_Q + SWA_Q_WIDTH
OFF_SWA_V = OFF_SWA_K + SWA_KV_WIDTH
OFF_MLA_CQ = OFF_SWA_V + SWA_KV_WIDTH
OFF_MLA_CKV = OFF_MLA_CQ + MLA_Q_RANK
OFF_MLA_KR = OFF_MLA_CKV + MLA_KV_RANK
D_FF = 4 * D_MODEL
ROPE_THETA = 10000.0
NORM_EPS = 1e-6
NEG_INF = -1e30

LANE = 128
MLA_PAD = LANE

C_F = 0
C_K = C_F + FOURIER_WIDTH
C_CQ = C_K + SWA_KV_WIDTH
C_CKV = C_CQ + MLA_Q_RANK
C_KR = C_CKV + MLA_KV_RANK
C_KROT = C_KR + LANE
C_KRROT = C_KROT + SWA_KV_WIDTH
W1_COLS = C_KRROT + LANE
R_Q = 0
R_V = R_Q + SWA_Q_WIDTH
W1T_ROWS = R_V + SWA_KV_WIDTH

BF16 = jnp.bfloat16
F32 = jnp.float32
LOG2E = math.log2(math.e)
_NT = (((1,), (1,)), ((), ()))
MLA_TK = 256
MLA_VT_ROWS = MLA_V_DIM + 16
MLA_S_SLOTS = 2
SWA_VT_ROWS = HEAD_DIM + 16
SWA_TQ = 256
SWA_SUB_TILES = 4

VMEM_LIMIT = 56 * 1024 * 1024


def _cparams(sem):
    return pltpu.CompilerParams(dimension_semantics=sem, vmem_limit_bytes=VMEM_LIMIT)


def _rms(x, g):
    ms = jnp.mean(x * x, axis=-1, keepdims=True)
    return x * lax.rsqrt(ms + NORM_EPS) * g


def _ada_kernel(cc_ref, w_ref, b_ref, o_ref):
    cc = cc_ref[...]
    s = cc * jax.nn.sigmoid(cc)
    o_ref[0] = jnp.dot(s, w_ref[0], preferred_element_type=F32,
                       precision=lax.Precision.HIGHEST) + b_ref[0]


def _ada_call(cc, w_ada, b_ada):
    depth = w_ada.shape[0]
    rows = cc.shape[0]
    tn = 1536
    n_out = w_ada.shape[2]
    return pl.pallas_call(
        _ada_kernel,
        grid=(depth, n_out // tn),
        in_specs=[
            pl.BlockSpec((rows, D_MODEL), lambda l, j: (0, 0)),
            pl.BlockSpec((1, D_MODEL, tn), lambda l, j: (l, 0, j)),
            pl.BlockSpec((1, 1, tn), lambda l, j: (l, 0, j)),
        ],
        out_specs=pl.BlockSpec((1, rows, tn), lambda l, j: (l, 0, j)),
        out_shape=jax.ShapeDtypeStruct((depth, rows, n_out), F32),
        compiler_params=_cparams(("arbitrary", "arbitrary")),
        name="ada_mod",
    )(cc, w_ada, b_ada.reshape(depth, 1, n_out))


def _fw_kernel(c_ref, s_ref, w_ref, o_ref):
    o_ref[...] = jnp.zeros(o_ref.shape, o_ref.dtype)
    inv = 1.0 / math.sqrt(FOURIER_GROUP_DIM)
    for g in range(FOURIER_GROUPS):
        w = w_ref[g]
        a = jnp.dot(c_ref[...], w, preferred_element_type=F32, precision=lax.Precision.HIGHEST) * inv
        b = jnp.dot(s_ref[...], w, preferred_element_type=F32, precision=lax.Precision.HIGHEST) * inv
        lo = g * FOURIER_GROUP_DIM
        hi = lo + FOURIER_GROUP_DIM
        o_ref[lo:hi, lo:hi] = a
        o_ref[lo:hi, FOURIER_WIDTH + lo:FOURIER_WIDTH + hi] = b


def _fw_call(c64, s64, w_f):
    return pl.pallas_call(
        _fw_kernel,
        out_shape=jax.ShapeDtypeStruct((FOURIER_WIDTH, 2 * FOURIER_WIDTH), F32),
        name="fourier_w",
    )(c64, s64, w_f)


def _ones_row(rows, cols):
    return (lax.broadcasted_iota(jnp.int32, (rows, cols), 0) == 0).astype(BF16)


def _proj_kernel(h_ref, mod_ref, g1_ref, w1_ref, w1t_ref, rope_ref, ropet_ref, wcs_ref, qn_ref, wuq_ref,
                 kvn_ref, wukvk_ref, wukvv_ref,
                 u_o, v_o, sq_o, sk_o, sv_o, mq_o, mk_o, mv_o):
    x = h_ref[0]
    tm = x.shape[0]
    mod = mod_ref[0]
    sh = mod[:, 0:D_MODEL]
    sc = mod[:, D_MODEL:2 * D_MODEL]
    y = (_rms(x, g1_ref[...]) * (1.0 + sc) + sh).astype(BF16)
    u = jnp.dot(y, w1_ref[...], preferred_element_type=F32)
    ut = lax.dot_general(w1t_ref[...], y, _NT, preferred_element_type=F32)

    cos_h = rope_ref[:, 0:LANE]
    sin_h = rope_ref[:, LANE:2 * LANE]
    cos_m = rope_ref[:, 2 * LANE:3 * LANE]
    sin_m = rope_ref[:, 3 * LANE:4 * LANE]
    cos_mt = ropet_ref[0:LANE, :]
    sin_mt = ropet_ref[LANE:2 * LANE, :]
    cos_ht = ropet_ref[2 * LANE:3 * LANE, :]
    sin_ht = ropet_ref[3 * LANE:4 * LANE, :]

    f = u[:, C_F:C_F + FOURIER_WIDTH].astype(BF16)
    uv = jnp.dot(f, wcs_ref[...], preferred_element_type=F32)
    u_o[0] = uv[:, :FOURIER_WIDTH]
    v_o[0] = uv[:, FOURIER_WIDTH:]

    def rope_rows(x1, x2, c, s, scale):
        return ((x1 * c - x2 * s) * scale).astype(BF16), ((x2 * c + x1 * s) * scale).astype(BF16)

    half = HEAD_DIM // 2
    c_h, s_h = cos_ht[0:half], sin_ht[0:half]
    for hh in range(SWA_HEADS):
        r0 = R_Q + HEAD_DIM * hh
        lo, hi = rope_rows(ut[r0:r0 + half], ut[r0 + half:r0 + HEAD_DIM], c_h, s_h, SWA_SCALE * LOG2E)
        sq_o[0, HEAD_DIM * hh:HEAD_DIM * hh + half, :] = lo
        sq_o[0, HEAD_DIM * hh + half:HEAD_DIM * (hh + 1), :] = hi
    kk = (u[:, C_K:C_K + LANE] * cos_h + u[:, C_KROT:C_KROT + LANE] * sin_h).astype(BF16)
    ones_swa = _ones_row(SWA_VT_ROWS - HEAD_DIM, tm)
    for kh in range(SWA_KV_HEADS):
        sk_o[0, kh] = kk[:, HEAD_DIM * kh:HEAD_DIM * (kh + 1)]
        sv_o[0, kh, 0:HEAD_DIM, :] = ut[R_V + HEAD_DIM * kh:R_V + HEAD_DIM * (kh + 1)].astype(BF16)
        sv_o[0, kh, HEAD_DIM:SWA_VT_ROWS, :] = ones_swa

    cqn = _rms(u[:, C_CQ:C_CQ + MLA_Q_RANK], qn_ref[...]).astype(BF16)
    qa = lax.dot_general(wuq_ref[...], cqn, _NT, preferred_element_type=F32)
    rh = MLA_ROPE_DIM // 2
    c_m, s_m = cos_mt[MLA_NOPE_DIM:MLA_NOPE_DIM + rh], sin_mt[MLA_NOPE_DIM:MLA_NOPE_DIM + rh]
    q_scale = MLA_SCALE * LOG2E
    for hh in range(MLA_HEADS):
        r0 = MLA_PAD * hh
        r1 = r0 + MLA_NOPE_DIM
        lo, hi = rope_rows(qa[r1:r1 + rh], qa[r1 + rh:r1 + 2 * rh], c_m, s_m, q_scale)
        mq_o[0, r0:r1, :] = (qa[r0:r1] * q_scale).astype(BF16)
        mq_o[0, r1:r1 + rh, :] = lo
        mq_o[0, r1 + rh:r1 + 2 * rh, :] = hi
        mq_o[0, r1 + 2 * rh:r0 + MLA_PAD, :] = jnp.zeros((MLA_PAD - MLA_NOPE_DIM - 2 * rh, tm), BF16)

    ckvn = _rms(u[:, C_CKV:C_CKV + MLA_KV_RANK], kvn_ref[...]).astype(BF16)
    kr = u[:, C_KR:C_KR + LANE] * cos_m + u[:, C_KRROT:C_KRROT + LANE] * sin_m
    kvk = jnp.dot(ckvn, wukvk_ref[...], preferred_element_type=F32)
    vt = lax.dot_general(wukvv_ref[...], ckvn, _NT, preferred_element_type=F32).astype(BF16)
    chunk = mv_o.shape[4]
    ones_mla = _ones_row(MLA_VT_ROWS - MLA_V_DIM, chunk)
    for hh in range(MLA_HEADS):
        mk_o[0, hh] = (kvk[:, MLA_PAD * hh:MLA_PAD * (hh + 1)] + kr).astype(BF16)
        for c in range(tm // chunk):
            mv_o[0, hh, c, 0:MLA_V_DIM, :] = vt[MLA_V_DIM * hh:MLA_V_DIM * (hh + 1), chunk * c:chunk * (c + 1)]
            mv_o[0, hh, c, MLA_V_DIM:MLA_VT_ROWS, :] = ones_mla


def _proj_call(h, mod3, mod_row, g1, w1, w1t, rope, ropet, wcs, qn, wuq, kvn, wukvk, wukvv, tm):
    b_, s_, _ = h.shape
    chunk = min(MLA_TK, tm)
    const = lambda b, i: (0, 0)
    if mod_row is None:
        mod_map = lambda b, i: (b, 0, 0)
    else:
        mod_map = lambda b, i: (mod_row, 0, 0)
    out_shapes = (
        jax.ShapeDtypeStruct((b_, s_, FOURIER_WIDTH), F32),
        jax.ShapeDtypeStruct((b_, s_, FOURIER_WIDTH), F32),
        jax.ShapeDtypeStruct((b_, SWA_Q_WIDTH, s_), BF16),
        jax.ShapeDtypeStruct((b_, SWA_KV_HEADS, s_, HEAD_DIM), BF16),
        jax.ShapeDtypeStruct((b_, SWA_KV_HEADS, SWA_VT_ROWS, s_), BF16),
        jax.ShapeDtypeStruct((b_, MLA_HEADS * MLA_PAD, s_), BF16),
        jax.ShapeDtypeStruct((b_, MLA_HEADS, s_, MLA_PAD), BF16),
        jax.ShapeDtypeStruct((b_, MLA_HEADS, s_ // chunk, MLA_VT_ROWS, chunk), BF16),
    )
    out_specs = (
        pl.BlockSpec((1, tm, FOURIER_WIDTH), lambda b, i: (b, i, 0)),
        pl.BlockSpec((1, tm, FOURIER_WIDTH), lambda b, i: (b, i, 0)),
        pl.BlockSpec((1, SWA_Q_WIDTH, tm), lambda b, i: (b, 0, i)),
        pl.BlockSpec((1, SWA_KV_HEADS, tm, HEAD_DIM), lambda b, i: (b, 0, i, 0)),
        pl.BlockSpec((1, SWA_KV_HEADS, SWA_VT_ROWS, tm), lambda b, i: (b, 0, 0, i)),
        pl.BlockSpec((1, MLA_HEADS * MLA_PAD, tm), lambda b, i: (b, 0, i)),
        pl.BlockSpec((1, MLA_HEADS, tm, MLA_PAD), lambda b, i: (b, 0, i, 0)),
        pl.BlockSpec((1, MLA_HEADS, tm // chunk, MLA_VT_ROWS, chunk), lambda b, i: (b, 0, i, 0, 0)),
    )
    return pl.pallas_call(
        _proj_kernel,
        grid=(b_, s_ // tm),
        in_specs=[
            pl.BlockSpec((1, tm, D_MODEL), lambda b, i: (b, i, 0)),
            pl.BlockSpec((1, 1, 6 * D_MODEL), mod_map),
            pl.BlockSpec((1, D_MODEL), const),
            pl.BlockSpec((D_MODEL, W1_COLS), const),
            pl.BlockSpec((W1T_ROWS, D_MODEL), const),
            pl.BlockSpec((tm, 4 * LANE), lambda b, i: (i, 0)),
            pl.BlockSpec((4 * LANE, tm), lambda b, i: (0, i)),
            pl.BlockSpec((FOURIER_WIDTH, 2 * FOURIER_WIDTH), const),
            pl.BlockSpec((1, MLA_Q_RANK), const),
            pl.BlockSpec((MLA_HEADS * MLA_PAD, MLA_Q_RANK), const),
            pl.BlockSpec((1, MLA_KV_RANK), const),
            pl.BlockSpec((MLA_KV_RANK, MLA_HEADS * MLA_PAD), const),
            pl.BlockSpec((MLA_HEADS * MLA_V_DIM, MLA_KV_RANK), const),
        ],
        out_specs=out_specs,
        out_shape=out_shapes,
        compiler_params=_cparams(("parallel", "parallel")),
        name="norm_in_proj",
    )(h, mod3, g1, w1, w1t, rope, ropet, wcs, qn, wuq, kvn, wukvk, wukvv)


def _dft_kernel(c_ref, s_ref, u_ref, v_ref, o_ref):
    o = jnp.dot(c_ref[...], u_ref[0].astype(BF16), preferred_element_type=F32)
    o = o + jnp.dot(s_ref[...], v_ref[0].astype(BF16), preferred_element_type=F32)
    o_ref[0] = o


def _dft_call(ctab, stab, u, v, tk):
    b_, s_, _ = u.shape
    whole = pl.BlockSpec((1, s_, FOURIER_WIDTH), lambda i, b: (b, 0, 0))
    return pl.pallas_call(
        _dft_kernel,
        grid=(s_ // tk, b_),
        in_specs=[
            pl.BlockSpec((tk, s_), lambda i, b: (i, 0)),
            pl.BlockSpec((tk, s_), lambda i, b: (i, 0)),
            whole, whole,
        ],
        out_specs=pl.BlockSpec((1, tk, FOURIER_WIDTH), lambda i, b: (b, i, 0)),
        out_shape=jax.ShapeDtypeStruct((b_, s_, FOURIER_WIDTH), F32),
        compiler_params=_cparams(("parallel", "arbitrary")),
        name="fourier_dft",
    )(ctab, stab, u, v)


FFT_SUB = 8
FFT_STEP = 2


def _fft1_kernel(m1_ref, m2_ref, tc_ref, ts_ref, u_ref, v_ref, bre_ref, bim_ref):
    n1, _, w = u_ref.shape[1:]
    sub = FFT_SUB
    for h in range(u_ref.shape[2] // sub):
        js = slice(sub * h, sub * (h + 1))
        xu = u_ref[0, :, js, :].reshape(n1 * sub, w).astype(BF16)
        xv = v_ref[0, :, js, :].reshape(n1 * sub, w).astype(BF16)
        a = jnp.dot(m1_ref[...], xu, preferred_element_type=F32)
        a = a + jnp.dot(m2_ref[...], xv, preferred_element_type=F32)
        are, aim = a[0:n1 * sub], a[n1 * sub:]
        tc, ts = tc_ref[h], ts_ref[h]
        bre_ref[0, :, js, :] = (are * tc + aim * ts).reshape(n1, sub, w)
        bim_ref[0, :, js, :] = (aim * tc - are * ts).reshape(n1, sub, w)


def _fft2_kernel(g1_ref, g2_ref, bre_ref, bim_ref, o_ref):
    n2 = g1_ref.shape[1]
    for j in range(o_ref.shape[2]):
        rows = slice(n2 * j, n2 * (j + 1))
        x = jnp.dot(g1_ref[...], bre_ref[0, rows, :].astype(BF16), preferred_element_type=F32)
        x = x + jnp.dot(g2_ref[...], bim_ref[0, rows, :].astype(BF16), preferred_element_type=F32)
        o_ref[0, :, j, :] = x


def _fft_call(tabs, u, v):
    m1, m2, tc, ts, g1, g2 = tabs
    b_, s_, w = u.shape
    n1 = m1.shape[1] // FFT_SUB
    n2 = s_ // n1
    u4 = u.reshape(b_, n1, n2, w)
    v4 = v.reshape(b_, n1, n2, w)
    per_step = FFT_STEP * FFT_SUB
    tile = pl.BlockSpec((1, n1, per_step, w), lambda t, b: (b, 0, t, 0))
    tab = pl.BlockSpec((FFT_STEP, n1 * FFT_SUB, w), lambda t, b: (t, 0, 0))
    mat = pl.BlockSpec((2 * n1 * FFT_SUB, n1 * FFT_SUB), lambda t, b: (0, 0))
    bre, bim = pl.pallas_call(
        _fft1_kernel,
        grid=(n2 // per_step, b_),
        in_specs=[mat, mat, tab, tab, tile, tile],
        out_specs=(tile, tile),
        out_shape=(jax.ShapeDtypeStruct((b_, n1, n2, w), F32),) * 2,
        compiler_params=_cparams(("parallel", "arbitrary")),
        name="fourier_fft1",
    )(m1, m2, tc, ts, u4, v4)
    blk = pl.BlockSpec((1, per_step * n2, w), lambda b, t: (b, t, 0))
    mat2 = pl.BlockSpec((n2, n2), lambda b, t: (0, 0))
    out = pl.pallas_call(
        _fft2_kernel,
        grid=(b_, n1 // per_step),
        in_specs=[mat2, mat2, blk, blk],
        out_specs=pl.BlockSpec((1, n2, per_step, w), lambda b, t: (b, 0, t, 0)),
        out_shape=jax.ShapeDtypeStruct((b_, n2, n1, w), F32),
        compiler_params=_cparams(("parallel", "parallel")),
        name="fourier_fft2",
    )(g1, g2, bre.reshape(b_, s_, w), bim.reshape(b_, s_, w))
    return out.reshape(b_, s_, w)


def _fft_tables(s_, n1, w):
    n2 = s_ // n1
    k1 = np.arange(n1)
    a1 = ((k1[:, None] * k1[None, :]) % n1).astype(np.float64) * (2.0 * math.pi / n1)
    c1, s1 = np.cos(a1), np.sin(a1)
    eye = np.eye(FFT_SUB)
    expand = lambda a: jnp.asarray(np.kron(a, eye), dtype=F32).astype(BF16)
    m1 = expand(np.concatenate([c1, -s1], axis=0))
    m2 = expand(np.concatenate([-s1, -c1], axis=0))
    s2 = jnp.arange(n2, dtype=jnp.int32)
    n_t = n2 // FFT_SUB
    row = jnp.arange(n1 * FFT_SUB, dtype=jnp.int32)
    s2_of = FFT_SUB * jnp.arange(n_t, dtype=jnp.int32)[:, None] + (row % FFT_SUB)[None, :]
    th = (((row // FFT_SUB)[None, :] * s2_of) % s_).astype(F32) * (2.0 * math.pi / s_)
    tc = jnp.broadcast_to(jnp.cos(th)[:, :, None], (n_t, n1 * FFT_SUB, w))
    ts = jnp.broadcast_to(jnp.sin(th)[:, :, None], (n_t, n1 * FFT_SUB, w))
    a2 = ((s2[:, None] * s2[None, :]) % n2).astype(F32) * (2.0 * math.pi / n2)
    scale = 1.0 / math.sqrt(s_)
    g1 = (jnp.cos(a2) * scale).astype(BF16)
    g2 = (jnp.sin(a2) * scale).astype(BF16)
    return m1, m2, tc, ts, g1, g2


def _mla_kernel(*refs, n_chunks, heads_per_step):
    hps = heads_per_step
    n_in = 6 if n_chunks else 4
    if n_chunks:
        zero_ref, qt_ref, k_ref, vt_ref, kc_ref, vct_ref = refs[:n_in]
    else:
        zero_ref, qt_ref, kc_ref, vct_ref = refs[:n_in]
    o_ref = refs[n_in]
    row0 = pl.multiple_of(zero_ref[0], MLA_TK)
    scratch = refs[n_in + 1:]
    tq = scratch[0].shape[1]
    n_sub = qt_ref.shape[2] // tq
    lc = kc_ref.shape[2]
    s_refs = [[scratch[MLA_S_SLOTS * (hps * u + hh):MLA_S_SLOTS * (hps * u + hh + 1)] for hh in range(hps)]
              for u in range(n_sub)]

    def keys(idx):
        if idx == 0:
            return (lambda hh: kc_ref[0, hh]), lc
        return (lambda hh: k_ref[0, hh, (idx - 1) * MLA_TK:idx * MLA_TK, :]), MLA_TK

    def values(idx):
        if idx == 0:
            return lambda hh: vct_ref[0, hh, 0]
        return lambda hh: vt_ref[0, hh, idx - 1]

    def logits(u, idx):
        ks, rows = keys(idx)
        cms = []
        for hh in range(hps):
            qt = qt_ref[0, MLA_PAD * hh:MLA_PAD * (hh + 1), tq * u:tq * (u + 1)]
            s = jnp.dot(ks(hh), qt, preferred_element_type=F32)
            s_refs[u][hh][idx % MLA_S_SLOTS][0:rows, :] = s
            cms.append(jnp.max(s, axis=0, keepdims=True))
        return tuple(cms)

    def accumulate(u, idx, carries, cms):
        _, rows = keys(idx)
        vts = values(idx)
        out = []
        for hh in range(hps):
            m, acc = carries[hh]
            m_new = jnp.maximum(m, cms[hh])
            alpha = jnp.exp2(m - m_new)
            s = s_refs[u][hh][idx % MLA_S_SLOTS][pl.ds(row0, rows), :]
            p = jnp.exp2(s - m_new).astype(BF16)
            acc = alpha * acc + jnp.dot(vts(hh), p, preferred_element_type=F32)
            out.append((m_new, acc))
        return tuple(out)

    init = (jnp.full((1, tq), NEG_INF, F32), jnp.zeros((MLA_VT_ROWS, tq), F32))
    carries = [tuple(init for _ in range(hps)) for _ in range(n_sub)]
    pending = [None] * n_sub
    n_idx = n_chunks + 1
    for t in range(n_idx + n_sub):
        for u in range(n_sub):
            idx = t - u
            new_cms = logits(u, idx) if 0 <= idx < n_idx else None
            if 1 <= idx <= n_idx:
                carries[u] = accumulate(u, idx - 1, carries[u], pending[u])
            pending[u] = new_cms
    for u in range(n_sub):
        o = jnp.concatenate([acc[0:MLA_V_DIM] / acc[MLA_V_DIM:MLA_V_DIM + 1] for (_, acc) in carries[u]],
                            axis=0)
        o_ref[0, tq * u:tq * (u + 1), :] = o.T.astype(o_ref.dtype)


def _mla_call(qt, k, vt, kc, vct, tq, n_sub):
    b_, _, sq = qt.shape
    h_ = kc.shape[1]
    lc = kc.shape[2]
    hps = 2
    n_chunks = 0 if k is None else k.shape[2] // MLA_TK
    assert n_chunks == 0 or lc <= MLA_TK
    in_specs = [pl.BlockSpec(memory_space=pltpu.SMEM),
                pl.BlockSpec((1, hps * MLA_PAD, tq), lambda b, p, i: (b, p, i))]
    args = [jnp.zeros((1,), jnp.int32), qt]
    if k is not None:
        sk = k.shape[2]
        in_specs += [pl.BlockSpec((1, hps, sk, MLA_PAD), lambda b, p, i: (b, p, 0, 0)),
                     pl.BlockSpec((1, hps, n_chunks, MLA_VT_ROWS, MLA_TK), lambda b, p, i: (b, p, 0, 0, 0))]
        args += [k, vt]
    in_specs += [pl.BlockSpec((1, hps, lc, MLA_PAD), lambda b, p, i: (b, p, 0, 0)),
                 pl.BlockSpec((1, hps, 1, MLA_VT_ROWS, lc), lambda b, p, i: (b, p, 0, 0, 0))]
    args += [kc, vct]
    return pl.pallas_call(
        functools.partial(_mla_kernel, n_chunks=n_chunks, heads_per_step=hps),
        grid=(b_, h_ // hps, sq // tq),
        in_specs=in_specs,
        out_specs=pl.BlockSpec((1, tq, hps * MLA_V_DIM), lambda b, p, i: (b, i, p)),
        out_shape=jax.ShapeDtypeStruct((b_, sq, h_ * MLA_V_DIM), BF16),
        scratch_shapes=[pltpu.VMEM((MLA_TK if n_chunks else lc, tq // n_sub), F32)] * (MLA_S_SLOTS * hps * n_sub),
        compiler_params=_cparams(("parallel", "parallel", "arbitrary")),
        name="mla_attn",
    )(*args)


def _swa_kernel(*refs, n_local, s_len):
    zero_ref, sink_ref, qt_ref = refs[:3]
    k_refs = refs[3:3 + n_local]
    vt_refs = refs[3 + n_local:3 + 2 * n_local]
    kc_ref, vct_ref, o_ref = refs[3 + 2 * n_local:6 + 2 * n_local]
    s_refs = refs[6 + 2 * n_local:]
    row0 = pl.multiple_of(zero_ref[0], BLOCK)
    tq = s_refs[0].shape[1]
    step_tq = qt_ref.shape[2]
    n_sub = step_tq // tq
    lc = kc_ref.shape[2]
    sub_blocks = tq // BLOCK + 2 if n_local else 0
    nk = sub_blocks * BLOCK
    sinks = [sink_ref[hh] * LOG2E for hh in range(SWA_HEADS)]

    def mask_bias(u):
        q0 = pl.program_id(1) * step_tq + u * tq
        kpos = q0 - BLOCK + lax.broadcasted_iota(jnp.int32, (nk, tq), 0)
        qpos = q0 + lax.broadcasted_iota(jnp.int32, (nk, tq), 1)
        valid = (jnp.abs(kpos - qpos) <= WINDOW) & (kpos >= 0) & (kpos < s_len)
        return jnp.where(valid, 0.0, NEG_INF).astype(F32)

    def keys(u, kh):
        kc = kc_ref[0, kh]
        vct = vct_ref[0, kh]
        if not n_local:
            return kc, vct
        first = u * (tq // BLOCK)
        blocks = range(first, first + sub_blocks)
        return (jnp.concatenate([kc] + [k_refs[j][0, kh] for j in blocks], axis=0),
                jnp.concatenate([vct] + [vt_refs[j][0, kh] for j in blocks], axis=1))

    def logits(unit, u, hh, k_all, bias):
        qt = qt_ref[0, HEAD_DIM * hh:HEAD_DIM * (hh + 1), tq * u:tq * (u + 1)]
        s_ref = s_refs[unit % 2]
        s_c = jnp.dot(k_all[0:lc], qt, preferred_element_type=F32)
        s_ref[0:lc, :] = s_c
        m = jnp.maximum(jnp.max(s_c, axis=0, keepdims=True), sinks[hh])
        if n_local:
            s_l = jnp.dot(k_all[lc:], qt, preferred_element_type=F32) + bias
            s_ref[lc:lc + nk, :] = s_l
            m = jnp.maximum(m, jnp.max(s_l, axis=0, keepdims=True))
        return m

    def accumulate(unit, hh, vt_all, m):
        p = jnp.exp2(s_refs[unit % 2][pl.ds(row0, lc + nk), :] - m).astype(BF16)
        acc = jnp.dot(vt_all, p, preferred_element_type=F32)
        l = acc[HEAD_DIM:HEAD_DIM + 1] + jnp.exp2(sinks[hh] - m)
        return acc[0:HEAD_DIM] / l

    units = [(u, hh) for u in range(n_sub) for hh in range(SWA_HEADS)]
    kv = {(u, kh): keys(u, kh) for u in range(n_sub) for kh in range(SWA_KV_HEADS)}
    biases = [mask_bias(u) if n_local else None for u in range(n_sub)]
    outs = [[] for _ in range(n_sub)]
    m_prev = None
    for unit in range(len(units) + 1):
        m_next = None
        if unit < len(units):
            u, hh = units[unit]
            m_next = logits(unit, u, hh, kv[(u, hh // SWA_GROUP)][0], biases[u])
        if unit > 0:
            u, hh = units[unit - 1]
            outs[u].append(accumulate(unit - 1, hh, kv[(u, hh // SWA_GROUP)][1], m_prev))
        m_prev = m_next
    for u in range(n_sub):
        o = jnp.concatenate(outs[u], axis=0)
        o_ref[0, tq * u:tq * (u + 1), :] = o.T.astype(o_ref.dtype)


def _swa_call(sink, qt, k, vt, kc, vct):
    b_, _, sq = qt.shape
    lc = kc.shape[2]
    local = k is not None
    sub_tq = SWA_TQ if local else sq
    tq = sub_tq * SWA_SUB_TILES if local else sq
    nb = sq // tq
    n_local = tq // BLOCK + 2 if local else 0
    sub_keys = lc + (sub_tq // BLOCK + 2) * BLOCK if local else lc
    in_specs = [pl.BlockSpec(memory_space=pltpu.SMEM), pl.BlockSpec(memory_space=pltpu.SMEM),
                pl.BlockSpec((1, SWA_Q_WIDTH, tq), lambda b, n: (b, 0, n))]
    args = [jnp.zeros((1,), jnp.int32), sink, qt]
    if local:
        nkb = sq // BLOCK
        per = tq // BLOCK

        def blk_idx(n, j):
            return jnp.clip(n * per - 1 + j, 0, nkb - 1)

        in_specs += [pl.BlockSpec((1, SWA_KV_HEADS, BLOCK, HEAD_DIM),
                                  functools.partial(lambda b, n, j: (b, 0, blk_idx(n, j), 0), j=j))
                     for j in range(n_local)]
        in_specs += [pl.BlockSpec((1, SWA_KV_HEADS, SWA_VT_ROWS, BLOCK),
                                  functools.partial(lambda b, n, j: (b, 0, 0, blk_idx(n, j)), j=j))
                     for j in range(n_local)]
        args += [k] * n_local + [vt] * n_local
    in_specs += [pl.BlockSpec((1, SWA_KV_HEADS, lc, HEAD_DIM), lambda b, n: (b, 0, 0, 0)),
                 pl.BlockSpec((1, SWA_KV_HEADS, SWA_VT_ROWS, lc), lambda b, n: (b, 0, 0, 0))]
    args += [kc, vct]
    return pl.pallas_call(
        functools.partial(_swa_kernel, n_local=n_local, s_len=sq),
        grid=(b_, nb),
        in_specs=in_specs,
        out_specs=pl.BlockSpec((1, tq, SWA_Q_WIDTH), lambda b, n: (b, n, 0)),
        out_shape=jax.ShapeDtypeStruct((b_, sq, SWA_Q_WIDTH), BF16),
        scratch_shapes=[pltpu.VMEM((sub_keys, sub_tq), F32)] * 2,
        compiler_params=_cparams(("parallel", "arbitrary")),
        name="swa_attn",
    )(*args)


def _mlp_kernel(fo_ref, so_ref, mo_ref, h_ref, mod_ref, g2_ref, wo_ref, w1_ref, w2_ref, gf_ref, o_ref, *,
                final, ff_chunk):
    mod = mod_ref[0]
    g1 = mod[:, 2 * D_MODEL:3 * D_MODEL]
    sh2 = mod[:, 3 * D_MODEL:4 * D_MODEL]
    sc2 = mod[:, 4 * D_MODEL:5 * D_MODEL]
    g2 = mod[:, 5 * D_MODEL:6 * D_MODEL]
    mix = jnp.concatenate([fo_ref[0].astype(BF16), so_ref[0], mo_ref[0]], axis=-1)
    h = h_ref[0] + g1 * jnp.dot(mix, wo_ref[...], preferred_element_type=F32)
    y = (_rms(h, g2_ref[...]) * (1.0 + sc2) + sh2).astype(BF16)
    acc = None
    for c in range(D_FF // ff_chunk):
        a = jnp.dot(y, w1_ref[:, c * ff_chunk:(c + 1) * ff_chunk], preferred_element_type=F32)
        a = jnp.maximum(a, 0.0)
        a = (a * a).astype(BF16)
        part = jnp.dot(a, w2_ref[c * ff_chunk:(c + 1) * ff_chunk, :], preferred_element_type=F32)
        acc = part if acc is None else acc + part
    h = h + g2 * acc
    if final:
        h = _rms(h, gf_ref[...])
    o_ref[0] = h


def _mlp_call(fo, so, mo, h, mod3, mod_row, g2, wo, w1, w2, gf, tm, final):
    b_, s_, _ = h.shape
    const = lambda b, i: (0, 0)
    tile = lambda b, i: (b, i, 0)
    if mod_row is None:
        mod_map = lambda b, i: (b, 0, 0)
    else:
        mod_map = lambda b, i: (mod_row, 0, 0)
    once = pl.Buffered(1)
    return pl.pallas_call(
        functools.partial(_mlp_kernel, final=final, ff_chunk=1024),
        grid=(b_, s_ // tm),
        in_specs=[
            pl.BlockSpec((1, tm, FOURIER_WIDTH), tile),
            pl.BlockSpec((1, tm, SWA_Q_WIDTH), tile),
            pl.BlockSpec((1, tm, MLA_HEADS * MLA_V_DIM), tile),
            pl.BlockSpec((1, tm, D_MODEL), tile),
            pl.BlockSpec((1, 1, 6 * D_MODEL), mod_map),
            pl.BlockSpec((1, D_MODEL), const),
            pl.BlockSpec((D_MIX, D_MODEL), const, pipeline_mode=once),
            pl.BlockSpec((D_MODEL, D_FF), const, pipeline_mode=once),
            pl.BlockSpec((D_FF, D_MODEL), const, pipeline_mode=once),
            pl.BlockSpec((1, D_MODEL), const),
        ],
        out_specs=pl.BlockSpec((1, tm, D_MODEL), tile),
        out_shape=jax.ShapeDtypeStruct((b_, s_, D_MODEL), F32),
        compiler_params=_cparams(("parallel", "parallel")),
        name="out_proj_mlp",
    )(fo, so, mo, h, mod3, g2, wo, w1, w2, gf)


def _rot_cols(w, heads, dim):
    w4 = w.reshape(w.shape[0], heads, 2, dim // 2)
    return jnp.stack([-w4[:, :, 1], w4[:, :, 0]], axis=2).reshape(w.shape[0], heads * dim)


def _pad_lanes(w, heads, before, after):
    w3 = w.reshape(w.shape[0], heads, w.shape[1] // heads)
    w3 = jnp.pad(w3, ((0, 0), (0, 0), (before, after)))
    return w3.reshape(w.shape[0], -1)


def _pack_w_in(w_in):
    wk = w_in[:, OFF_SWA_K:OFF_SWA_V]
    wkr = w_in[:, OFF_MLA_KR:OFF_MLA_KR + MLA_ROPE_DIM]
    pad_kr = lambda w: _pad_lanes(w, 1, MLA_NOPE_DIM, LANE - MLA_NOPE_DIM - MLA_ROPE_DIM)
    cols = [w_in[:, :OFF_SWA_Q], wk, w_in[:, OFF_MLA_CQ:OFF_MLA_KR], pad_kr(wkr),
            _rot_cols(wk, SWA_KV_HEADS, HEAD_DIM), pad_kr(_rot_cols(wkr, 1, MLA_ROPE_DIM))]
    rows = w_in[:, OFF_SWA_Q:OFF_SWA_K], w_in[:, OFF_SWA_V:OFF_MLA_CQ]
    return jnp.concatenate(cols, axis=1).astype(BF16), jnp.concatenate(rows, axis=1).T.astype(BF16)


def _pack_w_uq(w_uq):
    per = MLA_NOPE_DIM + MLA_ROPE_DIM
    return _pad_lanes(w_uq, MLA_HEADS, 0, MLA_PAD - per).T.astype(BF16)


def _pack_w_ukv(w_ukv):
    w3 = w_ukv.reshape(MLA_KV_RANK, MLA_HEADS, MLA_NOPE_DIM + MLA_V_DIM)
    wk = w3[:, :, :MLA_NOPE_DIM].reshape(MLA_KV_RANK, MLA_HEADS * MLA_NOPE_DIM)
    wv = w3[:, :, MLA_NOPE_DIM:].reshape(MLA_KV_RANK, MLA_HEADS * MLA_V_DIM)
    return _pad_lanes(wk, MLA_HEADS, 0, MLA_PAD - MLA_NOPE_DIM).astype(BF16), wv.T.astype(BF16)


def _rope_table(rows):
    r, col = jnp.meshgrid(jnp.arange(rows, dtype=F32), jnp.arange(GRID_W, dtype=F32), indexing="ij")
    r = r.reshape(-1)
    col = col.reshape(-1)

    def tables(dim):
        n_freq = dim // 4
        inv = ROPE_THETA ** (-jnp.arange(n_freq, dtype=F32) / n_freq)
        ang = jnp.concatenate([r[:, None] * inv[None, :], col[:, None] * inv[None, :]], axis=-1)
        return jnp.cos(ang), jnp.sin(ang)

    ch, sh = tables(HEAD_DIM)
    cr, sr = tables(MLA_ROPE_DIM)
    mla_pad = ((0, 0), (MLA_NOPE_DIM, LANE - MLA_NOPE_DIM - MLA_ROPE_DIM))
    nope_ones = (jnp.arange(LANE) < MLA_NOPE_DIM).astype(F32)[None, :]
    return jnp.concatenate([jnp.tile(ch, (1, 4)), jnp.tile(sh, (1, 4)),
                            jnp.pad(jnp.tile(cr, (1, 2)), mla_pad) + nope_ones,
                            jnp.pad(jnp.tile(sr, (1, 2)), mla_pad)], axis=-1)


def _identity_rope_table(n):
    ones = jnp.ones((n, LANE), F32)
    zeros = jnp.zeros((n, LANE), F32)
    m = jnp.concatenate([jnp.ones((n, MLA_NOPE_DIM + MLA_ROPE_DIM), F32),
                         jnp.zeros((n, LANE - MLA_NOPE_DIM - MLA_ROPE_DIM), F32)], axis=-1)
    return jnp.concatenate([ones, zeros, m, zeros], axis=-1)


def _dft_tables(n):
    k = jnp.arange(n, dtype=jnp.int32)
    ks = (k[:, None] * k[None, :]) % n
    ang = ks.astype(F32) * (2.0 * math.pi / n)
    scale = 1.0 / math.sqrt(n)
    return (jnp.cos(ang) * scale).astype(BF16), (-jnp.sin(ang) * scale).astype(BF16)


def kernel(x, c, ctx, c_ctx, w_ada, b_ada, norm1_g, norm2_g, w_in, w_fourier, swa_sink, mla_q_norm, w_uq,
           mla_kv_norm, w_ukv, w_out, w_mlp1, w_mlp2, final_norm_g):
    b_, s_, _ = x.shape
    lc = ctx.shape[1]
    depth = w_ada.shape[0]
    rows = s_ // GRID_W

    mod_rows = 16
    cc = jnp.concatenate([c, c_ctx[None, :], jnp.zeros((mod_rows - b_ - 1, D_MODEL), F32)], axis=0)
    mod_all = _ada_call(cc, w_ada, b_ada)

    rope = _rope_table(rows)
    rope_c = _identity_rope_table(lc)
    to_feature_major = lambda t: jnp.concatenate([t[:, 2 * LANE:], t[:, :2 * LANE]], axis=1).T
    ropet = to_feature_major(rope)
    ropet_c = to_feature_major(rope_c)
    fft_tabs = _fft_tables(s_, GRID_W, FOURIER_WIDTH)
    ctab_c, stab_c = _dft_tables(lc)
    kk = jnp.arange(FOURIER_GROUP_DIM, dtype=jnp.int32)
    ang64 = ((kk[:, None] * kk[None, :]) % FOURIER_GROUP_DIM).astype(F32) * (2.0 * math.pi / FOURIER_GROUP_DIM)
    c64, s64 = jnp.cos(ang64), jnp.sin(ang64)

    h, hc = x, ctx
    for l in range(depth):
        last = l == depth - 1
        mod3 = mod_all[l].reshape(mod_rows, 1, 6 * D_MODEL)
        g1 = norm1_g[l].reshape(1, D_MODEL)
        g2 = norm2_g[l].reshape(1, D_MODEL)
        gf = final_norm_g.reshape(1, D_MODEL)
        w1, w1t = _pack_w_in(w_in[l])
        wuq = _pack_w_uq(w_uq[l])
        wukvk, wukvv = _pack_w_ukv(w_ukv[l])
        wcs = _fw_call(c64, s64, w_fourier[l]).astype(BF16)
        qn = mla_q_norm[l].reshape(1, MLA_Q_RANK)
        kvn = mla_kv_norm[l].reshape(1, MLA_KV_RANK)
        wo = w_out[l].astype(BF16)
        wm1 = w_mlp1[l].astype(BF16)
        wm2 = w_mlp2[l].astype(BF16)
        sink = swa_sink[l]

        fu, fv, sq, sk, sv, mq, mk, mv = _proj_call(h, mod3, None, g1, w1, w1t, rope, ropet, wcs, qn, wuq,
                                                    kvn, wukvk, wukvv, tm=1024)
        fuc, fvc, sqc, skc, svc, mqc, mkc, mvc = _proj_call(hc, mod3, b_, g1, w1, w1t, rope_c, ropet_c, wcs, qn,
                                                            wuq, kvn, wukvk, wukvv, tm=lc)

        fo = _fft_call(fft_tabs, fu, fv)
        so = _swa_call(sink, sq, sk, sv, skc, svc)
        mo = _mla_call(mq, mk, mv, mkc, mvc, tq=1024, n_sub=2)
        h = _mlp_call(fo, so, mo, h, mod3, None, g2, wo, wm1, wm2, gf, tm=512, final=last)

        if not last:
            foc = _dft_call(ctab_c, stab_c, fuc, fvc, tk=lc)
            soc = _swa_call(sink, sqc, None, None, skc, svc)
            moc = _mla_call(mqc, None, None, mkc, mvc, tq=lc, n_sub=1)
            hc = _mlp_call(foc, soc, moc, hc, mod3, b_, g2, wo, wm1, wm2, gf, tm=lc, final=False)
    return h
```

```python
import functools
import math

import jax
import jax.numpy as jnp
import numpy as np
from jax import lax
from jax.experimental import pallas as pl
from jax.experimental.pallas import tpu as pltpu

D_MODEL = 1024
GRID_W = 64
HEAD_DIM = 64
FOURIER_GROUPS = 4
FOURIER_GROUP_DIM = 64
FOURIER_WIDTH = FOURIER_GROUPS * FOURIER_GROUP_DIM
SWA_HEADS = 6
SWA_KV_HEADS = 2
SWA_GROUP = SWA_HEADS // SWA_KV_HEADS
SWA_Q_WIDTH = SWA_HEADS * HEAD_DIM
SWA_KV_WIDTH = SWA_KV_HEADS * HEAD_DIM
WINDOW = 128
BLOCK = 128
MLA_HEADS = 6
MLA_NOPE_DIM = 64
MLA_ROPE_DIM = 32
MLA_V_DIM = 64
MLA_Q_RANK = 256
MLA_KV_RANK = 128
MLA_SCALE = (MLA_NOPE_DIM + MLA_ROPE_DIM) ** -0.5
SWA_SCALE = HEAD_DIM ** -0.5
D_MIX = FOURIER_WIDTH + SWA_Q_WIDTH + MLA_HEADS * MLA_V_DIM
OFF_SWA_Q = FOURIER_WIDTH
OFF_SWA_K = OFF_SWA_Q + SWA_Q_WIDTH
OFF_SWA_V = OFF_SWA_K + SWA_KV_WIDTH
OFF_MLA_CQ = OFF_SWA_V + SWA_KV_WIDTH
OFF_MLA_CKV = OFF_MLA_CQ + MLA_Q_RANK
OFF_MLA_KR = OFF_MLA_CKV + MLA_KV_RANK
D_FF = 4 * D_MODEL
ROPE_THETA = 10000.0
NORM_EPS = 1e-6
NEG_INF = -1e30

LANE = 128
MLA_PAD = LANE

C_F = 0
C_K = C_F + FOURIER_WIDTH
C_CQ = C_K + SWA_KV_WIDTH
C_CKV = C_CQ + MLA_Q_RANK
C_KR = C_CKV + MLA_KV_RANK
C_KROT = C_KR + LANE
W1_COLS = C_KROT + SWA_KV_WIDTH
R_Q = 0
R_V = R_Q + SWA_Q_WIDTH
W1T_ROWS = R_V + SWA_KV_WIDTH

BF16 = jnp.bfloat16
F32 = jnp.float32
LOG2E = math.log2(math.e)
_NT = (((1,), (1,)), ((), ()))
MLA_TK = 256
MLA_VT_ROWS = MLA_V_DIM + 16
MLA_S_SLOTS = 2
SWA_VT_ROWS = HEAD_DIM + 16
SWA_TQ = 256
SWA_SUB_TILES = 4

VMEM_LIMIT = 56 * 1024 * 1024


def _cparams(sem):
    return pltpu.CompilerParams(dimension_semantics=sem, vmem_limit_bytes=VMEM_LIMIT)


def _rms(x, g):
    ms = jnp.mean(x * x, axis=-1, keepdims=True)
    return x * lax.rsqrt(ms + NORM_EPS) * g


def _ada_kernel(cc_ref, w_ref, b_ref, o_ref):
    cc = cc_ref[...]
    s = cc * jax.nn.sigmoid(cc)
    o_ref[0] = jnp.dot(s, w_ref[0], preferred_element_type=F32,
                       precision=lax.Precision.HIGHEST) + b_ref[0]


def _ada_call(cc, w_ada, b_ada):
    depth = w_ada.shape[0]
    rows = cc.shape[0]
    tn = 1536
    n_out = w_ada.shape[2]
    return pl.pallas_call(
        _ada_kernel,
        grid=(depth, n_out // tn),
        in_specs=[
            pl.BlockSpec((rows, D_MODEL), lambda l, j: (0, 0)),
            pl.BlockSpec((1, D_MODEL, tn), lambda l, j: (l, 0, j)),
            pl.BlockSpec((1, 1, tn), lambda l, j: (l, 0, j)),
        ],
        out_specs=pl.BlockSpec((1, rows, tn), lambda l, j: (l, 0, j)),
        out_shape=jax.ShapeDtypeStruct((depth, rows, n_out), F32),
        compiler_params=_cparams(("arbitrary", "arbitrary")),
        name="ada_mod",
    )(cc, w_ada, b_ada.reshape(depth, 1, n_out))


def _fw_kernel(c_ref, s_ref, w_ref, o_ref):
    o_ref[...] = jnp.zeros(o_ref.shape, o_ref.dtype)
    inv = 1.0 / math.sqrt(FOURIER_GROUP_DIM)
    for g in range(FOURIER_GROUPS):
        w = w_ref[g]
        a = jnp.dot(c_ref[...], w, preferred_element_type=F32, precision=lax.Precision.HIGHEST) * inv
        b = jnp.dot(s_ref[...], w, preferred_element_type=F32, precision=lax.Precision.HIGHEST) * inv
        lo = g * FOURIER_GROUP_DIM
        hi = lo + FOURIER_GROUP_DIM
        o_ref[lo:hi, lo:hi] = a
        o_ref[lo:hi, FOURIER_WIDTH + lo:FOURIER_WIDTH + hi] = b


def _fw_call(c64, s64, w_f):
    return pl.pallas_call(
        _fw_kernel,
        out_shape=jax.ShapeDtypeStruct((FOURIER_WIDTH, 2 * FOURIER_WIDTH), F32),
        name="fourier_w",
    )(c64, s64, w_f)


def _ones_row(rows, cols):
    return (lax.broadcasted_iota(jnp.int32, (rows, cols), 0) == 0).astype(BF16)


def _proj_kernel(h_ref, mod_ref, g1_ref, w1_ref, w1t_ref, rope_ref, ropet_ref, wcs_ref, qn_ref, wuq_ref,
                 kvn_ref, wukvk_ref, wukvv_ref,
                 u_o, v_o, sq_o, sk_o, sv_o, mq_o, mk_o, mv_o):
    x = h_ref[0]
    tm = x.shape[0]
    mod = mod_ref[0]
    sh = mod[:, 0:D_MODEL]
    sc = mod[:, D_MODEL:2 * D_MODEL]
    y = (_rms(x, g1_ref[...]) * (1.0 + sc) + sh).astype(BF16)
    u = jnp.dot(y, w1_ref[...], preferred_element_type=F32)
    ut = lax.dot_general(w1t_ref[...], y, _NT, preferred_element_type=F32)

    cos_h = rope_ref[:, 0:LANE]
    sin_h = rope_ref[:, LANE:2 * LANE]
    cos_m = rope_ref[:, 2 * LANE:3 * LANE]
    sin_m = rope_ref[:, 3 * LANE:4 * LANE]
    cos_mt = ropet_ref[0:LANE, :]
    sin_mt = ropet_ref[LANE:2 * LANE, :]
    cos_ht = ropet_ref[2 * LANE:3 * LANE, :]
    sin_ht = ropet_ref[3 * LANE:4 * LANE, :]

    f = u[:, C_F:C_F + FOURIER_WIDTH].astype(BF16)
    uv = jnp.dot(f, wcs_ref[...], preferred_element_type=F32).astype(BF16)
    u_o[0] = uv[:, :FOURIER_WIDTH]
    v_o[0] = uv[:, FOURIER_WIDTH:]

    def rope_rows(x1, x2, c, s, scale):
        return ((x1 * c - x2 * s) * scale).astype(BF16), ((x2 * c + x1 * s) * scale).astype(BF16)

    half = HEAD_DIM // 2
    c_h, s_h = cos_ht[0:half], sin_ht[0:half]
    for hh in range(SWA_HEADS):
        r0 = R_Q + HEAD_DIM * hh
        lo, hi = rope_rows(ut[r0:r0 + half], ut[r0 + half:r0 + HEAD_DIM], c_h, s_h, SWA_SCALE * LOG2E)
        sq_o[0, HEAD_DIM * hh:HEAD_DIM * hh + half, :] = lo
        sq_o[0, HEAD_DIM * hh + half:HEAD_DIM * (hh + 1), :] = hi
    kk = (u[:, C_K:C_K + LANE] * cos_h + u[:, C_KROT:C_KROT + LANE] * sin_h).astype(BF16)
    ones_swa = _ones_row(SWA_VT_ROWS - HEAD_DIM, tm)
    for kh in range(SWA_KV_HEADS):
        sk_o[0, kh] = kk[:, HEAD_DIM * kh:HEAD_DIM * (kh + 1)]
        sv_o[0, kh, 0:HEAD_DIM, :] = ut[R_V + HEAD_DIM * kh:R_V + HEAD_DIM * (kh + 1)].astype(BF16)
        sv_o[0, kh, HEAD_DIM:SWA_VT_ROWS, :] = ones_swa

    cqn = _rms(u[:, C_CQ:C_CQ + MLA_Q_RANK], qn_ref[...]).astype(BF16)
    qa = lax.dot_general(wuq_ref[...], cqn, _NT, preferred_element_type=F32)
    rh = MLA_ROPE_DIM // 2
    c_m, s_m = cos_mt[MLA_NOPE_DIM:MLA_NOPE_DIM + rh], sin_mt[MLA_NOPE_DIM:MLA_NOPE_DIM + rh]
    q_scale = MLA_SCALE * LOG2E
    for hh in range(MLA_HEADS):
        r0 = MLA_PAD * hh
        r1 = r0 + MLA_NOPE_DIM
        lo, hi = rope_rows(qa[r1:r1 + rh], qa[r1 + rh:r1 + 2 * rh], c_m, s_m, q_scale)
        mq_o[0, r0:r1, :] = (qa[r0:r1] * q_scale).astype(BF16)
        mq_o[0, r1:r1 + rh, :] = lo
        mq_o[0, r1 + rh:r1 + 2 * rh, :] = hi
        mq_o[0, r1 + 2 * rh:r0 + MLA_PAD, :] = jnp.zeros((MLA_PAD - MLA_NOPE_DIM - 2 * rh, tm), BF16)

    ckvn = _rms(u[:, C_CKV:C_CKV + MLA_KV_RANK], kvn_ref[...]).astype(BF16)
    kr_tile = u[:, C_KR:C_KR + LANE]
    kr = kr_tile * cos_m + pltpu.roll(kr_tile, LANE - MLA_ROPE_DIM, 1) * sin_m
    kvk = jnp.dot(ckvn, wukvk_ref[...], preferred_element_type=F32)
    vt = lax.dot_general(wukvv_ref[...], ckvn, _NT, preferred_element_type=F32).astype(BF16)
    chunk = mv_o.shape[4]
    ones_mla = _ones_row(MLA_VT_ROWS - MLA_V_DIM, chunk)
    for hh in range(MLA_HEADS):
        mk_o[0, hh] = (kvk[:, MLA_PAD * hh:MLA_PAD * (hh + 1)] + kr).astype(BF16)
        for c in range(tm // chunk):
            mv_o[0, hh, c, 0:MLA_V_DIM, :] = vt[MLA_V_DIM * hh:MLA_V_DIM * (hh + 1), chunk * c:chunk * (c + 1)]
            mv_o[0, hh, c, MLA_V_DIM:MLA_VT_ROWS, :] = ones_mla


def _proj_call(h, mod3, mod_row, g1, w1, w1t, rope, ropet, wcs, qn, wuq, kvn, wukvk, wukvv, tm):
    b_, s_, _ = h.shape
    chunk = min(MLA_TK, tm)
    const = lambda b, i: (0, 0)
    if mod_row is None:
        mod_map = lambda b, i: (b, 0, 0)
    else:
        mod_map = lambda b, i: (mod_row, 0, 0)
    out_shapes = (
        jax.ShapeDtypeStruct((b_, s_, FOURIER_WIDTH), BF16),
        jax.ShapeDtypeStruct((b_, s_, FOURIER_WIDTH), BF16),
        jax.ShapeDtypeStruct((b_, SWA_Q_WIDTH, s_), BF16),
        jax.ShapeDtypeStruct((b_, SWA_KV_HEADS, s_, HEAD_DIM), BF16),
        jax.ShapeDtypeStruct((b_, SWA_KV_HEADS, SWA_VT_ROWS, s_), BF16),
        jax.ShapeDtypeStruct((b_, MLA_HEADS * MLA_PAD, s_), BF16),
        jax.ShapeDtypeStruct((b_, MLA_HEADS, s_, MLA_PAD), BF16),
        jax.ShapeDtypeStruct((b_, MLA_HEADS, s_ // chunk, MLA_VT_ROWS, chunk), BF16),
    )
    out_specs = (
        pl.BlockSpec((1, tm, FOURIER_WIDTH), lambda b, i: (b, i, 0)),
        pl.BlockSpec((1, tm, FOURIER_WIDTH), lambda b, i: (b, i, 0)),
        pl.BlockSpec((1, SWA_Q_WIDTH, tm), lambda b, i: (b, 0, i)),
        pl.BlockSpec((1, SWA_KV_HEADS, tm, HEAD_DIM), lambda b, i: (b, 0, i, 0)),
        pl.BlockSpec((1, SWA_KV_HEADS, SWA_VT_ROWS, tm), lambda b, i: (b, 0, 0, i)),
        pl.BlockSpec((1, MLA_HEADS * MLA_PAD, tm), lambda b, i: (b, 0, i)),
        pl.BlockSpec((1, MLA_HEADS, tm, MLA_PAD), lambda b, i: (b, 0, i, 0)),
        pl.BlockSpec((1, MLA_HEADS, tm // chunk, MLA_VT_ROWS, chunk), lambda b, i: (b, 0, i, 0, 0)),
    )
    return pl.pallas_call(
        _proj_kernel,
        grid=(b_, s_ // tm),
        in_specs=[
            pl.BlockSpec((1, tm, D_MODEL), lambda b, i: (b, i, 0)),
            pl.BlockSpec((1, 1, 6 * D_MODEL), mod_map),
            pl.BlockSpec((1, D_MODEL), const),
            pl.BlockSpec((D_MODEL, W1_COLS), const),
            pl.BlockSpec((W1T_ROWS, D_MODEL), const),
            pl.BlockSpec((tm, 4 * LANE), lambda b, i: (i, 0)),
            pl.BlockSpec((4 * LANE, tm), lambda b, i: (0, i)),
            pl.BlockSpec((FOURIER_WIDTH, 2 * FOURIER_WIDTH), const),
            pl.BlockSpec((1, MLA_Q_RANK), const),
            pl.BlockSpec((MLA_HEADS * MLA_PAD, MLA_Q_RANK), const),
            pl.BlockSpec((1, MLA_KV_RANK), const),
            pl.BlockSpec((MLA_KV_RANK, MLA_HEADS * MLA_PAD), const),
            pl.BlockSpec((MLA_HEADS * MLA_V_DIM, MLA_KV_RANK), const),
        ],
        out_specs=out_specs,
        out_shape=out_shapes,
        compiler_params=_cparams(("parallel", "parallel")),
        name="norm_in_proj",
    )(h, mod3, g1, w1, w1t, rope, ropet, wcs, qn, wuq, kvn, wukvk, wukvv)


def _dft_kernel(c_ref, s_ref, u_ref, v_ref, o_ref):
    o = jnp.dot(c_ref[...], u_ref[0].astype(BF16), preferred_element_type=F32)
    o = o + jnp.dot(s_ref[...], v_ref[0].astype(BF16), preferred_element_type=F32)
    o_ref[0] = o


def _dft_call(ctab, stab, u, v, tk):
    b_, s_, _ = u.shape
    whole = pl.BlockSpec((1, s_, FOURIER_WIDTH), lambda i, b: (b, 0, 0))
    return pl.pallas_call(
        _dft_kernel,
        grid=(s_ // tk, b_),
        in_specs=[
            pl.BlockSpec((tk, s_), lambda i, b: (i, 0)),
            pl.BlockSpec((tk, s_), lambda i, b: (i, 0)),
            whole, whole,
        ],
        out_specs=pl.BlockSpec((1, tk, FOURIER_WIDTH), lambda i, b: (b, i, 0)),
        out_shape=jax.ShapeDtypeStruct((b_, s_, FOURIER_WIDTH), F32),
        compiler_params=_cparams(("parallel", "arbitrary")),
        name="fourier_dft",
    )(ctab, stab, u, v)


FFT_SUB = 8
FFT_STEP = 2


def _fft1_kernel(m1_ref, m2_ref, tc_ref, ts_ref, u_ref, v_ref, bre_ref, bim_ref):
    n1, _, w = u_ref.shape[1:]
    sub = FFT_SUB
    u_all = u_ref[0].astype(F32)
    v_all = v_ref[0].astype(F32)
    bre, bim = [], []
    for h in range(u_ref.shape[2] // sub):
        js = slice(sub * h, sub * (h + 1))
        xu = u_all[:, js, :].reshape(n1 * sub, w).astype(BF16)
        xv = v_all[:, js, :].reshape(n1 * sub, w).astype(BF16)
        a = jnp.dot(m1_ref[...], xu, preferred_element_type=F32)
        a = a + jnp.dot(m2_ref[...], xv, preferred_element_type=F32)
        are, aim = a[0:n1 * sub], a[n1 * sub:]
        tc, ts = tc_ref[h], ts_ref[h]
        bre.append((are * tc + aim * ts).reshape(n1, sub, w))
        bim.append((aim * tc - are * ts).reshape(n1, sub, w))
    bre_ref[0] = jnp.concatenate(bre, axis=1).astype(bre_ref.dtype)
    bim_ref[0] = jnp.concatenate(bim, axis=1).astype(bim_ref.dtype)


def _fft2_kernel(g1_ref, g2_ref, bre_ref, bim_ref, o_ref):
    n2 = g1_ref.shape[1]
    for j in range(o_ref.shape[2]):
        rows = slice(n2 * j, n2 * (j + 1))
        x = jnp.dot(g1_ref[...], bre_ref[0, rows, :], preferred_element_type=F32)
        x = x + jnp.dot(g2_ref[...], bim_ref[0, rows, :], preferred_element_type=F32)
        o_ref[0, :, j, :] = x


def _fft_call(tabs, u, v):
    m1, m2, tc, ts, g1, g2 = tabs
    b_, s_, w = u.shape
    n1 = m1.shape[1] // FFT_SUB
    n2 = s_ // n1
    u4 = u.reshape(b_, n1, n2, w)
    v4 = v.reshape(b_, n1, n2, w)
    per_step = FFT_STEP * FFT_SUB
    tile = pl.BlockSpec((1, n1, per_step, w), lambda t, b: (b, 0, t, 0))
    tab = pl.BlockSpec((FFT_STEP, n1 * FFT_SUB, w), lambda t, b: (t, 0, 0))
    mat = pl.BlockSpec((2 * n1 * FFT_SUB, n1 * FFT_SUB), lambda t, b: (0, 0))
    bre, bim = pl.pallas_call(
        _fft1_kernel,
        grid=(n2 // per_step, b_),
        in_specs=[mat, mat, tab, tab, tile, tile],
        out_specs=(tile, tile),
        out_shape=(jax.ShapeDtypeStruct((b_, n1, n2, w), BF16),) * 2,
        compiler_params=_cparams(("parallel", "arbitrary")),
        name="fourier_fft1",
    )(m1, m2, tc, ts, u4, v4)
    blk = pl.BlockSpec((1, per_step * n2, w), lambda b, t: (b, t, 0))
    mat2 = pl.BlockSpec((n2, n2), lambda b, t: (0, 0))
    out = pl.pallas_call(
        _fft2_kernel,
        grid=(b_, n1 // per_step),
        in_specs=[mat2, mat2, blk, blk],
        out_specs=pl.BlockSpec((1, n2, per_step, w), lambda b, t: (b, 0, t, 0)),
        out_shape=jax.ShapeDtypeStruct((b_, n2, n1, w), F32),
        compiler_params=_cparams(("parallel", "parallel")),
        name="fourier_fft2",
    )(g1, g2, bre.reshape(b_, s_, w), bim.reshape(b_, s_, w))
    return out.reshape(b_, s_, w)


def _fft_tables(s_, n1, w):
    n2 = s_ // n1
    k1 = np.arange(n1)
    a1 = ((k1[:, None] * k1[None, :]) % n1).astype(np.float64) * (2.0 * math.pi / n1)
    c1, s1 = np.cos(a1), np.sin(a1)
    eye = np.eye(FFT_SUB)
    expand = lambda a: jnp.asarray(np.kron(a, eye), dtype=F32).astype(BF16)
    m1 = expand(np.concatenate([c1, -s1], axis=0))
    m2 = expand(np.concatenate([-s1, -c1], axis=0))
    s2 = jnp.arange(n2, dtype=jnp.int32)
    n_t = n2 // FFT_SUB
    row = jnp.arange(n1 * FFT_SUB, dtype=jnp.int32)
    s2_of = FFT_SUB * jnp.arange(n_t, dtype=jnp.int32)[:, None] + (row % FFT_SUB)[None, :]
    th = (((row // FFT_SUB)[None, :] * s2_of) % s_).astype(F32) * (2.0 * math.pi / s_)
    tc = jnp.broadcast_to(jnp.cos(th)[:, :, None], (n_t, n1 * FFT_SUB, w))
    ts = jnp.broadcast_to(jnp.sin(th)[:, :, None], (n_t, n1 * FFT_SUB, w))
    a2 = ((s2[:, None] * s2[None, :]) % n2).astype(F32) * (2.0 * math.pi / n2)
    scale = 1.0 / math.sqrt(s_)
    g1 = (jnp.cos(a2) * scale).astype(BF16)
    g2 = (jnp.sin(a2) * scale).astype(BF16)
    return m1, m2, tc, ts, g1, g2


def _mla_kernel(*refs, n_chunks, heads_per_step):
    hps = heads_per_step
    n_in = 6 if n_chunks else 4
    if n_chunks:
        zero_ref, qt_ref, k_ref, vt_ref, kc_ref, vct_ref = refs[:n_in]
    else:
        zero_ref, qt_ref, kc_ref, vct_ref = refs[:n_in]
    o_ref = refs[n_in]
    row0 = pl.multiple_of(zero_ref[0], MLA_TK)
    scratch = refs[n_in + 1:]
    tq = scratch[0].shape[1]
    n_sub = qt_ref.shape[2] // tq
    lc = kc_ref.shape[2]
    s_refs = [[scratch[MLA_S_SLOTS * (hps * u + hh):MLA_S_SLOTS * (hps * u + hh + 1)] for hh in range(hps)]
              for u in range(n_sub)]

    def keys(idx):
        if idx == 0:
            return (lambda hh: kc_ref[0, hh]), lc
        return (lambda hh: k_ref[0, hh, (idx - 1) * MLA_TK:idx * MLA_TK, :]), MLA_TK

    def values(idx):
        if idx == 0:
            return lambda hh: vct_ref[0, hh, 0]
        return lambda hh: vt_ref[0, hh, idx - 1]

    def logits(u, idx):
        ks, rows = keys(idx)
        cms = []
        for hh in range(hps):
            qt = qt_ref[0, MLA_PAD * hh:MLA_PAD * (hh + 1), tq * u:tq * (u + 1)]
            s = jnp.dot(ks(hh), qt, preferred_element_type=F32)
            s_refs[u][hh][idx % MLA_S_SLOTS][0:rows, :] = s
            cms.append(jnp.max(s, axis=0, keepdims=True))
        return tuple(cms)

    def accumulate(u, idx, carries, cms):
        _, rows = keys(idx)
        vts = values(idx)
        out = []
        for hh in range(hps):
            m, acc = carries[hh]
            m_new = jnp.maximum(m, cms[hh])
            alpha = jnp.exp2(m - m_new)
            s = s_refs[u][hh][idx % MLA_S_SLOTS][pl.ds(row0, rows), :]
            p = jnp.exp2(s - m_new).astype(BF16)
            acc = alpha * acc + jnp.dot(vts(hh), p, preferred_element_type=F32)
            out.append((m_new, acc))
        return tuple(out)

    init = (jnp.full((1, tq), NEG_INF, F32), jnp.zeros((MLA_VT_ROWS, tq), F32))
    carries = [tuple(init for _ in range(hps)) for _ in range(n_sub)]
    pending = [None] * n_sub
    n_idx = n_chunks + 1
    for t in range(n_idx + n_sub):
        for u in range(n_sub):
            idx = t - u
            new_cms = logits(u, idx) if 0 <= idx < n_idx else None
            if 1 <= idx <= n_idx:
                carries[u] = accumulate(u, idx - 1, carries[u], pending[u])
            pending[u] = new_cms
    for u in range(n_sub):
        o = jnp.concatenate([acc[0:MLA_V_DIM] / acc[MLA_V_DIM:MLA_V_DIM + 1] for (_, acc) in carries[u]],
                            axis=0)
        o_ref[0, tq * u:tq * (u + 1), :] = o.T.astype(o_ref.dtype)


def _mla_call(qt, k, vt, kc, vct, tq, n_sub):
    b_, _, sq = qt.shape
    h_ = kc.shape[1]
    lc = kc.shape[2]
    hps = 2
    n_chunks = 0 if k is None else k.shape[2] // MLA_TK
    assert n_chunks == 0 or lc <= MLA_TK
    in_specs = [pl.BlockSpec(memory_space=pltpu.SMEM),
                pl.BlockSpec((1, hps * MLA_PAD, tq), lambda b, p, i: (b, p, i))]
    args = [jnp.zeros((1,), jnp.int32), qt]
    if k is not None:
        sk = k.shape[2]
        in_specs += [pl.BlockSpec((1, hps, sk, MLA_PAD), lambda b, p, i: (b, p, 0, 0)),
                     pl.BlockSpec((1, hps, n_chunks, MLA_VT_ROWS, MLA_TK), lambda b, p, i: (b, p, 0, 0, 0))]
        args += [k, vt]
    in_specs += [pl.BlockSpec((1, hps, lc, MLA_PAD), lambda b, p, i: (b, p, 0, 0)),
                 pl.BlockSpec((1, hps, 1, MLA_VT_ROWS, lc), lambda b, p, i: (b, p, 0, 0, 0))]
    args += [kc, vct]
    return pl.pallas_call(
        functools.partial(_mla_kernel, n_chunks=n_chunks, heads_per_step=hps),
        grid=(b_, h_ // hps, sq // tq),
        in_specs=in_specs,
        out_specs=pl.BlockSpec((1, tq, hps * MLA_V_DIM), lambda b, p, i: (b, i, p)),
        out_shape=jax.ShapeDtypeStruct((b_, sq, h_ * MLA_V_DIM), BF16),
        scratch_shapes=[pltpu.VMEM((MLA_TK if n_chunks else lc, tq // n_sub), F32)] * (MLA_S_SLOTS * hps * n_sub),
        compiler_params=_cparams(("parallel", "parallel", "arbitrary")),
        name="mla_attn",
    )(*args)


def _swa_kernel(*refs, n_local, s_len):
    zero_ref, sink_ref, qt_ref = refs[:3]
    k_refs = refs[3:3 + n_local]
    vt_refs = refs[3 + n_local:3 + 2 * n_local]
    kc_ref, vct_ref, o_ref = refs[3 + 2 * n_local:6 + 2 * n_local]
    s_refs = refs[6 + 2 * n_local:]
    row0 = pl.multiple_of(zero_ref[0], BLOCK)
    tq = s_refs[0].shape[1]
    step_tq = qt_ref.shape[2]
    n_sub = step_tq // tq
    lc = kc_ref.shape[2]
    sub_blocks = tq // BLOCK + 2 if n_local else 0
    nk = sub_blocks * BLOCK
    sinks = [sink_ref[hh] * LOG2E for hh in range(SWA_HEADS)]

    def mask_bias(u):
        q0 = pl.program_id(1) * step_tq + u * tq
        kpos = q0 - BLOCK + lax.broadcasted_iota(jnp.int32, (nk, tq), 0)
        qpos = q0 + lax.broadcasted_iota(jnp.int32, (nk, tq), 1)
        valid = (jnp.abs(kpos - qpos) <= WINDOW) & (kpos >= 0) & (kpos < s_len)
        return jnp.where(valid, 0.0, NEG_INF).astype(F32)

    def keys(u, kh):
        kc = kc_ref[0, kh]
        vct = vct_ref[0, kh]
        if not n_local:
            return kc, vct
        first = u * (tq // BLOCK)
        blocks = range(first, first + sub_blocks)
        return (jnp.concatenate([kc] + [k_refs[j][0, kh] for j in blocks], axis=0),
                jnp.concatenate([vct] + [vt_refs[j][0, kh] for j in blocks], axis=1))

    def logits(unit, u, hh, k_all, bias):
        qt = qt_ref[0, HEAD_DIM * hh:HEAD_DIM * (hh + 1), tq * u:tq * (u + 1)]
        s_ref = s_refs[unit % 2]
        s_c = jnp.dot(k_all[0:lc], qt, preferred_element_type=F32)
        s_ref[0:lc, :] = s_c
        m = jnp.maximum(jnp.max(s_c, axis=0, keepdims=True), sinks[hh])
        if n_local:
            s_l = jnp.dot(k_all[lc:], qt, preferred_element_type=F32) + bias
            s_ref[lc:lc + nk, :] = s_l
            m = jnp.maximum(m, jnp.max(s_l, axis=0, keepdims=True))
        return m

    def accumulate(unit, hh, vt_all, m):
        p = jnp.exp2(s_refs[unit % 2][pl.ds(row0, lc + nk), :] - m).astype(BF16)
        acc = jnp.dot(vt_all, p, preferred_element_type=F32)
        l = acc[HEAD_DIM:HEAD_DIM + 1] + jnp.exp2(sinks[hh] - m)
        return acc[0:HEAD_DIM] / l

    units = [(u, hh) for u in range(n_sub) for hh in range(SWA_HEADS)]
    kv = {(u, kh): keys(u, kh) for u in range(n_sub) for kh in range(SWA_KV_HEADS)}
    biases = [mask_bias(u) if n_local else None for u in range(n_sub)]
    outs = [[] for _ in range(n_sub)]
    m_prev = None
    for unit in range(len(units) + 1):
        m_next = None
        if unit < len(units):
            u, hh = units[unit]
            m_next = logits(unit, u, hh, kv[(u, hh // SWA_GROUP)][0], biases[u])
        if unit > 0:
            u, hh = units[unit - 1]
            outs[u].append(accumulate(unit - 1, hh, kv[(u, hh // SWA_GROUP)][1], m_prev))
        m_prev = m_next
    for u in range(n_sub):
        o = jnp.concatenate(outs[u], axis=0)
        o_ref[0, tq * u:tq * (u + 1), :] = o.T.astype(o_ref.dtype)


def _swa_call(sink, qt, k, vt, kc, vct):
    b_, _, sq = qt.shape
    lc = kc.shape[2]
    local = k is not None
    sub_tq = SWA_TQ if local else sq
    tq = sub_tq * SWA_SUB_TILES if local else sq
    nb = sq // tq
    n_local = tq // BLOCK + 2 if local else 0
    sub_keys = lc + (sub_tq // BLOCK + 2) * BLOCK if local else lc
    in_specs = [pl.BlockSpec(memory_space=pltpu.SMEM), pl.BlockSpec(memory_space=pltpu.SMEM),
                pl.BlockSpec((1, SWA_Q_WIDTH, tq), lambda b, n: (b, 0, n))]
    args = [jnp.zeros((1,), jnp.int32), sink, qt]
    if local:
        nkb = sq // BLOCK
        per = tq // BLOCK

        def blk_idx(n, j):
            return jnp.clip(n * per - 1 + j, 0, nkb - 1)

        in_specs += [pl.BlockSpec((1, SWA_KV_HEADS, BLOCK, HEAD_DIM),
                                  functools.partial(lambda b, n, j: (b, 0, blk_idx(n, j), 0), j=j))
                     for j in range(n_local)]
        in_specs += [pl.BlockSpec((1, SWA_KV_HEADS, SWA_VT_ROWS, BLOCK),
                                  functools.partial(lambda b, n, j: (b, 0, 0, blk_idx(n, j)), j=j))
                     for j in range(n_local)]
        args += [k] * n_local + [vt] * n_local
    in_specs += [pl.BlockSpec((1, SWA_KV_HEADS, lc, HEAD_DIM), lambda b, n: (b, 0, 0, 0)),
                 pl.BlockSpec((1, SWA_KV_HEADS, SWA_VT_ROWS, lc), lambda b, n: (b, 0, 0, 0))]
    args += [kc, vct]
    return pl.pallas_call(
        functools.partial(_swa_kernel, n_local=n_local, s_len=sq),
        grid=(b_, nb),
        in_specs=in_specs,
        out_specs=pl.BlockSpec((1, tq, SWA_Q_WIDTH), lambda b, n: (b, n, 0)),
        out_shape=jax.ShapeDtypeStruct((b_, sq, SWA_Q_WIDTH), BF16),
        scratch_shapes=[pltpu.VMEM((sub_keys, sub_tq), F32)] * 2,
        compiler_params=_cparams(("parallel", "arbitrary")),
        name="swa_attn",
    )(*args)


def _mlp_kernel(fo_ref, so_ref, mo_ref, h_ref, mod_ref, g2_ref, wo_ref, w1_ref, w2_ref, gf_ref, o_ref, *,
                final, ff_chunk):
    mod = mod_ref[0]
    g1 = mod[:, 2 * D_MODEL:3 * D_MODEL]
    sh2 = mod[:, 3 * D_MODEL:4 * D_MODEL]
    sc2 = mod[:, 4 * D_MODEL:5 * D_MODEL]
    g2 = mod[:, 5 * D_MODEL:6 * D_MODEL]
    mix = jnp.concatenate([fo_ref[0].astype(BF16), so_ref[0], mo_ref[0]], axis=-1)
    h = h_ref[0] + g1 * jnp.dot(mix, wo_ref[...], preferred_element_type=F32)
    y = (_rms(h, g2_ref[...]) * (1.0 + sc2) + sh2).astype(BF16)
    acc = None
    for c in range(D_FF // ff_chunk):
        a = jnp.dot(y, w1_ref[:, c * ff_chunk:(c + 1) * ff_chunk], preferred_element_type=F32)
        a = jnp.maximum(a, 0.0)
        a = (a * a).astype(BF16)
        part = jnp.dot(a, w2_ref[c * ff_chunk:(c + 1) * ff_chunk, :], preferred_element_type=F32)
        acc = part if acc is None else acc + part
    h = h + g2 * acc
    if final:
        h = _rms(h, gf_ref[...])
    o_ref[0] = h


def _mlp_call(fo, so, mo, h, mod3, mod_row, g2, wo, w1, w2, gf, tm, final):
    b_, s_, _ = h.shape
    const = lambda b, i: (0, 0)
    tile = lambda b, i: (b, i, 0)
    if mod_row is None:
        mod_map = lambda b, i: (b, 0, 0)
    else:
        mod_map = lambda b, i: (mod_row, 0, 0)
    once = pl.Buffered(1)
    return pl.pallas_call(
        functools.partial(_mlp_kernel, final=final, ff_chunk=1024),
        grid=(b_, s_ // tm),
        in_specs=[
            pl.BlockSpec((1, tm, FOURIER_WIDTH), tile),
            pl.BlockSpec((1, tm, SWA_Q_WIDTH), tile),
            pl.BlockSpec((1, tm, MLA_HEADS * MLA_V_DIM), tile),
            pl.BlockSpec((1, tm, D_MODEL), tile),
            pl.BlockSpec((1, 1, 6 * D_MODEL), mod_map),
            pl.BlockSpec((1, D_MODEL), const),
            pl.BlockSpec((D_MIX, D_MODEL), const, pipeline_mode=once),
            pl.BlockSpec((D_MODEL, D_FF), const, pipeline_mode=once),
            pl.BlockSpec((D_FF, D_MODEL), const, pipeline_mode=once),
            pl.BlockSpec((1, D_MODEL), const),
        ],
        out_specs=pl.BlockSpec((1, tm, D_MODEL), tile),
        out_shape=jax.ShapeDtypeStruct((b_, s_, D_MODEL), F32),
        compiler_params=_cparams(("parallel", "parallel")),
        name="out_proj_mlp",
    )(fo, so, mo, h, mod3, g2, wo, w1, w2, gf)


def _rot_cols(w, heads, dim):
    w4 = w.reshape(w.shape[0], heads, 2, dim // 2)
    return jnp.stack([-w4[:, :, 1], w4[:, :, 0]], axis=2).reshape(w.shape[0], heads * dim)


def _pad_lanes(w, heads, before, after):
    w3 = w.reshape(w.shape[0], heads, w.shape[1] // heads)
    w3 = jnp.pad(w3, ((0, 0), (0, 0), (before, after)))
    return w3.reshape(w.shape[0], -1)


def _pack_w_in(w_in):
    wk = w_in[:, OFF_SWA_K:OFF_SWA_V]
    wkr = w_in[:, OFF_MLA_KR:OFF_MLA_KR + MLA_ROPE_DIM]
    kr_tile = jnp.concatenate([jnp.zeros((D_MODEL, MLA_NOPE_DIM), F32), wkr, _rot_cols(wkr, 1, MLA_ROPE_DIM)],
                              axis=1)
    cols = [w_in[:, :OFF_SWA_Q], wk, w_in[:, OFF_MLA_CQ:OFF_MLA_KR], kr_tile,
            _rot_cols(wk, SWA_KV_HEADS, HEAD_DIM)]
    rows = w_in[:, OFF_SWA_Q:OFF_SWA_K], w_in[:, OFF_SWA_V:OFF_MLA_CQ]
    return jnp.concatenate(cols, axis=1).astype(BF16), jnp.concatenate(rows, axis=1).T.astype(BF16)


def _pack_w_uq(w_uq):
    per = MLA_NOPE_DIM + MLA_ROPE_DIM
    return _pad_lanes(w_uq, MLA_HEADS, 0, MLA_PAD - per).T.astype(BF16)


def _pack_w_ukv(w_ukv):
    w3 = w_ukv.reshape(MLA_KV_RANK, MLA_HEADS, MLA_NOPE_DIM + MLA_V_DIM)
    wk = w3[:, :, :MLA_NOPE_DIM].reshape(MLA_KV_RANK, MLA_HEADS * MLA_NOPE_DIM)
    wv = w3[:, :, MLA_NOPE_DIM:].reshape(MLA_KV_RANK, MLA_HEADS * MLA_V_DIM)
    return _pad_lanes(wk, MLA_HEADS, 0, MLA_PAD - MLA_NOPE_DIM).astype(BF16), wv.T.astype(BF16)


def _rope_table(rows):
    r, col = jnp.meshgrid(jnp.arange(rows, dtype=F32), jnp.arange(GRID_W, dtype=F32), indexing="ij")
    r = r.reshape(-1)
    col = col.reshape(-1)

    def tables(dim):
        n_freq = dim // 4
        inv = ROPE_THETA ** (-jnp.arange(n_freq, dtype=F32) / n_freq)
        ang = jnp.concatenate([r[:, None] * inv[None, :], col[:, None] * inv[None, :]], axis=-1)
        return jnp.cos(ang), jnp.sin(ang)

    ch, sh = tables(HEAD_DIM)
    cr, sr = tables(MLA_ROPE_DIM)
    mla_pad = ((0, 0), (MLA_NOPE_DIM, LANE - MLA_NOPE_DIM - MLA_ROPE_DIM))
    nope_ones = (jnp.arange(LANE) < MLA_NOPE_DIM).astype(F32)[None, :]
    return jnp.concatenate([jnp.tile(ch, (1, 4)), jnp.tile(sh, (1, 4)),
                            jnp.pad(jnp.tile(cr, (1, 2)), mla_pad) + nope_ones,
                            jnp.pad(jnp.tile(sr, (1, 2)), mla_pad)], axis=-1)


def _identity_rope_table(n):
    ones = jnp.ones((n, LANE), F32)
    zeros = jnp.zeros((n, LANE), F32)
    m = jnp.concatenate([jnp.ones((n, MLA_NOPE_DIM + MLA_ROPE_DIM), F32),
                         jnp.zeros((n, LANE - MLA_NOPE_DIM - MLA_ROPE_DIM), F32)], axis=-1)
    return jnp.concatenate([ones, zeros, m, zeros], axis=-1)


def _dft_tables(n):
    k = jnp.arange(n, dtype=jnp.int32)
    ks = (k[:, None] * k[None, :]) % n
    ang = ks.astype(F32) * (2.0 * math.pi / n)
    scale = 1.0 / math.sqrt(n)
    return (jnp.cos(ang) * scale).astype(BF16), (-jnp.sin(ang) * scale).astype(BF16)


def kernel(x, c, ctx, c_ctx, w_ada, b_ada, norm1_g, norm2_g, w_in, w_fourier, swa_sink, mla_q_norm, w_uq,
           mla_kv_norm, w_ukv, w_out, w_mlp1, w_mlp2, final_norm_g):
    b_, s_, _ = x.shape
    lc = ctx.shape[1]
    depth = w_ada.shape[0]
    rows = s_ // GRID_W

    mod_rows = 16
    cc = jnp.concatenate([c, c_ctx[None, :], jnp.zeros((mod_rows - b_ - 1, D_MODEL), F32)], axis=0)
    mod_all = _ada_call(cc, w_ada, b_ada)

    rope = _rope_table(rows)
    rope_c = _identity_rope_table(lc)
    to_feature_major = lambda t: jnp.concatenate([t[:, 2 * LANE:], t[:, :2 * LANE]], axis=1).T
    ropet = to_feature_major(rope)
    ropet_c = to_feature_major(rope_c)
    fft_tabs = _fft_tables(s_, GRID_W, FOURIER_WIDTH)
    ctab_c, stab_c = _dft_tables(lc)
    kk = jnp.arange(FOURIER_GROUP_DIM, dtype=jnp.int32)
    ang64 = ((kk[:, None] * kk[None, :]) % FOURIER_GROUP_DIM).astype(F32) * (2.0 * math.pi / FOURIER_GROUP_DIM)
    c64, s64 = jnp.cos(ang64), jnp.sin(ang64)

    h, hc = x, ctx
    for l in range(depth):
        last = l == depth - 1
        mod3 = mod_all[l].reshape(mod_rows, 1, 6 * D_MODEL)
        g1 = norm1_g[l].reshape(1, D_MODEL)
        g2 = norm2_g[l].reshape(1, D_MODEL)
        gf = final_norm_g.reshape(1, D_MODEL)
        w1, w1t = _pack_w_in(w_in[l])
        wuq = _pack_w_uq(w_uq[l])
        wukvk, wukvv = _pack_w_ukv(w_ukv[l])
        wcs = _fw_call(c64, s64, w_fourier[l]).astype(BF16)
        qn = mla_q_norm[l].reshape(1, MLA_Q_RANK)
        kvn = mla_kv_norm[l].reshape(1, MLA_KV_RANK)
        wo = w_out[l].astype(BF16)
        wm1 = w_mlp1[l].astype(BF16)
        wm2 = w_mlp2[l].astype(BF16)
        sink = swa_sink[l]

        fu, fv, sq, sk, sv, mq, mk, mv = _proj_call(h, mod3, None, g1, w1, w1t, rope, ropet, wcs, qn, wuq,
                                                    kvn, wukvk, wukvv, tm=1024)
        fuc, fvc, sqc, skc, svc, mqc, mkc, mvc = _proj_call(hc, mod3, b_, g1, w1, w1t, rope_c, ropet_c, wcs, qn,
                                                            wuq, kvn, wukvk, wukvv, tm=lc)

        fo = _fft_call(fft_tabs, fu, fv)
        so = _swa_call(sink, sq, sk, sv, skc, svc)
        mo = _mla_call(mq, mk, mv, mkc, mvc, tq=1024, n_sub=2)
        h = _mlp_call(fo, so, mo, h, mod3, None, g2, wo, wm1, wm2, gf, tm=512, final=last)

        if not last:
            foc = _dft_call(ctab_c, stab_c, fuc, fvc, tk=lc)
            soc = _swa_call(sink, sqc, None, None, skc, svc)
            moc = _mla_call(mqc, None, None, mkc, mvc, tq=lc, n_sub=1)
            hc = _mlp_call(foc, soc, moc, hc, mod3, b_, g2, wo, wm1, wm2, gf, tm=lc, final=False)
    return h
```

```python
import functools
import math

import jax
import jax.numpy as jnp
import numpy as np
from jax import lax
from jax.experimental import pallas as pl
from jax.experimental.pallas import tpu as pltpu

D_MODEL = 1024
GRID_W = 64
HEAD_DIM = 64
FOURIER_GROUPS = 4
FOURIER_GROUP_DIM = 64
FOURIER_WIDTH = FOURIER_GROUPS * FOURIER_GROUP_DIM
SWA_HEADS = 6
SWA_KV_HEADS = 2
SWA_GROUP = SWA_HEADS // SWA_KV_HEADS
SWA_Q_WIDTH = SWA_HEADS * HEAD_DIM
SWA_KV_WIDTH = SWA_KV_HEADS * HEAD_DIM
WINDOW = 128
BLOCK = 128
MLA_HEADS = 6
MLA_NOPE_DIM = 64
MLA_ROPE_DIM = 32
MLA_V_DIM = 64
MLA_Q_RANK = 256
MLA_KV_RANK = 128
MLA_SCALE = (MLA_NOPE_DIM + MLA_ROPE_DIM) ** -0.5
SWA_SCALE = HEAD_DIM ** -0.5
D_MIX = FOURIER_WIDTH + SWA_Q_WIDTH + MLA_HEADS * MLA_V_DIM
OFF_SWA_Q = FOURIER_WIDTH
OFF_SWA_K = OFF_SWA_Q + SWA_Q_WIDTH
OFF_SWA_V = OFF_SWA_K + SWA_KV_WIDTH
OFF_MLA_CQ = OFF_SWA_V + SWA_KV_WIDTH
OFF_MLA_CKV = OFF_MLA_CQ + MLA_Q_RANK
OFF_MLA_KR = OFF_MLA_CKV + MLA_KV_RANK
D_FF = 4 * D_MODEL
ROPE_THETA = 10000.0
NORM_EPS = 1e-6
NEG_INF = -1e30

LANE = 128
MLA_PAD = LANE

C_F = 0
C_K = C_F + FOURIER_WIDTH
C_CQ = C_K + SWA_KV_WIDTH
C_CKV = C_CQ + MLA_Q_RANK
C_KR = C_CKV + MLA_KV_RANK
C_KROT = C_KR + LANE
W1_COLS = C_KROT + SWA_KV_WIDTH
R_Q = 0
R_V = R_Q + SWA_Q_WIDTH
W1T_ROWS = R_V + SWA_KV_WIDTH

BF16 = jnp.bfloat16
F32 = jnp.float32
LOG2E = math.log2(math.e)
_NT = (((1,), (1,)), ((), ()))
MLA_TK = 256
MLA_VT_ROWS = MLA_V_DIM + 16
MLA_S_SLOTS = 2
SWA_VT_ROWS = HEAD_DIM + 16
SWA_TQ = 256
SWA_SUB_TILES = 4

VMEM_LIMIT = 56 * 1024 * 1024


def _cparams(sem):
    return pltpu.CompilerParams(dimension_semantics=sem, vmem_limit_bytes=VMEM_LIMIT)


def _rms(x, g):
    ms = jnp.mean(x * x, axis=-1, keepdims=True)
    return x * lax.rsqrt(ms + NORM_EPS) * g


def _ada_kernel(cc_ref, w_ref, b_ref, o_ref):
    cc = cc_ref[...]
    s = cc * jax.nn.sigmoid(cc)
    o_ref[0] = jnp.dot(s, w_ref[0], preferred_element_type=F32,
                       precision=lax.Precision.HIGHEST) + b_ref[0]


def _ada_call(cc, w_ada, b_ada):
    depth = w_ada.shape[0]
    rows = cc.shape[0]
    tn = 1536
    n_out = w_ada.shape[2]
    return pl.pallas_call(
        _ada_kernel,
        grid=(depth, n_out // tn),
        in_specs=[
            pl.BlockSpec((rows, D_MODEL), lambda l, j: (0, 0)),
            pl.BlockSpec((1, D_MODEL, tn), lambda l, j: (l, 0, j)),
            pl.BlockSpec((1, 1, tn), lambda l, j: (l, 0, j)),
        ],
        out_specs=pl.BlockSpec((1, rows, tn), lambda l, j: (l, 0, j)),
        out_shape=jax.ShapeDtypeStruct((depth, rows, n_out), F32),
        compiler_params=_cparams(("arbitrary", "arbitrary")),
        name="ada_mod",
    )(cc, w_ada, b_ada.reshape(depth, 1, n_out))


def _fw_kernel(c_ref, s_ref, w_ref, o_ref):
    o_ref[...] = jnp.zeros(o_ref.shape, o_ref.dtype)
    inv = 1.0 / math.sqrt(FOURIER_GROUP_DIM)
    for g in range(FOURIER_GROUPS):
        w = w_ref[g]
        a = jnp.dot(c_ref[...], w, preferred_element_type=F32, precision=lax.Precision.HIGHEST) * inv
        b = jnp.dot(s_ref[...], w, preferred_element_type=F32, precision=lax.Precision.HIGHEST) * inv
        lo = g * FOURIER_GROUP_DIM
        hi = lo + FOURIER_GROUP_DIM
        o_ref[lo:hi, lo:hi] = a
        o_ref[lo:hi, FOURIER_WIDTH + lo:FOURIER_WIDTH + hi] = b


def _fw_call(c64, s64, w_f):
    return pl.pallas_call(
        _fw_kernel,
        out_shape=jax.ShapeDtypeStruct((FOURIER_WIDTH, 2 * FOURIER_WIDTH), F32),
        name="fourier_w",
    )(c64, s64, w_f)


def _ones_row(rows, cols):
    return (lax.broadcasted_iota(jnp.int32, (rows, cols), 0) == 0).astype(BF16)


def _proj_kernel(h_ref, mod_ref, g1_ref, w1_ref, w1t_ref, rope_ref, ropet_ref, wcs_ref, qn_ref, wuq_ref,
                 kvn_ref, wukvk_ref, wukvv_ref,
                 u_o, v_o, sq_o, sk_o, sv_o, mq_o, mk_o, mv_o):
    x = h_ref[0]
    tm = x.shape[0]
    mod = mod_ref[0]
    sh = mod[:, 0:D_MODEL]
    sc = mod[:, D_MODEL:2 * D_MODEL]
    y = (_rms(x, g1_ref[...]) * (1.0 + sc) + sh).astype(BF16)
    u = jnp.dot(y, w1_ref[...], preferred_element_type=F32)
    ut = lax.dot_general(w1t_ref[...], y, _NT, preferred_element_type=F32)

    cos_h = rope_ref[:, 0:LANE]
    sin_h = rope_ref[:, LANE:2 * LANE]
    cos_m = rope_ref[:, 2 * LANE:3 * LANE]
    sin_m = rope_ref[:, 3 * LANE:4 * LANE]
    cos_mt = ropet_ref[0:LANE, :]
    sin_mt = ropet_ref[LANE:2 * LANE, :]
    cos_ht = ropet_ref[2 * LANE:3 * LANE, :]
    sin_ht = ropet_ref[3 * LANE:4 * LANE, :]

    f = u[:, C_F:C_F + FOURIER_WIDTH].astype(BF16)
    uv = jnp.dot(f, wcs_ref[...], preferred_element_type=F32).astype(BF16)
    u_o[0] = uv[:, :FOURIER_WIDTH]
    v_o[0] = uv[:, FOURIER_WIDTH:]

    def rope_rows(x1, x2, c, s, scale):
        return ((x1 * c - x2 * s) * scale).astype(BF16), ((x2 * c + x1 * s) * scale).astype(BF16)

    half = HEAD_DIM // 2
    c_h, s_h = cos_ht[0:half], sin_ht[0:half]
    for hh in range(SWA_HEADS):
        r0 = R_Q + HEAD_DIM * hh
        lo, hi = rope_rows(ut[r0:r0 + half], ut[r0 + half:r0 + HEAD_DIM], c_h, s_h, SWA_SCALE * LOG2E)
        sq_o[0, HEAD_DIM * hh:HEAD_DIM * hh + half, :] = lo
        sq_o[0, HEAD_DIM * hh + half:HEAD_DIM * (hh + 1), :] = hi
    kk = (u[:, C_K:C_K + LANE] * cos_h + u[:, C_KROT:C_KROT + LANE] * sin_h).astype(BF16)
    ones_swa = _ones_row(SWA_VT_ROWS - HEAD_DIM, tm)
    for kh in range(SWA_KV_HEADS):
        sk_o[0, kh] = kk[:, HEAD_DIM * kh:HEAD_DIM * (kh + 1)]
        sv_o[0, kh, 0:HEAD_DIM, :] = ut[R_V + HEAD_DIM * kh:R_V + HEAD_DIM * (kh + 1)].astype(BF16)
        sv_o[0, kh, HEAD_DIM:SWA_VT_ROWS, :] = ones_swa

    cqn = _rms(u[:, C_CQ:C_CQ + MLA_Q_RANK], qn_ref[...]).astype(BF16)
    qa = lax.dot_general(wuq_ref[...], cqn, _NT, preferred_element_type=F32)
    rh = MLA_ROPE_DIM // 2
    c_m, s_m = cos_mt[MLA_NOPE_DIM:MLA_NOPE_DIM + rh], sin_mt[MLA_NOPE_DIM:MLA_NOPE_DIM + rh]
    q_scale = MLA_SCALE * LOG2E
    for hh in range(MLA_HEADS):
        r0 = MLA_PAD * hh
        r1 = r0 + MLA_NOPE_DIM
        lo, hi = rope_rows(qa[r1:r1 + rh], qa[r1 + rh:r1 + 2 * rh], c_m, s_m, q_scale)
        mq_o[0, r0:r1, :] = (qa[r0:r1] * q_scale).astype(BF16)
        mq_o[0, r1:r1 + rh, :] = lo
        mq_o[0, r1 + rh:r1 + 2 * rh, :] = hi
        mq_o[0, r1 + 2 * rh:r0 + MLA_PAD, :] = jnp.zeros((MLA_PAD - MLA_NOPE_DIM - 2 * rh, tm), BF16)

    ckvn = _rms(u[:, C_CKV:C_CKV + MLA_KV_RANK], kvn_ref[...]).astype(BF16)
    kr_tile = u[:, C_KR:C_KR + LANE]
    kr = kr_tile * cos_m + pltpu.roll(kr_tile, LANE - MLA_ROPE_DIM, 1) * sin_m
    kvk = jnp.dot(ckvn, wukvk_ref[...], preferred_element_type=F32)
    vt = lax.dot_general(wukvv_ref[...], ckvn, _NT, preferred_element_type=F32).astype(BF16)
    chunk = mv_o.shape[4]
    ones_mla = _ones_row(MLA_VT_ROWS - MLA_V_DIM, chunk)
    for hh in range(MLA_HEADS):
        mk_o[0, hh] = (kvk[:, MLA_PAD * hh:MLA_PAD * (hh + 1)] + kr).astype(BF16)
        for c in range(tm // chunk):
            mv_o[0, hh, c, 0:MLA_V_DIM, :] = vt[MLA_V_DIM * hh:MLA_V_DIM * (hh + 1), chunk * c:chunk * (c + 1)]
            mv_o[0, hh, c, MLA_V_DIM:MLA_VT_ROWS, :] = ones_mla


def _proj_call(h, mod3, mod_row, g1, w1, w1t, rope, ropet, wcs, qn, wuq, kvn, wukvk, wukvv, tm):
    b_, s_, _ = h.shape
    chunk = min(MLA_TK, tm)
    const = lambda b, i: (0, 0)
    if mod_row is None:
        mod_map = lambda b, i: (b, 0, 0)
    else:
        mod_map = lambda b, i: (mod_row, 0, 0)
    out_shapes = (
        jax.ShapeDtypeStruct((b_, s_, FOURIER_WIDTH), BF16),
        jax.ShapeDtypeStruct((b_, s_, FOURIER_WIDTH), BF16),
        jax.ShapeDtypeStruct((b_, SWA_Q_WIDTH, s_), BF16),
        jax.ShapeDtypeStruct((b_, SWA_KV_HEADS, s_, HEAD_DIM), BF16),
        jax.ShapeDtypeStruct((b_, SWA_KV_HEADS, SWA_VT_ROWS, s_), BF16),
        jax.ShapeDtypeStruct((b_, MLA_HEADS * MLA_PAD, s_), BF16),
        jax.ShapeDtypeStruct((b_, MLA_HEADS, s_, MLA_PAD), BF16),
        jax.ShapeDtypeStruct((b_, MLA_HEADS, s_ // chunk, MLA_VT_ROWS, chunk), BF16),
    )
    out_specs = (
        pl.BlockSpec((1, tm, FOURIER_WIDTH), lambda b, i: (b, i, 0)),
        pl.BlockSpec((1, tm, FOURIER_WIDTH), lambda b, i: (b, i, 0)),
        pl.BlockSpec((1, SWA_Q_WIDTH, tm), lambda b, i: (b, 0, i)),
        pl.BlockSpec((1, SWA_KV_HEADS, tm, HEAD_DIM), lambda b, i: (b, 0, i, 0)),
        pl.BlockSpec((1, SWA_KV_HEADS, SWA_VT_ROWS, tm), lambda b, i: (b, 0, 0, i)),
        pl.BlockSpec((1, MLA_HEADS * MLA_PAD, tm), lambda b, i: (b, 0, i)),
        pl.BlockSpec((1, MLA_HEADS, tm, MLA_PAD), lambda b, i: (b, 0, i, 0)),
        pl.BlockSpec((1, MLA_HEADS, tm // chunk, MLA_VT_ROWS, chunk), lambda b, i: (b, 0, i, 0, 0)),
    )
    return pl.pallas_call(
        _proj_kernel,
        grid=(b_, s_ // tm),
        in_specs=[
            pl.BlockSpec((1, tm, D_MODEL), lambda b, i: (b, i, 0)),
            pl.BlockSpec((1, 1, 6 * D_MODEL), mod_map),
            pl.BlockSpec((1, D_MODEL), const),
            pl.BlockSpec((D_MODEL, W1_COLS), const),
            pl.BlockSpec((W1T_ROWS, D_MODEL), const),
            pl.BlockSpec((tm, 4 * LANE), lambda b, i: (i, 0)),
            pl.BlockSpec((4 * LANE, tm), lambda b, i: (0, i)),
            pl.BlockSpec((FOURIER_WIDTH, 2 * FOURIER_WIDTH), const),
            pl.BlockSpec((1, MLA_Q_RANK), const),
            pl.BlockSpec((MLA_HEADS * MLA_PAD, MLA_Q_RANK), const),
            pl.BlockSpec((1, MLA_KV_RANK), const),
            pl.BlockSpec((MLA_KV_RANK, MLA_HEADS * MLA_PAD), const),
            pl.BlockSpec((MLA_HEADS * MLA_V_DIM, MLA_KV_RANK), const),
        ],
        out_specs=out_specs,
        out_shape=out_shapes,
        compiler_params=_cparams(("parallel", "parallel")),
        name="norm_in_proj",
    )(h, mod3, g1, w1, w1t, rope, ropet, wcs, qn, wuq, kvn, wukvk, wukvv)


def _dft_kernel(c_ref, s_ref, u_ref, v_ref, o_ref):
    o = jnp.dot(c_ref[...], u_ref[0].astype(BF16), preferred_element_type=F32)
    o = o + jnp.dot(s_ref[...], v_ref[0].astype(BF16), preferred_element_type=F32)
    o_ref[0] = o


def _dft_call(ctab, stab, u, v, tk):
    b_, s_, _ = u.shape
    whole = pl.BlockSpec((1, s_, FOURIER_WIDTH), lambda i, b: (b, 0, 0))
    return pl.pallas_call(
        _dft_kernel,
        grid=(s_ // tk, b_),
        in_specs=[
            pl.BlockSpec((tk, s_), lambda i, b: (i, 0)),
            pl.BlockSpec((tk, s_), lambda i, b: (i, 0)),
            whole, whole,
        ],
        out_specs=pl.BlockSpec((1, tk, FOURIER_WIDTH), lambda i, b: (b, i, 0)),
        out_shape=jax.ShapeDtypeStruct((b_, s_, FOURIER_WIDTH), F32),
        compiler_params=_cparams(("parallel", "arbitrary")),
        name="fourier_dft",
    )(ctab, stab, u, v)


FFT_SUB = 8
FFT_STEP = 2


def _fft1_kernel(m1_ref, m2_ref, tc_ref, ts_ref, u_ref, v_ref, bre_ref, bim_ref):
    n1, _, w = u_ref.shape[1:]
    sub = FFT_SUB
    u_all = u_ref[0].astype(F32)
    v_all = v_ref[0].astype(F32)
    bre, bim = [], []
    for h in range(u_ref.shape[2] // sub):
        js = slice(sub * h, sub * (h + 1))
        xu = u_all[:, js, :].reshape(n1 * sub, w).astype(BF16)
        xv = v_all[:, js, :].reshape(n1 * sub, w).astype(BF16)
        a = jnp.dot(m1_ref[...], xu, preferred_element_type=F32)
        a = a + jnp.dot(m2_ref[...], xv, preferred_element_type=F32)
        are, aim = a[0:n1 * sub], a[n1 * sub:]
        tc, ts = tc_ref[h], ts_ref[h]
        bre.append((are * tc + aim * ts).reshape(n1, sub, w))
        bim.append((aim * tc - are * ts).reshape(n1, sub, w))
    bre_ref[0] = jnp.concatenate(bre, axis=1).astype(bre_ref.dtype)
    bim_ref[0] = jnp.concatenate(bim, axis=1).astype(bim_ref.dtype)


def _fft2_kernel(g1_ref, g2_ref, bre_ref, bim_ref, o_ref):
    n2 = g1_ref.shape[1]
    for j in range(o_ref.shape[2]):
        rows = slice(n2 * j, n2 * (j + 1))
        x = jnp.dot(g1_ref[...], bre_ref[0, rows, :], preferred_element_type=F32)
        x = x + jnp.dot(g2_ref[...], bim_ref[0, rows, :], preferred_element_type=F32)
        o_ref[0, :, j, :] = x


def _fft_call(tabs, u, v):
    m1, m2, tc, ts, g1, g2 = tabs
    b_, s_, w = u.shape
    n1 = m1.shape[1] // FFT_SUB
    n2 = s_ // n1
    u4 = u.reshape(b_, n1, n2, w)
    v4 = v.reshape(b_, n1, n2, w)
    per_step = FFT_STEP * FFT_SUB
    tile = pl.BlockSpec((1, n1, per_step, w), lambda t, b: (b, 0, t, 0))
    tab = pl.BlockSpec((FFT_STEP, n1 * FFT_SUB, w), lambda t, b: (t, 0, 0))
    mat = pl.BlockSpec((2 * n1 * FFT_SUB, n1 * FFT_SUB), lambda t, b: (0, 0))
    bre, bim = pl.pallas_call(
        _fft1_kernel,
        grid=(n2 // per_step, b_),
        in_specs=[mat, mat, tab, tab, tile, tile],
        out_specs=(tile, tile),
        out_shape=(jax.ShapeDtypeStruct((b_, n1, n2, w), BF16),) * 2,
        compiler_params=_cparams(("parallel", "arbitrary")),
        name="fourier_fft1",
    )(m1, m2, tc, ts, u4, v4)
    blk = pl.BlockSpec((1, per_step * n2, w), lambda b, t: (b, t, 0))
    mat2 = pl.BlockSpec((n2, n2), lambda b, t: (0, 0))
    out = pl.pallas_call(
        _fft2_kernel,
        grid=(b_, n1 // per_step),
        in_specs=[mat2, mat2, blk, blk],
        out_specs=pl.BlockSpec((1, n2, per_step, w), lambda b, t: (b, 0, t, 0)),
        out_shape=jax.ShapeDtypeStruct((b_, n2, n1, w), F32),
        compiler_params=_cparams(("parallel", "parallel")),
        name="fourier_fft2",
    )(g1, g2, bre.reshape(b_, s_, w), bim.reshape(b_, s_, w))
    return out.reshape(b_, s_, w)


def _fft_tables(s_, n1, w):
    n2 = s_ // n1
    k1 = np.arange(n1)
    a1 = ((k1[:, None] * k1[None, :]) % n1).astype(np.float64) * (2.0 * math.pi / n1)
    c1, s1 = np.cos(a1), np.sin(a1)
    eye = np.eye(FFT_SUB)
    expand = lambda a: jnp.asarray(np.kron(a, eye), dtype=F32).astype(BF16)
    m1 = expand(np.concatenate([c1, -s1], axis=0))
    m2 = expand(np.concatenate([-s1, -c1], axis=0))
    s2 = jnp.arange(n2, dtype=jnp.int32)
    n_t = n2 // FFT_SUB
    row = jnp.arange(n1 * FFT_SUB, dtype=jnp.int32)
    s2_of = FFT_SUB * jnp.arange(n_t, dtype=jnp.int32)[:, None] + (row % FFT_SUB)[None, :]
    th = (((row // FFT_SUB)[None, :] * s2_of) % s_).astype(F32) * (2.0 * math.pi / s_)
    tc = jnp.broadcast_to(jnp.cos(th)[:, :, None], (n_t, n1 * FFT_SUB, w))
    ts = jnp.broadcast_to(jnp.sin(th)[:, :, None], (n_t, n1 * FFT_SUB, w))
    a2 = ((s2[:, None] * s2[None, :]) % n2).astype(F32) * (2.0 * math.pi / n2)
    scale = 1.0 / math.sqrt(s_)
    g1 = (jnp.cos(a2) * scale).astype(BF16)
    g2 = (jnp.sin(a2) * scale).astype(BF16)
    return m1, m2, tc, ts, g1, g2


def _mla_kernel(*refs, n_chunks, heads_per_step):
    hps = heads_per_step
    n_in = 6 if n_chunks else 4
    if n_chunks:
        zero_ref, qt_ref, k_ref, vt_ref, kc_ref, vct_ref = refs[:n_in]
    else:
        zero_ref, qt_ref, kc_ref, vct_ref = refs[:n_in]
    o_ref = refs[n_in]
    row0 = pl.multiple_of(zero_ref[0], MLA_TK)
    scratch = refs[n_in + 1:]
    tq = scratch[0].shape[1]
    n_sub = qt_ref.shape[2] // tq
    lc = kc_ref.shape[2]
    s_refs = [[scratch[MLA_S_SLOTS * (hps * u + hh):MLA_S_SLOTS * (hps * u + hh + 1)] for hh in range(hps)]
              for u in range(n_sub)]

    def keys(idx):
        if idx == 0:
            return (lambda hh: kc_ref[0, hh]), lc
        return (lambda hh: k_ref[0, hh, (idx - 1) * MLA_TK:idx * MLA_TK, :]), MLA_TK

    def values(idx):
        if idx == 0:
            return lambda hh: vct_ref[0, hh, 0]
        return lambda hh: vt_ref[0, hh, idx - 1]

    def logits(u, idx):
        ks, rows = keys(idx)
        cms = []
        for hh in range(hps):
            qt = qt_ref[0, MLA_PAD * hh:MLA_PAD * (hh + 1), tq * u:tq * (u + 1)]
            s = jnp.dot(ks(hh), qt, preferred_element_type=F32)
            s_refs[u][hh][idx % MLA_S_SLOTS][0:rows, :] = s
            cms.append(jnp.max(s, axis=0, keepdims=True))
        return tuple(cms)

    def accumulate(u, idx, carries, cms):
        _, rows = keys(idx)
        vts = values(idx)
        out = []
        for hh in range(hps):
            m, acc = carries[hh]
            m_new = jnp.maximum(m, cms[hh])
            alpha = jnp.exp2(m - m_new)
            s = s_refs[u][hh][idx % MLA_S_SLOTS][pl.ds(row0, rows), :]
            p = jnp.exp2(s - m_new).astype(BF16)
            acc = alpha * acc + jnp.dot(vts(hh), p, preferred_element_type=F32)
            out.append((m_new, acc))
        return tuple(out)

    init = (jnp.full((1, tq), NEG_INF, F32), jnp.zeros((MLA_VT_ROWS, tq), F32))
    carries = [tuple(init for _ in range(hps)) for _ in range(n_sub)]
    pending = [None] * n_sub
    n_idx = n_chunks + 1
    for t in range(n_idx + n_sub):
        for u in range(n_sub):
            idx = t - u
            new_cms = logits(u, idx) if 0 <= idx < n_idx else None
            if 1 <= idx <= n_idx:
                carries[u] = accumulate(u, idx - 1, carries[u], pending[u])
            pending[u] = new_cms
    for u in range(n_sub):
        o = jnp.concatenate([acc[0:MLA_V_DIM] / acc[MLA_V_DIM:MLA_V_DIM + 1] for (_, acc) in carries[u]],
                            axis=0)
        o_ref[0, tq * u:tq * (u + 1), :] = o.T.astype(o_ref.dtype)


def _mla_call(qt, k, vt, kc, vct, tq, n_sub):
    b_, _, sq = qt.shape
    h_ = kc.shape[1]
    lc = kc.shape[2]
    hps = 2
    n_chunks = 0 if k is None else k.shape[2] // MLA_TK
    assert n_chunks == 0 or lc <= MLA_TK
    in_specs = [pl.BlockSpec(memory_space=pltpu.SMEM),
                pl.BlockSpec((1, hps * MLA_PAD, tq), lambda b, p, i: (b, p, i))]
    args = [jnp.zeros((1,), jnp.int32), qt]
    if k is not None:
        sk = k.shape[2]
        in_specs += [pl.BlockSpec((1, hps, sk, MLA_PAD), lambda b, p, i: (b, p, 0, 0)),
                     pl.BlockSpec((1, hps, n_chunks, MLA_VT_ROWS, MLA_TK), lambda b, p, i: (b, p, 0, 0, 0))]
        args += [k, vt]
    in_specs += [pl.BlockSpec((1, hps, lc, MLA_PAD), lambda b, p, i: (b, p, 0, 0)),
                 pl.BlockSpec((1, hps, 1, MLA_VT_ROWS, lc), lambda b, p, i: (b, p, 0, 0, 0))]
    args += [kc, vct]
    return pl.pallas_call(
        functools.partial(_mla_kernel, n_chunks=n_chunks, heads_per_step=hps),
        grid=(b_, h_ // hps, sq // tq),
        in_specs=in_specs,
        out_specs=pl.BlockSpec((1, tq, hps * MLA_V_DIM), lambda b, p, i: (b, i, p)),
        out_shape=jax.ShapeDtypeStruct((b_, sq, h_ * MLA_V_DIM), BF16),
        scratch_shapes=[pltpu.VMEM((MLA_TK if n_chunks else lc, tq // n_sub), F32)] * (MLA_S_SLOTS * hps * n_sub),
        compiler_params=_cparams(("parallel", "parallel", "arbitrary")),
        name="mla_attn",
    )(*args)


def _swa_kernel(*refs, n_local, s_len):
    zero_ref, sink_ref, qt_ref = refs[:3]
    k_refs = refs[3:3 + n_local]
    vt_refs = refs[3 + n_local:3 + 2 * n_local]
    kc_ref, vct_ref, o_ref = refs[3 + 2 * n_local:6 + 2 * n_local]
    s_refs = refs[6 + 2 * n_local:]
    row0 = pl.multiple_of(zero_ref[0], BLOCK)
    tq = s_refs[0].shape[1]
    step_tq = qt_ref.shape[2]
    n_sub = step_tq // tq
    lc = kc_ref.shape[2]
    sub_blocks = tq // BLOCK + 2 if n_local else 0
    nk = sub_blocks * BLOCK
    sinks = [sink_ref[hh] * LOG2E for hh in range(SWA_HEADS)]

    def mask_bias(u):
        q0 = pl.program_id(1) * step_tq + u * tq
        kpos = q0 - BLOCK + lax.broadcasted_iota(jnp.int32, (nk, tq), 0)
        qpos = q0 + lax.broadcasted_iota(jnp.int32, (nk, tq), 1)
        valid = (jnp.abs(kpos - qpos) <= WINDOW) & (kpos >= 0) & (kpos < s_len)
        return jnp.where(valid, 0.0, NEG_INF).astype(F32)

    def keys(u, kh):
        kc = kc_ref[0, kh]
        vct = vct_ref[0, kh]
        if not n_local:
            return kc, vct
        first = u * (tq // BLOCK)
        blocks = range(first, first + sub_blocks)
        return (jnp.concatenate([kc] + [k_refs[j][0, kh] for j in blocks], axis=0),
                jnp.concatenate([vct] + [vt_refs[j][0, kh] for j in blocks], axis=1))

    def logits(unit, u, hh, k_all, bias):
        qt = qt_ref[0, HEAD_DIM * hh:HEAD_DIM * (hh + 1), tq * u:tq * (u + 1)]
        s_ref = s_refs[unit % 2]
        s_c = jnp.dot(k_all[0:lc], qt, preferred_element_type=F32)
        s_ref[0:lc, :] = s_c
        m = jnp.maximum(jnp.max(s_c, axis=0, keepdims=True), sinks[hh])
        if n_local:
            s_l = jnp.dot(k_all[lc:], qt, preferred_element_type=F32) + bias
            s_ref[lc:lc + nk, :] = s_l
            m = jnp.maximum(m, jnp.max(s_l, axis=0, keepdims=True))
        return m

    def accumulate(unit, hh, vt_all, m):
        p = jnp.exp2(s_refs[unit % 2][pl.ds(row0, lc + nk), :] - m).astype(BF16)
        acc = jnp.dot(vt_all, p, preferred_element_type=F32)
        l = acc[HEAD_DIM:HEAD_DIM + 1] + jnp.exp2(sinks[hh] - m)
        return acc[0:HEAD_DIM] / l

    units = [(u, hh) for u in range(n_sub) for hh in range(SWA_HEADS)]
    kv = {(u, kh): keys(u, kh) for u in range(n_sub) for kh in range(SWA_KV_HEADS)}
    biases = [mask_bias(u) if n_local else None for u in range(n_sub)]
    outs = [[] for _ in range(n_sub)]
    m_prev = None
    for unit in range(len(units) + 1):
        m_next = None
        if unit < len(units):
            u, hh = units[unit]
            m_next = logits(unit, u, hh, kv[(u, hh // SWA_GROUP)][0], biases[u])
        if unit > 0:
            u, hh = units[unit - 1]
            outs[u].append(accumulate(unit - 1, hh, kv[(u, hh // SWA_GROUP)][1], m_prev))
        m_prev = m_next
    for u in range(n_sub):
        o = jnp.concatenate(outs[u], axis=0)
        o_ref[0, tq * u:tq * (u + 1), :] = o.T.astype(o_ref.dtype)


def _swa_call(sink, qt, k, vt, kc, vct):
    b_, _, sq = qt.shape
    lc = kc.shape[2]
    local = k is not None
    sub_tq = SWA_TQ if local else sq
    tq = sub_tq * SWA_SUB_TILES if local else sq
    nb = sq // tq
    n_local = tq // BLOCK + 2 if local else 0
    sub_keys = lc + (sub_tq // BLOCK + 2) * BLOCK if local else lc
    in_specs = [pl.BlockSpec(memory_space=pltpu.SMEM), pl.BlockSpec(memory_space=pltpu.SMEM),
                pl.BlockSpec((1, SWA_Q_WIDTH, tq), lambda b, n: (b, 0, n))]
    args = [jnp.zeros((1,), jnp.int32), sink, qt]
    if local:
        nkb = sq // BLOCK
        per = tq // BLOCK

        def blk_idx(n, j):
            return jnp.clip(n * per - 1 + j, 0, nkb - 1)

        in_specs += [pl.BlockSpec((1, SWA_KV_HEADS, BLOCK, HEAD_DIM),
                                  functools.partial(lambda b, n, j: (b, 0, blk_idx(n, j), 0), j=j))
                     for j in range(n_local)]
        in_specs += [pl.BlockSpec((1, SWA_KV_HEADS, SWA_VT_ROWS, BLOCK),
                                  functools.partial(lambda b, n, j: (b, 0, 0, blk_idx(n, j)), j=j))
                     for j in range(n_local)]
        args += [k] * n_local + [vt] * n_local
    in_specs += [pl.BlockSpec((1, SWA_KV_HEADS, lc, HEAD_DIM), lambda b, n: (b, 0, 0, 0)),
                 pl.BlockSpec((1, SWA_KV_HEADS, SWA_VT_ROWS, lc), lambda b, n: (b, 0, 0, 0))]
    args += [kc, vct]
    return pl.pallas_call(
        functools.partial(_swa_kernel, n_local=n_local, s_len=sq),
        grid=(b_, nb),
        in_specs=in_specs,
        out_specs=pl.BlockSpec((1, tq, SWA_Q_WIDTH), lambda b, n: (b, n, 0)),
        out_shape=jax.ShapeDtypeStruct((b_, sq, SWA_Q_WIDTH), BF16),
        scratch_shapes=[pltpu.VMEM((sub_keys, sub_tq), F32)] * 2,
        compiler_params=_cparams(("parallel", "arbitrary")),
        name="swa_attn",
    )(*args)


def _mlp_kernel(fo_ref, so_ref, mo_ref, h_ref, mod_ref, g2_ref, wo_ref, w1_ref, w2_ref, gf_ref, o_ref, *,
                final, ff_chunk):
    mod = mod_ref[0]
    g1 = mod[:, 2 * D_MODEL:3 * D_MODEL]
    sh2 = mod[:, 3 * D_MODEL:4 * D_MODEL]
    sc2 = mod[:, 4 * D_MODEL:5 * D_MODEL]
    g2 = mod[:, 5 * D_MODEL:6 * D_MODEL]
    mix = jnp.concatenate([fo_ref[0].astype(BF16), so_ref[0], mo_ref[0]], axis=-1)
    h = h_ref[0] + g1 * jnp.dot(mix, wo_ref[...], preferred_element_type=F32)
    y = (_rms(h, g2_ref[...]) * (1.0 + sc2) + sh2).astype(BF16)
    acc = None
    for c in range(D_FF // ff_chunk):
        a = jnp.dot(y, w1_ref[:, c * ff_chunk:(c + 1) * ff_chunk], preferred_element_type=F32)
        a = jnp.maximum(a, 0.0)
        a = (a * a).astype(BF16)
        part = jnp.dot(a, w2_ref[c * ff_chunk:(c + 1) * ff_chunk, :], preferred_element_type=F32)
        acc = part if acc is None else acc + part
    h = h + g2 * acc
    if final:
        h = _rms(h, gf_ref[...])
    o_ref[0] = h


def _mlp_call(fo, so, mo, h, mod3, mod_row, g2, wo, w1, w2, gf, tm, final):
    b_, s_, _ = h.shape
    const = lambda b, i: (0, 0)
    tile = lambda b, i: (b, i, 0)
    if mod_row is None:
        mod_map = lambda b, i: (b, 0, 0)
    else:
        mod_map = lambda b, i: (mod_row, 0, 0)
    once = pl.Buffered(1)
    return pl.pallas_call(
        functools.partial(_mlp_kernel, final=final, ff_chunk=1024),
        grid=(b_, s_ // tm),
        in_specs=[
            pl.BlockSpec((1, tm, FOURIER_WIDTH), tile),
            pl.BlockSpec((1, tm, SWA_Q_WIDTH), tile),
            pl.BlockSpec((1, tm, MLA_HEADS * MLA_V_DIM), tile),
            pl.BlockSpec((1, tm, D_MODEL), tile),
            pl.BlockSpec((1, 1, 6 * D_MODEL), mod_map),
            pl.BlockSpec((1, D_MODEL), const),
            pl.BlockSpec((D_MIX, D_MODEL), const, pipeline_mode=once),
            pl.BlockSpec((D_MODEL, D_FF), const, pipeline_mode=once),
            pl.BlockSpec((D_FF, D_MODEL), const, pipeline_mode=once),
            pl.BlockSpec((1, D_MODEL), const),
        ],
        out_specs=pl.BlockSpec((1, tm, D_MODEL), tile),
        out_shape=jax.ShapeDtypeStruct((b_, s_, D_MODEL), F32),
        compiler_params=_cparams(("parallel", "parallel")),
        name="out_proj_mlp",
    )(fo, so, mo, h, mod3, g2, wo, w1, w2, gf)


def _rot_cols(w, heads, dim):
    w4 = w.reshape(w.shape[0], heads, 2, dim // 2)
    return jnp.stack([-w4[:, :, 1], w4[:, :, 0]], axis=2).reshape(w.shape[0], heads * dim)


def _pad_lanes(w, heads, before, after):
    w3 = w.reshape(w.shape[0], heads, w.shape[1] // heads)
    w3 = jnp.pad(w3, ((0, 0), (0, 0), (before, after)))
    return w3.reshape(w.shape[0], -1)


def _pack_w_in(w_in):
    wk = w_in[:, OFF_SWA_K:OFF_SWA_V]
    wkr = w_in[:, OFF_MLA_KR:OFF_MLA_KR + MLA_ROPE_DIM]
    kr_tile = jnp.concatenate([jnp.zeros((D_MODEL, MLA_NOPE_DIM), F32), wkr, _rot_cols(wkr, 1, MLA_ROPE_DIM)],
                              axis=1)
    cols = [w_in[:, :OFF_SWA_Q], wk, w_in[:, OFF_MLA_CQ:OFF_MLA_KR], kr_tile,
            _rot_cols(wk, SWA_KV_HEADS, HEAD_DIM)]
    rows = w_in[:, OFF_SWA_Q:OFF_SWA_K], w_in[:, OFF_SWA_V:OFF_MLA_CQ]
    return jnp.concatenate(cols, axis=1).astype(BF16), jnp.concatenate(rows, axis=1).T.astype(BF16)


def _pack_w_uq(w_uq):
    per = MLA_NOPE_DIM + MLA_ROPE_DIM
    return _pad_lanes(w_uq, MLA_HEADS, 0, MLA_PAD - per).T.astype(BF16)


def _pack_w_ukv(w_ukv):
    w3 = w_ukv.reshape(MLA_KV_RANK, MLA_HEADS, MLA_NOPE_DIM + MLA_V_DIM)
    wk = w3[:, :, :MLA_NOPE_DIM].reshape(MLA_KV_RANK, MLA_HEADS * MLA_NOPE_DIM)
    wv = w3[:, :, MLA_NOPE_DIM:].reshape(MLA_KV_RANK, MLA_HEADS * MLA_V_DIM)
    return _pad_lanes(wk, MLA_HEADS, 0, MLA_PAD - MLA_NOPE_DIM).astype(BF16), wv.T.astype(BF16)


def _rope_table(rows):
    r, col = jnp.meshgrid(jnp.arange(rows, dtype=F32), jnp.arange(GRID_W, dtype=F32), indexing="ij")
    r = r.reshape(-1)
    col = col.reshape(-1)

    def tables(dim):
        n_freq = dim // 4
        inv = ROPE_THETA ** (-jnp.arange(n_freq, dtype=F32) / n_freq)
        ang = jnp.concatenate([r[:, None] * inv[None, :], col[:, None] * inv[None, :]], axis=-1)
        return jnp.cos(ang), jnp.sin(ang)

    ch, sh = tables(HEAD_DIM)
    cr, sr = tables(MLA_ROPE_DIM)
    mla_pad = ((0, 0), (MLA_NOPE_DIM, LANE - MLA_NOPE_DIM - MLA_ROPE_DIM))
    nope_ones = (jnp.arange(LANE) < MLA_NOPE_DIM).astype(F32)[None, :]
    return jnp.concatenate([jnp.tile(ch, (1, 4)), jnp.tile(sh, (1, 4)),
                            jnp.pad(jnp.tile(cr, (1, 2)), mla_pad) + nope_ones,
                            jnp.pad(jnp.tile(sr, (1, 2)), mla_pad)], axis=-1)


def _identity_rope_table(n):
    ones = jnp.ones((n, LANE), F32)
    zeros = jnp.zeros((n, LANE), F32)
    m = jnp.concatenate([jnp.ones((n, MLA_NOPE_DIM + MLA_ROPE_DIM), F32),
                         jnp.zeros((n, LANE - MLA_NOPE_DIM - MLA_ROPE_DIM), F32)], axis=-1)
    return jnp.concatenate([ones, zeros, m, zeros], axis=-1)


def _dft_tables(n):
    k = jnp.arange(n, dtype=jnp.int32)
    ks = (k[:, None] * k[None, :]) % n
    ang = ks.astype(F32) * (2.0 * math.pi / n)
    scale = 1.0 / math.sqrt(n)
    return (jnp.cos(ang) * scale).astype(BF16), (-jnp.sin(ang) * scale).astype(BF16)


def kernel(x, c, ctx, c_ctx, w_ada, b_ada, norm1_g, norm2_g, w_in, w_fourier, swa_sink, mla_q_norm, w_uq,
           mla_kv_norm, w_ukv, w_out, w_mlp1, w_mlp2, final_norm_g):
    b_, s_, _ = x.shape
    lc = ctx.shape[1]
    depth = w_ada.shape[0]
    rows = s_ // GRID_W

    mod_rows = 16
    cc = jnp.concatenate([c, c_ctx[None, :], jnp.zeros((mod_rows - b_ - 1, D_MODEL), F32)], axis=0)
    mod_all = _ada_call(cc, w_ada, b_ada)

    rope = _rope_table(rows)
    rope_c = _identity_rope_table(lc)
    to_feature_major = lambda t: jnp.concatenate([t[:, 2 * LANE:], t[:, :2 * LANE]], axis=1).T
    ropet = to_feature_major(rope)
    ropet_c = to_feature_major(rope_c)
    fft_tabs = _fft_tables(s_, GRID_W, FOURIER_WIDTH)
    ctab_c, stab_c = _dft_tables(lc)
    kk = jnp.arange(FOURIER_GROUP_DIM, dtype=jnp.int32)
    ang64 = ((kk[:, None] * kk[None, :]) % FOURIER_GROUP_DIM).astype(F32) * (2.0 * math.pi / FOURIER_GROUP_DIM)
    c64, s64 = jnp.cos(ang64), jnp.sin(ang64)

    h, hc = x, ctx
    for l in range(depth):
        last = l == depth - 1
        mod3 = mod_all[l].reshape(mod_rows, 1, 6 * D_MODEL)
        g1 = norm1_g[l].reshape(1, D_MODEL)
        g2 = norm2_g[l].reshape(1, D_MODEL)
        gf = final_norm_g.reshape(1, D_MODEL)
        w1, w1t = _pack_w_in(w_in[l])
        wuq = _pack_w_uq(w_uq[l])
        wukvk, wukvv = _pack_w_ukv(w_ukv[l])
        wcs = _fw_call(c64, s64, w_fourier[l]).astype(BF16)
        qn = mla_q_norm[l].reshape(1, MLA_Q_RANK)
        kvn = mla_kv_norm[l].reshape(1, MLA_KV_RANK)
        wo = w_out[l].astype(BF16)
        wm1 = w_mlp1[l].astype(BF16)
        wm2 = w_mlp2[l].astype(BF16)
        sink = swa_sink[l]

        fu, fv, sq, sk, sv, mq, mk, mv = _proj_call(h, mod3, None, g1, w1, w1t, rope, ropet, wcs, qn, wuq,
                                                    kvn, wukvk, wukvv, tm=1024)
        fuc, fvc, sqc, skc, svc, mqc, mkc, mvc = _proj_call(hc, mod3, b_, g1, w1, w1t, rope_c, ropet_c, wcs, qn,
                                                            wuq, kvn, wukvk, wukvv, tm=lc)

        fo = _fft_call(fft_tabs, fu, fv)
        so = _swa_call(sink, sq, sk, sv, skc, svc)
        mo = _mla_call(mq, mk, mv, mkc, mvc, tq=2048, n_sub=4)
        h = _mlp_call(fo, so, mo, h, mod3, None, g2, wo, wm1, wm2, gf, tm=512, final=last)

        if not last:
            foc = _dft_call(ctab_c, stab_c, fuc, fvc, tk=lc)
            soc = _swa_call(sink, sqc, None, None, skc, svc)
            moc = _mla_call(mqc, None, None, mkc, mvc, tq=lc, n_sub=1)
            flat = lambda t: t.reshape(1, b_ * lc, t.shape[-1])
            hc = _mlp_call(flat(foc), flat(soc), flat(moc), flat(hc), mod3, b_, g2, wo, wm1, wm2, gf,
                           tm=512, final=False).reshape(b_, lc, D_MODEL)
    return h
```

```python
import functools
import math

import jax
import jax.numpy as jnp
import numpy as np
from jax import lax
from jax.experimental import pallas as pl
from jax.experimental.pallas import tpu as pltpu

D_MODEL = 1024
GRID_W = 64
HEAD_DIM = 64
FOURIER_GROUPS = 4
FOURIER_GROUP_DIM = 64
FOURIER_WIDTH = FOURIER_GROUPS * FOURIER_GROUP_DIM
SWA_HEADS = 6
SWA_KV_HEADS = 2
SWA_GROUP = SWA_HEADS // SWA_KV_HEADS
SWA_Q_WIDTH = SWA_HEADS * HEAD_DIM
SWA_KV_WIDTH = SWA_KV_HEADS * HEAD_DIM
WINDOW = 128
BLOCK = 128
MLA_HEADS = 6
MLA_NOPE_DIM = 64
MLA_ROPE_DIM = 32
MLA_V_DIM = 64
MLA_Q_RANK = 256
MLA_KV_RANK = 128
MLA_SCALE = (MLA_NOPE_DIM + MLA_ROPE_DIM) ** -0.5
SWA_SCALE = HEAD_DIM ** -0.5
D_MIX = FOURIER_WIDTH + SWA_Q_WIDTH + MLA_HEADS * MLA_V_DIM
OFF_SWA_Q = FOURIER_WIDTH
OFF_SWA_K = OFF_SWA_Q + SWA_Q_WIDTH
OFF_SWA_V = OFF_SWA_K + SWA_KV_WIDTH
OFF_MLA_CQ = OFF_SWA_V + SWA_KV_WIDTH
OFF_MLA_CKV = OFF_MLA_CQ + MLA_Q_RANK
OFF_MLA_KR = OFF_MLA_CKV + MLA_KV_RANK
D_FF = 4 * D_MODEL
ROPE_THETA = 10000.0
NORM_EPS = 1e-6
NEG_INF = -1e30

LANE = 128
MLA_PAD = LANE

C_F = 0
C_K = C_F + FOURIER_WIDTH
C_CQ = C_K + SWA_KV_WIDTH
C_CKV = C_CQ + MLA_Q_RANK
C_KR = C_CKV + MLA_KV_RANK
C_KROT = C_KR + LANE
W1_COLS = C_KROT + SWA_KV_WIDTH
R_Q = 0
R_V = R_Q + SWA_Q_WIDTH
W1T_ROWS = R_V + SWA_KV_WIDTH

BF16 = jnp.bfloat16
F32 = jnp.float32
LOG2E = math.log2(math.e)
_NT = (((1,), (1,)), ((), ()))
MLA_TK = 256
MLA_VT_ROWS = MLA_V_DIM + 16
MLA_S_SLOTS = 2
SWA_VT_ROWS = HEAD_DIM + 16
SWA_TQ = 256
SWA_SUB_TILES = 4
SWA_UNIT_HEADS = 2

VMEM_LIMIT = 56 * 1024 * 1024


def _cparams(sem):
    return pltpu.CompilerParams(dimension_semantics=sem, vmem_limit_bytes=VMEM_LIMIT)


def _rms(x, g):
    ms = jnp.mean(x * x, axis=-1, keepdims=True)
    return x * lax.rsqrt(ms + NORM_EPS) * g


def _ada_kernel(cc_ref, w_ref, b_ref, o_ref):
    cc = cc_ref[...]
    s = cc * jax.nn.sigmoid(cc)
    o_ref[0] = jnp.dot(s, w_ref[0], preferred_element_type=F32,
                       precision=lax.Precision.HIGHEST) + b_ref[0]


def _ada_call(cc, w_ada, b_ada):
    depth = w_ada.shape[0]
    rows = cc.shape[0]
    tn = 1536
    n_out = w_ada.shape[2]
    return pl.pallas_call(
        _ada_kernel,
        grid=(depth, n_out // tn),
        in_specs=[
            pl.BlockSpec((rows, D_MODEL), lambda l, j: (0, 0)),
            pl.BlockSpec((1, D_MODEL, tn), lambda l, j: (l, 0, j)),
            pl.BlockSpec((1, 1, tn), lambda l, j: (l, 0, j)),
        ],
        out_specs=pl.BlockSpec((1, rows, tn), lambda l, j: (l, 0, j)),
        out_shape=jax.ShapeDtypeStruct((depth, rows, n_out), F32),
        compiler_params=_cparams(("arbitrary", "arbitrary")),
        name="ada_mod",
    )(cc, w_ada, b_ada.reshape(depth, 1, n_out))


def _fw_kernel(c_ref, s_ref, w_ref, o_ref):
    o_ref[...] = jnp.zeros(o_ref.shape, o_ref.dtype)
    inv = 1.0 / math.sqrt(FOURIER_GROUP_DIM)
    for g in range(FOURIER_GROUPS):
        w = w_ref[g]
        a = jnp.dot(c_ref[...], w, preferred_element_type=F32, precision=lax.Precision.HIGHEST) * inv
        b = jnp.dot(s_ref[...], w, preferred_element_type=F32, precision=lax.Precision.HIGHEST) * inv
        lo = g * FOURIER_GROUP_DIM
        hi = lo + FOURIER_GROUP_DIM
        o_ref[lo:hi, lo:hi] = a
        o_ref[lo:hi, FOURIER_WIDTH + lo:FOURIER_WIDTH + hi] = b


def _fw_call(c64, s64, w_f):
    return pl.pallas_call(
        _fw_kernel,
        out_shape=jax.ShapeDtypeStruct((FOURIER_WIDTH, 2 * FOURIER_WIDTH), F32),
        name="fourier_w",
    )(c64, s64, w_f)


def _ones_row(rows, cols):
    return (lax.broadcasted_iota(jnp.int32, (rows, cols), 0) == 0).astype(BF16)


def _proj_kernel(h_ref, mod_ref, g1_ref, w1_ref, w1t_ref, rope_ref, ropet_ref, wcs_ref, qn_ref, wuq_ref,
                 kvn_ref, wukvk_ref, wukvv_ref,
                 u_o, v_o, sq_o, sk_o, sv_o, mq_o, mk_o, mv_o):
    x = h_ref[0]
    tm = x.shape[0]
    mod = mod_ref[0]
    sh = mod[:, 0:D_MODEL]
    sc = mod[:, D_MODEL:2 * D_MODEL]
    y = (_rms(x, g1_ref[...]) * (1.0 + sc) + sh).astype(BF16)
    u = jnp.dot(y, w1_ref[...], preferred_element_type=F32)
    ut = lax.dot_general(w1t_ref[...], y, _NT, preferred_element_type=F32)

    cos_h = rope_ref[:, 0:LANE]
    sin_h = rope_ref[:, LANE:2 * LANE]
    cos_m = rope_ref[:, 2 * LANE:3 * LANE]
    sin_m = rope_ref[:, 3 * LANE:4 * LANE]
    cos_mt = ropet_ref[0:LANE, :]
    sin_mt = ropet_ref[LANE:2 * LANE, :]
    cos_ht = ropet_ref[2 * LANE:3 * LANE, :]
    sin_ht = ropet_ref[3 * LANE:4 * LANE, :]

    f = u[:, C_F:C_F + FOURIER_WIDTH].astype(BF16)
    uv = jnp.dot(f, wcs_ref[...], preferred_element_type=F32).astype(BF16)
    u_o[0] = uv[:, :FOURIER_WIDTH]
    v_o[0] = uv[:, FOURIER_WIDTH:]

    def rope_rows(x1, x2, c, s, scale):
        return ((x1 * c - x2 * s) * scale).astype(BF16), ((x2 * c + x1 * s) * scale).astype(BF16)

    half = HEAD_DIM // 2
    c_h, s_h = cos_ht[0:half], sin_ht[0:half]
    for hh in range(SWA_HEADS):
        r0 = R_Q + HEAD_DIM * hh
        lo, hi = rope_rows(ut[r0:r0 + half], ut[r0 + half:r0 + HEAD_DIM], c_h, s_h, SWA_SCALE * LOG2E)
        sq_o[0, HEAD_DIM * hh:HEAD_DIM * hh + half, :] = lo
        sq_o[0, HEAD_DIM * hh + half:HEAD_DIM * (hh + 1), :] = hi
    kk = (u[:, C_K:C_K + LANE] * cos_h + u[:, C_KROT:C_KROT + LANE] * sin_h).astype(BF16)
    ones_swa = _ones_row(SWA_VT_ROWS - HEAD_DIM, tm)
    for kh in range(SWA_KV_HEADS):
        sk_o[0, kh] = kk[:, HEAD_DIM * kh:HEAD_DIM * (kh + 1)]
        sv_o[0, kh, 0:HEAD_DIM, :] = ut[R_V + HEAD_DIM * kh:R_V + HEAD_DIM * (kh + 1)].astype(BF16)
        sv_o[0, kh, HEAD_DIM:SWA_VT_ROWS, :] = ones_swa

    cqn = _rms(u[:, C_CQ:C_CQ + MLA_Q_RANK], qn_ref[...]).astype(BF16)
    qa = lax.dot_general(wuq_ref[...], cqn, _NT, preferred_element_type=F32)
    rh = MLA_ROPE_DIM // 2
    c_m, s_m = cos_mt[MLA_NOPE_DIM:MLA_NOPE_DIM + rh], sin_mt[MLA_NOPE_DIM:MLA_NOPE_DIM + rh]
    q_scale = MLA_SCALE * LOG2E
    for hh in range(MLA_HEADS):
        r0 = MLA_PAD * hh
        r1 = r0 + MLA_NOPE_DIM
        lo, hi = rope_rows(qa[r1:r1 + rh], qa[r1 + rh:r1 + 2 * rh], c_m, s_m, q_scale)
        mq_o[0, r0:r1, :] = (qa[r0:r1] * q_scale).astype(BF16)
        mq_o[0, r1:r1 + rh, :] = lo
        mq_o[0, r1 + rh:r1 + 2 * rh, :] = hi
        mq_o[0, r1 + 2 * rh:r0 + MLA_PAD, :] = jnp.zeros((MLA_PAD - MLA_NOPE_DIM - 2 * rh, tm), BF16)

    ckvn = _rms(u[:, C_CKV:C_CKV + MLA_KV_RANK], kvn_ref[...]).astype(BF16)
    kr_tile = u[:, C_KR:C_KR + LANE]
    kr = kr_tile * cos_m + pltpu.roll(kr_tile, LANE - MLA_ROPE_DIM, 1) * sin_m
    kvk = jnp.dot(ckvn, wukvk_ref[...], preferred_element_type=F32)
    vt = lax.dot_general(wukvv_ref[...], ckvn, _NT, preferred_element_type=F32).astype(BF16)
    chunk = mv_o.shape[4]
    ones_mla = _ones_row(MLA_VT_ROWS - MLA_V_DIM, chunk)
    for hh in range(MLA_HEADS):
        mk_o[0, hh] = (kvk[:, MLA_PAD * hh:MLA_PAD * (hh + 1)] + kr).astype(BF16)
        for c in range(tm // chunk):
            mv_o[0, hh, c, 0:MLA_V_DIM, :] = vt[MLA_V_DIM * hh:MLA_V_DIM * (hh + 1), chunk * c:chunk * (c + 1)]
            mv_o[0, hh, c, MLA_V_DIM:MLA_VT_ROWS, :] = ones_mla


def _proj_call(h, mod3, mod_row, g1, w1, w1t, rope, ropet, wcs, qn, wuq, kvn, wukvk, wukvv, tm):
    b_, s_, _ = h.shape
    chunk = min(MLA_TK, tm)
    const = lambda b, i: (0, 0)
    if mod_row is None:
        mod_map = lambda b, i: (b, 0, 0)
    else:
        mod_map = lambda b, i: (mod_row, 0, 0)
    out_shapes = (
        jax.ShapeDtypeStruct((b_, s_, FOURIER_WIDTH), BF16),
        jax.ShapeDtypeStruct((b_, s_, FOURIER_WIDTH), BF16),
        jax.ShapeDtypeStruct((b_, SWA_Q_WIDTH, s_), BF16),
        jax.ShapeDtypeStruct((b_, SWA_KV_HEADS, s_, HEAD_DIM), BF16),
        jax.ShapeDtypeStruct((b_, SWA_KV_HEADS, SWA_VT_ROWS, s_), BF16),
        jax.ShapeDtypeStruct((b_, MLA_HEADS * MLA_PAD, s_), BF16),
        jax.ShapeDtypeStruct((b_, MLA_HEADS, s_, MLA_PAD), BF16),
        jax.ShapeDtypeStruct((b_, MLA_HEADS, s_ // chunk, MLA_VT_ROWS, chunk), BF16),
    )
    out_specs = (
        pl.BlockSpec((1, tm, FOURIER_WIDTH), lambda b, i: (b, i, 0)),
        pl.BlockSpec((1, tm, FOURIER_WIDTH), lambda b, i: (b, i, 0)),
        pl.BlockSpec((1, SWA_Q_WIDTH, tm), lambda b, i: (b, 0, i)),
        pl.BlockSpec((1, SWA_KV_HEADS, tm, HEAD_DIM), lambda b, i: (b, 0, i, 0)),
        pl.BlockSpec((1, SWA_KV_HEADS, SWA_VT_ROWS, tm), lambda b, i: (b, 0, 0, i)),
        pl.BlockSpec((1, MLA_HEADS * MLA_PAD, tm), lambda b, i: (b, 0, i)),
        pl.BlockSpec((1, MLA_HEADS, tm, MLA_PAD), lambda b, i: (b, 0, i, 0)),
        pl.BlockSpec((1, MLA_HEADS, tm // chunk, MLA_VT_ROWS, chunk), lambda b, i: (b, 0, i, 0, 0)),
    )
    return pl.pallas_call(
        _proj_kernel,
        grid=(b_, s_ // tm),
        in_specs=[
            pl.BlockSpec((1, tm, D_MODEL), lambda b, i: (b, i, 0)),
            pl.BlockSpec((1, 1, 6 * D_MODEL), mod_map),
            pl.BlockSpec((1, D_MODEL), const),
            pl.BlockSpec((D_MODEL, W1_COLS), const),
            pl.BlockSpec((W1T_ROWS, D_MODEL), const),
            pl.BlockSpec((tm, 4 * LANE), lambda b, i: (i, 0)),
            pl.BlockSpec((4 * LANE, tm), lambda b, i: (0, i)),
            pl.BlockSpec((FOURIER_WIDTH, 2 * FOURIER_WIDTH), const),
            pl.BlockSpec((1, MLA_Q_RANK), const),
            pl.BlockSpec((MLA_HEADS * MLA_PAD, MLA_Q_RANK), const),
            pl.BlockSpec((1, MLA_KV_RANK), const),
            pl.BlockSpec((MLA_KV_RANK, MLA_HEADS * MLA_PAD), const),
            pl.BlockSpec((MLA_HEADS * MLA_V_DIM, MLA_KV_RANK), const),
        ],
        out_specs=out_specs,
        out_shape=out_shapes,
        compiler_params=_cparams(("parallel", "parallel")),
        name="norm_in_proj",
    )(h, mod3, g1, w1, w1t, rope, ropet, wcs, qn, wuq, kvn, wukvk, wukvv)


def _dft_kernel(c_ref, s_ref, u_ref, v_ref, o_ref):
    o = jnp.dot(c_ref[...], u_ref[0].astype(BF16), preferred_element_type=F32)
    o = o + jnp.dot(s_ref[...], v_ref[0].astype(BF16), preferred_element_type=F32)
    o_ref[0] = o


def _dft_call(ctab, stab, u, v, tk):
    b_, s_, _ = u.shape
    whole = pl.BlockSpec((1, s_, FOURIER_WIDTH), lambda i, b: (b, 0, 0))
    return pl.pallas_call(
        _dft_kernel,
        grid=(s_ // tk, b_),
        in_specs=[
            pl.BlockSpec((tk, s_), lambda i, b: (i, 0)),
            pl.BlockSpec((tk, s_), lambda i, b: (i, 0)),
            whole, whole,
        ],
        out_specs=pl.BlockSpec((1, tk, FOURIER_WIDTH), lambda i, b: (b, i, 0)),
        out_shape=jax.ShapeDtypeStruct((b_, s_, FOURIER_WIDTH), F32),
        compiler_params=_cparams(("parallel", "arbitrary")),
        name="fourier_dft",
    )(ctab, stab, u, v)


FFT_SUB = 8
FFT_STEP = 2


def _fft1_kernel(m1_ref, m2_ref, tc_ref, ts_ref, u_ref, v_ref, bre_ref, bim_ref):
    n1, _, w = u_ref.shape[1:]
    sub = FFT_SUB
    u_all = u_ref[0].astype(F32)
    v_all = v_ref[0].astype(F32)
    bre, bim = [], []
    for h in range(u_ref.shape[2] // sub):
        js = slice(sub * h, sub * (h + 1))
        xu = u_all[:, js, :].reshape(n1 * sub, w).astype(BF16)
        xv = v_all[:, js, :].reshape(n1 * sub, w).astype(BF16)
        a = jnp.dot(m1_ref[...], xu, preferred_element_type=F32)
        a = a + jnp.dot(m2_ref[...], xv, preferred_element_type=F32)
        are, aim = a[0:n1 * sub], a[n1 * sub:]
        tc, ts = tc_ref[h], ts_ref[h]
        bre.append((are * tc + aim * ts).reshape(n1, sub, w))
        bim.append((aim * tc - are * ts).reshape(n1, sub, w))
    bre_ref[0] = jnp.concatenate(bre, axis=1).astype(bre_ref.dtype)
    bim_ref[0] = jnp.concatenate(bim, axis=1).astype(bim_ref.dtype)


def _fft2_kernel(g1_ref, g2_ref, bre_ref, bim_ref, o_ref):
    n2 = g1_ref.shape[1]
    for j in range(o_ref.shape[2]):
        rows = slice(n2 * j, n2 * (j + 1))
        x = jnp.dot(g1_ref[...], bre_ref[0, rows, :], preferred_element_type=F32)
        x = x + jnp.dot(g2_ref[...], bim_ref[0, rows, :], preferred_element_type=F32)
        o_ref[0, :, j, :] = x


def _fft_call(tabs, u, v):
    m1, m2, tc, ts, g1, g2 = tabs
    b_, s_, w = u.shape
    n1 = m1.shape[1] // FFT_SUB
    n2 = s_ // n1
    u4 = u.reshape(b_, n1, n2, w)
    v4 = v.reshape(b_, n1, n2, w)
    per_step = FFT_STEP * FFT_SUB
    tile = pl.BlockSpec((1, n1, per_step, w), lambda t, b: (b, 0, t, 0))
    tab = pl.BlockSpec((FFT_STEP, n1 * FFT_SUB, w), lambda t, b: (t, 0, 0))
    mat = pl.BlockSpec((2 * n1 * FFT_SUB, n1 * FFT_SUB), lambda t, b: (0, 0))
    bre, bim = pl.pallas_call(
        _fft1_kernel,
        grid=(n2 // per_step, b_),
        in_specs=[mat, mat, tab, tab, tile, tile],
        out_specs=(tile, tile),
        out_shape=(jax.ShapeDtypeStruct((b_, n1, n2, w), BF16),) * 2,
        compiler_params=_cparams(("parallel", "arbitrary")),
        name="fourier_fft1",
    )(m1, m2, tc, ts, u4, v4)
    blk = pl.BlockSpec((1, per_step * n2, w), lambda b, t: (b, t, 0))
    mat2 = pl.BlockSpec((n2, n2), lambda b, t: (0, 0))
    out = pl.pallas_call(
        _fft2_kernel,
        grid=(b_, n1 // per_step),
        in_specs=[mat2, mat2, blk, blk],
        out_specs=pl.BlockSpec((1, n2, per_step, w), lambda b, t: (b, 0, t, 0)),
        out_shape=jax.ShapeDtypeStruct((b_, n2, n1, w), F32),
        compiler_params=_cparams(("parallel", "parallel")),
        name="fourier_fft2",
    )(g1, g2, bre.reshape(b_, s_, w), bim.reshape(b_, s_, w))
    return out.reshape(b_, s_, w)


def _fft_tables(s_, n1, w):
    n2 = s_ // n1
    k1 = np.arange(n1)
    a1 = ((k1[:, None] * k1[None, :]) % n1).astype(np.float64) * (2.0 * math.pi / n1)
    c1, s1 = np.cos(a1), np.sin(a1)
    eye = np.eye(FFT_SUB)
    expand = lambda a: jnp.asarray(np.kron(a, eye), dtype=F32).astype(BF16)
    m1 = expand(np.concatenate([c1, -s1], axis=0))
    m2 = expand(np.concatenate([-s1, -c1], axis=0))
    s2 = jnp.arange(n2, dtype=jnp.int32)
    n_t = n2 // FFT_SUB
    row = jnp.arange(n1 * FFT_SUB, dtype=jnp.int32)
    s2_of = FFT_SUB * jnp.arange(n_t, dtype=jnp.int32)[:, None] + (row % FFT_SUB)[None, :]
    th = (((row // FFT_SUB)[None, :] * s2_of) % s_).astype(F32) * (2.0 * math.pi / s_)
    tc = jnp.broadcast_to(jnp.cos(th)[:, :, None], (n_t, n1 * FFT_SUB, w))
    ts = jnp.broadcast_to(jnp.sin(th)[:, :, None], (n_t, n1 * FFT_SUB, w))
    a2 = ((s2[:, None] * s2[None, :]) % n2).astype(F32) * (2.0 * math.pi / n2)
    scale = 1.0 / math.sqrt(s_)
    g1 = (jnp.cos(a2) * scale).astype(BF16)
    g2 = (jnp.sin(a2) * scale).astype(BF16)
    return m1, m2, tc, ts, g1, g2


def _mla_kernel(*refs, n_chunks, heads_per_step):
    hps = heads_per_step
    n_in = 6 if n_chunks else 4
    if n_chunks:
        zero_ref, qt_ref, k_ref, vt_ref, kc_ref, vct_ref = refs[:n_in]
    else:
        zero_ref, qt_ref, kc_ref, vct_ref = refs[:n_in]
    o_ref = refs[n_in]
    row0 = pl.multiple_of(zero_ref[0], MLA_TK)
    scratch = refs[n_in + 1:]
    tq = scratch[0].shape[1]
    n_sub = qt_ref.shape[2] // tq
    lc = kc_ref.shape[2]
    s_refs = [[scratch[MLA_S_SLOTS * (hps * u + hh):MLA_S_SLOTS * (hps * u + hh + 1)] for hh in range(hps)]
              for u in range(n_sub)]

    def keys(idx):
        if idx == 0:
            return (lambda hh: kc_ref[0, hh]), lc
        return (lambda hh: k_ref[0, hh, (idx - 1) * MLA_TK:idx * MLA_TK, :]), MLA_TK

    def values(idx):
        if idx == 0:
            return lambda hh: vct_ref[0, hh, 0]
        return lambda hh: vt_ref[0, hh, idx - 1]

    def logits(u, idx):
        ks, rows = keys(idx)
        cms = []
        for hh in range(hps):
            qt = qt_ref[0, MLA_PAD * hh:MLA_PAD * (hh + 1), tq * u:tq * (u + 1)]
            s = jnp.dot(ks(hh), qt, preferred_element_type=F32)
            s_refs[u][hh][idx % MLA_S_SLOTS][0:rows, :] = s
            cms.append(jnp.max(s, axis=0, keepdims=True))
        return tuple(cms)

    def accumulate(u, idx, carries, cms):
        _, rows = keys(idx)
        vts = values(idx)
        out = []
        for hh in range(hps):
            m, acc = carries[hh]
            m_new = jnp.maximum(m, cms[hh])
            alpha = jnp.exp2(m - m_new)
            s = s_refs[u][hh][idx % MLA_S_SLOTS][pl.ds(row0, rows), :]
            p = jnp.exp2(s - m_new).astype(BF16)
            acc = alpha * acc + jnp.dot(vts(hh), p, preferred_element_type=F32)
            out.append((m_new, acc))
        return tuple(out)

    init = (jnp.full((1, tq), NEG_INF, F32), jnp.zeros((MLA_VT_ROWS, tq), F32))
    carries = [tuple(init for _ in range(hps)) for _ in range(n_sub)]
    pending = [None] * n_sub
    n_idx = n_chunks + 1
    for t in range(n_idx + n_sub):
        for u in range(n_sub):
            idx = t - u
            new_cms = logits(u, idx) if 0 <= idx < n_idx else None
            if 1 <= idx <= n_idx:
                carries[u] = accumulate(u, idx - 1, carries[u], pending[u])
            pending[u] = new_cms
    for u in range(n_sub):
        o = jnp.concatenate([acc[0:MLA_V_DIM] / acc[MLA_V_DIM:MLA_V_DIM + 1] for (_, acc) in carries[u]],
                            axis=0)
        o_ref[0, tq * u:tq * (u + 1), :] = o.T.astype(o_ref.dtype)


def _mla_call(qt, k, vt, kc, vct, tq, n_sub):
    b_, _, sq = qt.shape
    h_ = kc.shape[1]
    lc = kc.shape[2]
    hps = 2
    n_chunks = 0 if k is None else k.shape[2] // MLA_TK
    assert n_chunks == 0 or lc <= MLA_TK
    in_specs = [pl.BlockSpec(memory_space=pltpu.SMEM),
                pl.BlockSpec((1, hps * MLA_PAD, tq), lambda b, p, i: (b, p, i))]
    args = [jnp.zeros((1,), jnp.int32), qt]
    if k is not None:
        sk = k.shape[2]
        in_specs += [pl.BlockSpec((1, hps, sk, MLA_PAD), lambda b, p, i: (b, p, 0, 0)),
                     pl.BlockSpec((1, hps, n_chunks, MLA_VT_ROWS, MLA_TK), lambda b, p, i: (b, p, 0, 0, 0))]
        args += [k, vt]
    in_specs += [pl.BlockSpec((1, hps, lc, MLA_PAD), lambda b, p, i: (b, p, 0, 0)),
                 pl.BlockSpec((1, hps, 1, MLA_VT_ROWS, lc), lambda b, p, i: (b, p, 0, 0, 0))]
    args += [kc, vct]
    return pl.pallas_call(
        functools.partial(_mla_kernel, n_chunks=n_chunks, heads_per_step=hps),
        grid=(b_, h_ // hps, sq // tq),
        in_specs=in_specs,
        out_specs=pl.BlockSpec((1, tq, hps * MLA_V_DIM), lambda b, p, i: (b, i, p)),
        out_shape=jax.ShapeDtypeStruct((b_, sq, h_ * MLA_V_DIM), BF16),
        scratch_shapes=[pltpu.VMEM((MLA_TK if n_chunks else lc, tq // n_sub), F32)] * (MLA_S_SLOTS * hps * n_sub),
        compiler_params=_cparams(("parallel", "parallel", "arbitrary")),
        name="mla_attn",
    )(*args)


def _swa_kernel(*refs, n_local, s_len):
    zero_ref, sink_ref, qt_ref = refs[:3]
    k_refs = refs[3:3 + n_local]
    vt_refs = refs[3 + n_local:3 + 2 * n_local]
    kc_ref, vct_ref, o_ref = refs[3 + 2 * n_local:6 + 2 * n_local]
    s_refs = refs[6 + 2 * n_local:]
    row0 = pl.multiple_of(zero_ref[0], BLOCK)
    tq = s_refs[0].shape[1]
    step_tq = qt_ref.shape[2]
    n_sub = step_tq // tq
    lc = kc_ref.shape[2]
    sub_blocks = tq // BLOCK + 2 if n_local else 0
    nk = sub_blocks * BLOCK
    sinks = [sink_ref[hh] * LOG2E for hh in range(SWA_HEADS)]

    def mask_bias(u):
        q0 = pl.program_id(1) * step_tq + u * tq
        kpos = q0 - BLOCK + lax.broadcasted_iota(jnp.int32, (nk, tq), 0)
        qpos = q0 + lax.broadcasted_iota(jnp.int32, (nk, tq), 1)
        valid = (jnp.abs(kpos - qpos) <= WINDOW) & (kpos >= 0) & (kpos < s_len)
        return jnp.where(valid, 0.0, NEG_INF).astype(F32)

    def keys(u, kh):
        kc = kc_ref[0, kh]
        vct = vct_ref[0, kh]
        if not n_local:
            return kc, vct
        first = u * (tq // BLOCK)
        blocks = range(first, first + sub_blocks)
        return (jnp.concatenate([kc] + [k_refs[j][0, kh] for j in blocks], axis=0),
                jnp.concatenate([vct] + [vt_refs[j][0, kh] for j in blocks], axis=1))

    def logits(unit, u, heads, bias):
        ms = []
        for k, hh in enumerate(heads):
            k_all = kv[(u, hh // SWA_GROUP)][0]
            qt = qt_ref[0, HEAD_DIM * hh:HEAD_DIM * (hh + 1), tq * u:tq * (u + 1)]
            s_ref = s_refs[SWA_UNIT_HEADS * (unit % 2) + k]
            s_c = jnp.dot(k_all[0:lc], qt, preferred_element_type=F32)
            s_ref[0:lc, :] = s_c
            m = jnp.maximum(jnp.max(s_c, axis=0, keepdims=True), sinks[hh])
            if n_local:
                s_l = jnp.dot(k_all[lc:], qt, preferred_element_type=F32) + bias
                s_ref[lc:lc + nk, :] = s_l
                m = jnp.maximum(m, jnp.max(s_l, axis=0, keepdims=True))
            ms.append(m)
        return ms

    def accumulate(unit, u, heads, ms):
        res = []
        for k, hh in enumerate(heads):
            vt_all = kv[(u, hh // SWA_GROUP)][1]
            s_ref = s_refs[SWA_UNIT_HEADS * (unit % 2) + k]
            p = jnp.exp2(s_ref[pl.ds(row0, lc + nk), :] - ms[k]).astype(BF16)
            acc = jnp.dot(vt_all, p, preferred_element_type=F32)
            l = acc[HEAD_DIM:HEAD_DIM + 1] + jnp.exp2(sinks[hh] - ms[k])
            res.append(acc[0:HEAD_DIM] / l)
        return res

    units = [(u, tuple(range(h0, h0 + SWA_UNIT_HEADS)))
             for u in range(n_sub) for h0 in range(0, SWA_HEADS, SWA_UNIT_HEADS)]
    kv = {(u, kh): keys(u, kh) for u in range(n_sub) for kh in range(SWA_KV_HEADS)}
    biases = [mask_bias(u) if n_local else None for u in range(n_sub)]
    outs = [[] for _ in range(n_sub)]
    m_prev = None
    for unit in range(len(units) + 1):
        m_next = None
        if unit < len(units):
            u, heads = units[unit]
            m_next = logits(unit, u, heads, biases[u])
        if unit > 0:
            u, heads = units[unit - 1]
            outs[u] += accumulate(unit - 1, u, heads, m_prev)
        m_prev = m_next
    for u in range(n_sub):
        o = jnp.concatenate(outs[u], axis=0)
        o_ref[0, tq * u:tq * (u + 1), :] = o.T.astype(o_ref.dtype)


def _swa_call(sink, qt, k, vt, kc, vct):
    b_, _, sq = qt.shape
    lc = kc.shape[2]
    local = k is not None
    sub_tq = SWA_TQ if local else sq
    tq = sub_tq * SWA_SUB_TILES if local else sq
    nb = sq // tq
    n_local = tq // BLOCK + 2 if local else 0
    sub_keys = lc + (sub_tq // BLOCK + 2) * BLOCK if local else lc
    in_specs = [pl.BlockSpec(memory_space=pltpu.SMEM), pl.BlockSpec(memory_space=pltpu.SMEM),
                pl.BlockSpec((1, SWA_Q_WIDTH, tq), lambda b, n: (b, 0, n))]
    args = [jnp.zeros((1,), jnp.int32), sink, qt]
    if local:
        nkb = sq // BLOCK
        per = tq // BLOCK

        def blk_idx(n, j):
            return jnp.clip(n * per - 1 + j, 0, nkb - 1)

        in_specs += [pl.BlockSpec((1, SWA_KV_HEADS, BLOCK, HEAD_DIM),
                                  functools.partial(lambda b, n, j: (b, 0, blk_idx(n, j), 0), j=j))
                     for j in range(n_local)]
        in_specs += [pl.BlockSpec((1, SWA_KV_HEADS, SWA_VT_ROWS, BLOCK),
                                  functools.partial(lambda b, n, j: (b, 0, 0, blk_idx(n, j)), j=j))
                     for j in range(n_local)]
        args += [k] * n_local + [vt] * n_local
    in_specs += [pl.BlockSpec((1, SWA_KV_HEADS, lc, HEAD_DIM), lambda b, n: (b, 0, 0, 0)),
                 pl.BlockSpec((1, SWA_KV_HEADS, SWA_VT_ROWS, lc), lambda b, n: (b, 0, 0, 0))]
    args += [kc, vct]
    return pl.pallas_call(
        functools.partial(_swa_kernel, n_local=n_local, s_len=sq),
        grid=(b_, nb),
        in_specs=in_specs,
        out_specs=pl.BlockSpec((1, tq, SWA_Q_WIDTH), lambda b, n: (b, n, 0)),
        out_shape=jax.ShapeDtypeStruct((b_, sq, SWA_Q_WIDTH), BF16),
        scratch_shapes=[pltpu.VMEM((sub_keys, sub_tq), F32)] * (2 * SWA_UNIT_HEADS),
        compiler_params=_cparams(("parallel", "arbitrary")),
        name="swa_attn",
    )(*args)


def _mlp_kernel(fo_ref, so_ref, mo_ref, h_ref, mod_ref, g2_ref, wo_ref, w1_ref, w2_ref, gf_ref, o_ref, *,
                final, ff_chunk):
    mod = mod_ref[0]
    g1 = mod[:, 2 * D_MODEL:3 * D_MODEL]
    sh2 = mod[:, 3 * D_MODEL:4 * D_MODEL]
    sc2 = mod[:, 4 * D_MODEL:5 * D_MODEL]
    g2 = mod[:, 5 * D_MODEL:6 * D_MODEL]
    mix = jnp.concatenate([fo_ref[0].astype(BF16), so_ref[0], mo_ref[0]], axis=-1)
    h = h_ref[0] + g1 * jnp.dot(mix, wo_ref[...], preferred_element_type=F32)
    y = (_rms(h, g2_ref[...]) * (1.0 + sc2) + sh2).astype(BF16)
    acc = None
    for c in range(D_FF // ff_chunk):
        a = jnp.dot(y, w1_ref[:, c * ff_chunk:(c + 1) * ff_chunk], preferred_element_type=F32)
        a = jnp.maximum(a, 0.0)
        a = (a * a).astype(BF16)
        part = jnp.dot(a, w2_ref[c * ff_chunk:(c + 1) * ff_chunk, :], preferred_element_type=F32)
        acc = part if acc is None else acc + part
    h = h + g2 * acc
    if final:
        h = _rms(h, gf_ref[...])
    o_ref[0] = h


def _mlp_call(fo, so, mo, h, mod3, mod_row, g2, wo, w1, w2, gf, tm, final):
    b_, s_, _ = h.shape
    const = lambda b, i: (0, 0)
    tile = lambda b, i: (b, i, 0)
    if mod_row is None:
        mod_map = lambda b, i: (b, 0, 0)
    else:
        mod_map = lambda b, i: (mod_row, 0, 0)
    once = pl.Buffered(1)
    return pl.pallas_call(
        functools.partial(_mlp_kernel, final=final, ff_chunk=1024),
        grid=(b_, s_ // tm),
        in_specs=[
            pl.BlockSpec((1, tm, FOURIER_WIDTH), tile),
            pl.BlockSpec((1, tm, SWA_Q_WIDTH), tile),
            pl.BlockSpec((1, tm, MLA_HEADS * MLA_V_DIM), tile),
            pl.BlockSpec((1, tm, D_MODEL), tile),
            pl.BlockSpec((1, 1, 6 * D_MODEL), mod_map),
            pl.BlockSpec((1, D_MODEL), const),
            pl.BlockSpec((D_MIX, D_MODEL), const, pipeline_mode=once),
            pl.BlockSpec((D_MODEL, D_FF), const, pipeline_mode=once),
            pl.BlockSpec((D_FF, D_MODEL), const, pipeline_mode=once),
            pl.BlockSpec((1, D_MODEL), const),
        ],
        out_specs=pl.BlockSpec((1, tm, D_MODEL), tile),
        out_shape=jax.ShapeDtypeStruct((b_, s_, D_MODEL), F32),
        compiler_params=_cparams(("parallel", "parallel")),
        name="out_proj_mlp",
    )(fo, so, mo, h, mod3, g2, wo, w1, w2, gf)


def _rot_cols(w, heads, dim):
    w4 = w.reshape(w.shape[0], heads, 2, dim // 2)
    return jnp.stack([-w4[:, :, 1], w4[:, :, 0]], axis=2).reshape(w.shape[0], heads * dim)


def _pad_lanes(w, heads, before, after):
    w3 = w.reshape(w.shape[0], heads, w.shape[1] // heads)
    w3 = jnp.pad(w3, ((0, 0), (0, 0), (before, after)))
    return w3.reshape(w.shape[0], -1)


def _pack_w_in(w_in):
    wk = w_in[:, OFF_SWA_K:OFF_SWA_V]
    wkr = w_in[:, OFF_MLA_KR:OFF_MLA_KR + MLA_ROPE_DIM]
    kr_tile = jnp.concatenate([jnp.zeros((D_MODEL, MLA_NOPE_DIM), F32), wkr, _rot_cols(wkr, 1, MLA_ROPE_DIM)],
                              axis=1)
    cols = [w_in[:, :OFF_SWA_Q], wk, w_in[:, OFF_MLA_CQ:OFF_MLA_KR], kr_tile,
            _rot_cols(wk, SWA_KV_HEADS, HEAD_DIM)]
    rows = w_in[:, OFF_SWA_Q:OFF_SWA_K], w_in[:, OFF_SWA_V:OFF_MLA_CQ]
    return jnp.concatenate(cols, axis=1).astype(BF16), jnp.concatenate(rows, axis=1).T.astype(BF16)


def _pack_w_uq(w_uq):
    per = MLA_NOPE_DIM + MLA_ROPE_DIM
    return _pad_lanes(w_uq, MLA_HEADS, 0, MLA_PAD - per).T.astype(BF16)


def _pack_w_ukv(w_ukv):
    w3 = w_ukv.reshape(MLA_KV_RANK, MLA_HEADS, MLA_NOPE_DIM + MLA_V_DIM)
    wk = w3[:, :, :MLA_NOPE_DIM].reshape(MLA_KV_RANK, MLA_HEADS * MLA_NOPE_DIM)
    wv = w3[:, :, MLA_NOPE_DIM:].reshape(MLA_KV_RANK, MLA_HEADS * MLA_V_DIM)
    return _pad_lanes(wk, MLA_HEADS, 0, MLA_PAD - MLA_NOPE_DIM).astype(BF16), wv.T.astype(BF16)


def _rope_table(rows):
    r, col = jnp.meshgrid(jnp.arange(rows, dtype=F32), jnp.arange(GRID_W, dtype=F32), indexing="ij")
    r = r.reshape(-1)
    col = col.reshape(-1)

    def tables(dim):
        n_freq = dim // 4
        inv = ROPE_THETA ** (-jnp.arange(n_freq, dtype=F32) / n_freq)
        ang = jnp.concatenate([r[:, None] * inv[None, :], col[:, None] * inv[None, :]], axis=-1)
        return jnp.cos(ang), jnp.sin(ang)

    ch, sh = tables(HEAD_DIM)
    cr, sr = tables(MLA_ROPE_DIM)
    mla_pad = ((0, 0), (MLA_NOPE_DIM, LANE - MLA_NOPE_DIM - MLA_ROPE_DIM))
    nope_ones = (jnp.arange(LANE) < MLA_NOPE_DIM).astype(F32)[None, :]
    return jnp.concatenate([jnp.tile(ch, (1, 4)), jnp.tile(sh, (1, 4)),
                            jnp.pad(jnp.tile(cr, (1, 2)), mla_pad) + nope_ones,
                            jnp.pad(jnp.tile(sr, (1, 2)), mla_pad)], axis=-1)


def _identity_rope_table(n):
    ones = jnp.ones((n, LANE), F32)
    zeros = jnp.zeros((n, LANE), F32)
    m = jnp.concatenate([jnp.ones((n, MLA_NOPE_DIM + MLA_ROPE_DIM), F32),
                         jnp.zeros((n, LANE - MLA_NOPE_DIM - MLA_ROPE_DIM), F32)], axis=-1)
    return jnp.concatenate([ones, zeros, m, zeros], axis=-1)


def _dft_tables(n):
    k = jnp.arange(n, dtype=jnp.int32)
    ks = (k[:, None] * k[None, :]) % n
    ang = ks.astype(F32) * (2.0 * math.pi / n)
    scale = 1.0 / math.sqrt(n)
    return (jnp.cos(ang) * scale).astype(BF16), (-jnp.sin(ang) * scale).astype(BF16)


def kernel(x, c, ctx, c_ctx, w_ada, b_ada, norm1_g, norm2_g, w_in, w_fourier, swa_sink, mla_q_norm, w_uq,
           mla_kv_norm, w_ukv, w_out, w_mlp1, w_mlp2, final_norm_g):
    b_, s_, _ = x.shape
    lc = ctx.shape[1]
    depth = w_ada.shape[0]
    rows = s_ // GRID_W

    mod_rows = 16
    cc = jnp.concatenate([c, c_ctx[None, :], jnp.zeros((mod_rows - b_ - 1, D_MODEL), F32)], axis=0)
    mod_all = _ada_call(cc, w_ada, b_ada)

    rope = _rope_table(rows)
    rope_c = _identity_rope_table(lc)
    to_feature_major = lambda t: jnp.concatenate([t[:, 2 * LANE:], t[:, :2 * LANE]], axis=1).T
    ropet = to_feature_major(rope)
    ropet_c = to_feature_major(rope_c)
    fft_tabs = _fft_tables(s_, GRID_W, FOURIER_WIDTH)
    ctab_c, stab_c = _dft_tables(lc)
    kk = jnp.arange(FOURIER_GROUP_DIM, dtype=jnp.int32)
    ang64 = ((kk[:, None] * kk[None, :]) % FOURIER_GROUP_DIM).astype(F32) * (2.0 * math.pi / FOURIER_GROUP_DIM)
    c64, s64 = jnp.cos(ang64), jnp.sin(ang64)

    h, hc = x, ctx
    for l in range(depth):
        last = l == depth - 1
        mod3 = mod_all[l].reshape(mod_rows, 1, 6 * D_MODEL)
        g1 = norm1_g[l].reshape(1, D_MODEL)
        g2 = norm2_g[l].reshape(1, D_MODEL)
        gf = final_norm_g.reshape(1, D_MODEL)
        w1, w1t = _pack_w_in(w_in[l])
        wuq = _pack_w_uq(w_uq[l])
        wukvk, wukvv = _pack_w_ukv(w_ukv[l])
        wcs = _fw_call(c64, s64, w_fourier[l]).astype(BF16)
        qn = mla_q_norm[l].reshape(1, MLA_Q_RANK)
        kvn = mla_kv_norm[l].reshape(1, MLA_KV_RANK)
        wo = w_out[l].astype(BF16)
        wm1 = w_mlp1[l].astype(BF16)
        wm2 = w_mlp2[l].astype(BF16)
        sink = swa_sink[l]

        fu, fv, sq, sk, sv, mq, mk, mv = _proj_call(h, mod3, None, g1, w1, w1t, rope, ropet, wcs, qn, wuq,
                                                    kvn, wukvk, wukvv, tm=1024)
        fuc, fvc, sqc, skc, svc, mqc, mkc, mvc = _proj_call(hc, mod3, b_, g1, w1, w1t, rope_c, ropet_c, wcs, qn,
                                                            wuq, kvn, wukvk, wukvv, tm=lc)

        fo = _fft_call(fft_tabs, fu, fv)
        so = _swa_call(sink, sq, sk, sv, skc, svc)
        mo = _mla_call(mq, mk, mv, mkc, mvc, tq=2048, n_sub=4)
        h = _mlp_call(fo, so, mo, h, mod3, None, g2, wo, wm1, wm2, gf, tm=512, final=last)

        if not last:
            foc = _dft_call(ctab_c, stab_c, fuc, fvc, tk=lc)
            soc = _swa_call(sink, sqc, None, None, skc, svc)
            moc = _mla_call(mqc, None, None, mkc, mvc, tq=lc, n_sub=1)
            flat = lambda t: t.reshape(1, b_ * lc, t.shape[-1])
            hc = _mlp_call(flat(foc), flat(soc), flat(moc), flat(hc), mod3, b_, g2, wo, wm1, wm2, gf,
                           tm=512, final=False).reshape(b_, lc, D_MODEL)
    return h
```

```python
import functools
import math

import jax
import jax.numpy as jnp
import numpy as np
from jax import lax
from jax.experimental import pallas as pl
from jax.experimental.pallas import tpu as pltpu

D_MODEL = 1024
GRID_W = 64
HEAD_DIM = 64
FOURIER_GROUPS = 4
FOURIER_GROUP_DIM = 64
FOURIER_WIDTH = FOURIER_GROUPS * FOURIER_GROUP_DIM
SWA_HEADS = 6
SWA_KV_HEADS = 2
SWA_GROUP = SWA_HEADS // SWA_KV_HEADS
SWA_Q_WIDTH = SWA_HEADS * HEAD_DIM
SWA_KV_WIDTH = SWA_KV_HEADS * HEAD_DIM
WINDOW = 128
BLOCK = 128
MLA_HEADS = 6
MLA_NOPE_DIM = 64
MLA_ROPE_DIM = 32
MLA_V_DIM = 64
MLA_Q_RANK = 256
MLA_KV_RANK = 128
MLA_SCALE = (MLA_NOPE_DIM + MLA_ROPE_DIM) ** -0.5
SWA_SCALE = HEAD_DIM ** -0.5
D_MIX = FOURIER_WIDTH + SWA_Q_WIDTH + MLA_HEADS * MLA_V_DIM
OFF_SWA_Q = FOURIER_WIDTH
OFF_SWA_K = OFF_SWA_Q + SWA_Q_WIDTH
OFF_SWA_V = OFF_SWA_K + SWA_KV_WIDTH
OFF_MLA_CQ = OFF_SWA_V + SWA_KV_WIDTH
OFF_MLA_CKV = OFF_MLA_CQ + MLA_Q_RANK
OFF_MLA_KR = OFF_MLA_CKV + MLA_KV_RANK
D_FF = 4 * D_MODEL
ROPE_THETA = 10000.0
NORM_EPS = 1e-6
NEG_INF = -1e30

LANE = 128
MLA_PAD = LANE

C_F = 0
C_K = C_F + FOURIER_WIDTH
C_CQ = C_K + SWA_KV_WIDTH
C_CKV = C_CQ + MLA_Q_RANK
C_KR = C_CKV + MLA_KV_RANK
C_KROT = C_KR + LANE
W1_COLS = C_KROT + SWA_KV_WIDTH
R_Q = 0
R_V = R_Q + SWA_Q_WIDTH
W1T_ROWS = R_V + SWA_KV_WIDTH

BF16 = jnp.bfloat16
F32 = jnp.float32
LOG2E = math.log2(math.e)
_NT = (((1,), (1,)), ((), ()))
MLA_TK = 256
MLA_VT_ROWS = MLA_V_DIM + 16
MLA_S_SLOTS = 2
SWA_VT_ROWS = HEAD_DIM + 16
SWA_TQ = 256
SWA_SUB_TILES = 4
SWA_UNIT_HEADS = 3

VMEM_LIMIT = 56 * 1024 * 1024


def _cparams(sem):
    return pltpu.CompilerParams(dimension_semantics=sem, vmem_limit_bytes=VMEM_LIMIT)


def _rms(x, g):
    ms = jnp.mean(x * x, axis=-1, keepdims=True)
    return x * lax.rsqrt(ms + NORM_EPS) * g


def _ada_kernel(cc_ref, w_ref, b_ref, o_ref):
    cc = cc_ref[...]
    s = cc * jax.nn.sigmoid(cc)
    o_ref[0] = jnp.dot(s, w_ref[0], preferred_element_type=F32,
                       precision=lax.Precision.HIGHEST) + b_ref[0]


def _ada_call(cc, w_ada, b_ada):
    depth = w_ada.shape[0]
    rows = cc.shape[0]
    tn = 1536
    n_out = w_ada.shape[2]
    return pl.pallas_call(
        _ada_kernel,
        grid=(depth, n_out // tn),
        in_specs=[
            pl.BlockSpec((rows, D_MODEL), lambda l, j: (0, 0)),
            pl.BlockSpec((1, D_MODEL, tn), lambda l, j: (l, 0, j)),
            pl.BlockSpec((1, 1, tn), lambda l, j: (l, 0, j)),
        ],
        out_specs=pl.BlockSpec((1, rows, tn), lambda l, j: (l, 0, j)),
        out_shape=jax.ShapeDtypeStruct((depth, rows, n_out), F32),
        compiler_params=_cparams(("arbitrary", "arbitrary")),
        name="ada_mod",
    )(cc, w_ada, b_ada.reshape(depth, 1, n_out))


def _fw_kernel(c_ref, s_ref, w_ref, o_ref):
    o_ref[...] = jnp.zeros(o_ref.shape, o_ref.dtype)
    inv = 1.0 / math.sqrt(FOURIER_GROUP_DIM)
    for g in range(FOURIER_GROUPS):
        w = w_ref[g]
        a = jnp.dot(c_ref[...], w, preferred_element_type=F32, precision=lax.Precision.HIGHEST) * inv
        b = jnp.dot(s_ref[...], w, preferred_element_type=F32, precision=lax.Precision.HIGHEST) * inv
        lo = g * FOURIER_GROUP_DIM
        hi = lo + FOURIER_GROUP_DIM
        o_ref[lo:hi, lo:hi] = a
        o_ref[lo:hi, FOURIER_WIDTH + lo:FOURIER_WIDTH + hi] = b


def _fw_call(c64, s64, w_f):
    return pl.pallas_call(
        _fw_kernel,
        out_shape=jax.ShapeDtypeStruct((FOURIER_WIDTH, 2 * FOURIER_WIDTH), F32),
        name="fourier_w",
    )(c64, s64, w_f)


def _ones_row(rows, cols):
    return (lax.broadcasted_iota(jnp.int32, (rows, cols), 0) == 0).astype(BF16)


def _proj_kernel(h_ref, mod_ref, g1_ref, w1_ref, w1t_ref, rope_ref, ropet_ref, wcs_ref, qn_ref, wuq_ref,
                 kvn_ref, wukvk_ref, wukvv_ref,
                 u_o, v_o, sq_o, sk_o, sv_o, mq_o, mk_o, mv_o):
    x = h_ref[0]
    tm = x.shape[0]
    mod = mod_ref[0]
    sh = mod[:, 0:D_MODEL]
    sc = mod[:, D_MODEL:2 * D_MODEL]
    y = (_rms(x, g1_ref[...]) * (1.0 + sc) + sh).astype(BF16)
    u = jnp.dot(y, w1_ref[...], preferred_element_type=F32)
    ut = lax.dot_general(w1t_ref[...], y, _NT, preferred_element_type=F32)

    cos_h = rope_ref[:, 0:LANE]
    sin_h = rope_ref[:, LANE:2 * LANE]
    cos_m = rope_ref[:, 2 * LANE:3 * LANE]
    sin_m = rope_ref[:, 3 * LANE:4 * LANE]
    cos_mt = ropet_ref[0:LANE, :]
    sin_mt = ropet_ref[LANE:2 * LANE, :]
    cos_ht = ropet_ref[2 * LANE:3 * LANE, :]
    sin_ht = ropet_ref[3 * LANE:4 * LANE, :]

    f = u[:, C_F:C_F + FOURIER_WIDTH].astype(BF16)
    uv = jnp.dot(f, wcs_ref[...], preferred_element_type=F32).astype(BF16)
    u_o[0] = uv[:, :FOURIER_WIDTH]
    v_o[0] = uv[:, FOURIER_WIDTH:]

    def rope_rows(x1, x2, c, s, scale):
        return ((x1 * c - x2 * s) * scale).astype(BF16), ((x2 * c + x1 * s) * scale).astype(BF16)

    half = HEAD_DIM // 2
    c_h, s_h = cos_ht[0:half], sin_ht[0:half]
    for hh in range(SWA_HEADS):
        r0 = R_Q + HEAD_DIM * hh
        lo, hi = rope_rows(ut[r0:r0 + half], ut[r0 + half:r0 + HEAD_DIM], c_h, s_h, SWA_SCALE * LOG2E)
        sq_o[0, HEAD_DIM * hh:HEAD_DIM * hh + half, :] = lo
        sq_o[0, HEAD_DIM * hh + half:HEAD_DIM * (hh + 1), :] = hi
    kk = (u[:, C_K:C_K + LANE] * cos_h + u[:, C_KROT:C_KROT + LANE] * sin_h).astype(BF16)
    ones_swa = _ones_row(SWA_VT_ROWS - HEAD_DIM, tm)
    for kh in range(SWA_KV_HEADS):
        sk_o[0, kh] = kk[:, HEAD_DIM * kh:HEAD_DIM * (kh + 1)]
        sv_o[0, kh, 0:HEAD_DIM, :] = ut[R_V + HEAD_DIM * kh:R_V + HEAD_DIM * (kh + 1)].astype(BF16)
        sv_o[0, kh, HEAD_DIM:SWA_VT_ROWS, :] = ones_swa

    cqn = _rms(u[:, C_CQ:C_CQ + MLA_Q_RANK], qn_ref[...]).astype(BF16)
    qa = lax.dot_general(wuq_ref[...], cqn, _NT, preferred_element_type=F32)
    rh = MLA_ROPE_DIM // 2
    c_m, s_m = cos_mt[MLA_NOPE_DIM:MLA_NOPE_DIM + rh], sin_mt[MLA_NOPE_DIM:MLA_NOPE_DIM + rh]
    q_scale = MLA_SCALE * LOG2E
    for hh in range(MLA_HEADS):
        r0 = MLA_PAD * hh
        r1 = r0 + MLA_NOPE_DIM
        lo, hi = rope_rows(qa[r1:r1 + rh], qa[r1 + rh:r1 + 2 * rh], c_m, s_m, q_scale)
        mq_o[0, r0:r1, :] = (qa[r0:r1] * q_scale).astype(BF16)
        mq_o[0, r1:r1 + rh, :] = lo
        mq_o[0, r1 + rh:r1 + 2 * rh, :] = hi
        mq_o[0, r1 + 2 * rh:r0 + MLA_PAD, :] = jnp.zeros((MLA_PAD - MLA_NOPE_DIM - 2 * rh, tm), BF16)

    ckvn = _rms(u[:, C_CKV:C_CKV + MLA_KV_RANK], kvn_ref[...]).astype(BF16)
    kr_tile = u[:, C_KR:C_KR + LANE]
    kr = kr_tile * cos_m + pltpu.roll(kr_tile, LANE - MLA_ROPE_DIM, 1) * sin_m
    kvk = jnp.dot(ckvn, wukvk_ref[...], preferred_element_type=F32)
    vt = lax.dot_general(wukvv_ref[...], ckvn, _NT, preferred_element_type=F32).astype(BF16)
    chunk = mv_o.shape[4]
    ones_mla = _ones_row(MLA_VT_ROWS - MLA_V_DIM, chunk)
    for hh in range(MLA_HEADS):
        mk_o[0, hh] = (kvk[:, MLA_PAD * hh:MLA_PAD * (hh + 1)] + kr).astype(BF16)
        for c in range(tm // chunk):
            mv_o[0, hh, c, 0:MLA_V_DIM, :] = vt[MLA_V_DIM * hh:MLA_V_DIM * (hh + 1), chunk * c:chunk * (c + 1)]
            mv_o[0, hh, c, MLA_V_DIM:MLA_VT_ROWS, :] = ones_mla


def _proj_call(h, mod3, mod_row, g1, w1, w1t, rope, ropet, wcs, qn, wuq, kvn, wukvk, wukvv, tm):
    b_, s_, _ = h.shape
    chunk = min(MLA_TK, tm)
    const = lambda b, i: (0, 0)
    if mod_row is None:
        mod_map = lambda b, i: (b, 0, 0)
    else:
        mod_map = lambda b, i: (mod_row, 0, 0)
    out_shapes = (
        jax.ShapeDtypeStruct((b_, s_, FOURIER_WIDTH), BF16),
        jax.ShapeDtypeStruct((b_, s_, FOURIER_WIDTH), BF16),
        jax.ShapeDtypeStruct((b_, SWA_Q_WIDTH, s_), BF16),
        jax.ShapeDtypeStruct((b_, SWA_KV_HEADS, s_, HEAD_DIM), BF16),
        jax.ShapeDtypeStruct((b_, SWA_KV_HEADS, SWA_VT_ROWS, s_), BF16),
        jax.ShapeDtypeStruct((b_, MLA_HEADS * MLA_PAD, s_), BF16),
        jax.ShapeDtypeStruct((b_, MLA_HEADS, s_, MLA_PAD), BF16),
        jax.ShapeDtypeStruct((b_, MLA_HEADS, s_ // chunk, MLA_VT_ROWS, chunk), BF16),
    )
    out_specs = (
        pl.BlockSpec((1, tm, FOURIER_WIDTH), lambda b, i: (b, i, 0)),
        pl.BlockSpec((1, tm, FOURIER_WIDTH), lambda b, i: (b, i, 0)),
        pl.BlockSpec((1, SWA_Q_WIDTH, tm), lambda b, i: (b, 0, i)),
        pl.BlockSpec((1, SWA_KV_HEADS, tm, HEAD_DIM), lambda b, i: (b, 0, i, 0)),
        pl.BlockSpec((1, SWA_KV_HEADS, SWA_VT_ROWS, tm), lambda b, i: (b, 0, 0, i)),
        pl.BlockSpec((1, MLA_HEADS * MLA_PAD, tm), lambda b, i: (b, 0, i)),
        pl.BlockSpec((1, MLA_HEADS, tm, MLA_PAD), lambda b, i: (b, 0, i, 0)),
        pl.BlockSpec((1, MLA_HEADS, tm // chunk, MLA_VT_ROWS, chunk), lambda b, i: (b, 0, i, 0, 0)),
    )
    return pl.pallas_call(
        _proj_kernel,
        grid=(b_, s_ // tm),
        in_specs=[
            pl.BlockSpec((1, tm, D_MODEL), lambda b, i: (b, i, 0)),
            pl.BlockSpec((1, 1, 6 * D_MODEL), mod_map),
            pl.BlockSpec((1, D_MODEL), const),
            pl.BlockSpec((D_MODEL, W1_COLS), const),
            pl.BlockSpec((W1T_ROWS, D_MODEL), const),
            pl.BlockSpec((tm, 4 * LANE), lambda b, i: (i, 0)),
            pl.BlockSpec((4 * LANE, tm), lambda b, i: (0, i)),
            pl.BlockSpec((FOURIER_WIDTH, 2 * FOURIER_WIDTH), const),
            pl.BlockSpec((1, MLA_Q_RANK), const),
            pl.BlockSpec((MLA_HEADS * MLA_PAD, MLA_Q_RANK), const),
            pl.BlockSpec((1, MLA_KV_RANK), const),
            pl.BlockSpec((MLA_KV_RANK, MLA_HEADS * MLA_PAD), const),
            pl.BlockSpec((MLA_HEADS * MLA_V_DIM, MLA_KV_RANK), const),
        ],
        out_specs=out_specs,
        out_shape=out_shapes,
        compiler_params=_cparams(("parallel", "parallel")),
        name="norm_in_proj",
    )(h, mod3, g1, w1, w1t, rope, ropet, wcs, qn, wuq, kvn, wukvk, wukvv)


def _dft_kernel(c_ref, s_ref, u_ref, v_ref, o_ref):
    o = jnp.dot(c_ref[...], u_ref[0].astype(BF16), preferred_element_type=F32)
    o = o + jnp.dot(s_ref[...], v_ref[0].astype(BF16), preferred_element_type=F32)
    o_ref[0] = o


def _dft_call(ctab, stab, u, v, tk):
    b_, s_, _ = u.shape
    whole = pl.BlockSpec((1, s_, FOURIER_WIDTH), lambda i, b: (b, 0, 0))
    return pl.pallas_call(
        _dft_kernel,
        grid=(s_ // tk, b_),
        in_specs=[
            pl.BlockSpec((tk, s_), lambda i, b: (i, 0)),
            pl.BlockSpec((tk, s_), lambda i, b: (i, 0)),
            whole, whole,
        ],
        out_specs=pl.BlockSpec((1, tk, FOURIER_WIDTH), lambda i, b: (b, i, 0)),
        out_shape=jax.ShapeDtypeStruct((b_, s_, FOURIER_WIDTH), F32),
        compiler_params=_cparams(("parallel", "arbitrary")),
        name="fourier_dft",
    )(ctab, stab, u, v)


FFT_SUB = 8
FFT_STEP = 2


def _fft1_kernel(m1_ref, m2_ref, tc_ref, ts_ref, u_ref, v_ref, bre_ref, bim_ref):
    n1, _, w = u_ref.shape[1:]
    sub = FFT_SUB
    u_all = u_ref[0].astype(F32)
    v_all = v_ref[0].astype(F32)
    bre, bim = [], []
    for h in range(u_ref.shape[2] // sub):
        js = slice(sub * h, sub * (h + 1))
        xu = u_all[:, js, :].reshape(n1 * sub, w).astype(BF16)
        xv = v_all[:, js, :].reshape(n1 * sub, w).astype(BF16)
        a = jnp.dot(m1_ref[...], xu, preferred_element_type=F32)
        a = a + jnp.dot(m2_ref[...], xv, preferred_element_type=F32)
        are, aim = a[0:n1 * sub], a[n1 * sub:]
        tc, ts = tc_ref[h], ts_ref[h]
        bre.append((are * tc + aim * ts).reshape(n1, sub, w))
        bim.append((aim * tc - are * ts).reshape(n1, sub, w))
    bre_ref[0] = jnp.concatenate(bre, axis=1).astype(bre_ref.dtype)
    bim_ref[0] = jnp.concatenate(bim, axis=1).astype(bim_ref.dtype)


def _fft2_kernel(g1_ref, g2_ref, bre_ref, bim_ref, o_ref):
    n2 = g1_ref.shape[1]
    for j in range(o_ref.shape[2]):
        rows = slice(n2 * j, n2 * (j + 1))
        x = jnp.dot(g1_ref[...], bre_ref[0, rows, :], preferred_element_type=F32)
        x = x + jnp.dot(g2_ref[...], bim_ref[0, rows, :], preferred_element_type=F32)
        o_ref[0, :, j, :] = x


def _fft_call(tabs, u, v):
    m1, m2, tc, ts, g1, g2 = tabs
    b_, s_, w = u.shape
    n1 = m1.shape[1] // FFT_SUB
    n2 = s_ // n1
    u4 = u.reshape(b_, n1, n2, w)
    v4 = v.reshape(b_, n1, n2, w)
    per_step = FFT_STEP * FFT_SUB
    tile = pl.BlockSpec((1, n1, per_step, w), lambda t, b: (b, 0, t, 0))
    tab = pl.BlockSpec((FFT_STEP, n1 * FFT_SUB, w), lambda t, b: (t, 0, 0))
    mat = pl.BlockSpec((2 * n1 * FFT_SUB, n1 * FFT_SUB), lambda t, b: (0, 0))
    bre, bim = pl.pallas_call(
        _fft1_kernel,
        grid=(n2 // per_step, b_),
        in_specs=[mat, mat, tab, tab, tile, tile],
        out_specs=(tile, tile),
        out_shape=(jax.ShapeDtypeStruct((b_, n1, n2, w), BF16),) * 2,
        compiler_params=_cparams(("parallel", "arbitrary")),
        name="fourier_fft1",
    )(m1, m2, tc, ts, u4, v4)
    blk = pl.BlockSpec((1, per_step * n2, w), lambda b, t: (b, t, 0))
    mat2 = pl.BlockSpec((n2, n2), lambda b, t: (0, 0))
    out = pl.pallas_call(
        _fft2_kernel,
        grid=(b_, n1 // per_step),
        in_specs=[mat2, mat2, blk, blk],
        out_specs=pl.BlockSpec((1, n2, per_step, w), lambda b, t: (b, 0, t, 0)),
        out_shape=jax.ShapeDtypeStruct((b_, n2, n1, w), F32),
        compiler_params=_cparams(("parallel", "parallel")),
        name="fourier_fft2",
    )(g1, g2, bre.reshape(b_, s_, w), bim.reshape(b_, s_, w))
    return out.reshape(b_, s_, w)


def _fft_tables(s_, n1, w):
    n2 = s_ // n1
    k1 = np.arange(n1)
    a1 = ((k1[:, None] * k1[None, :]) % n1).astype(np.float64) * (2.0 * math.pi / n1)
    c1, s1 = np.cos(a1), np.sin(a1)
    eye = np.eye(FFT_SUB)
    expand = lambda a: jnp.asarray(np.kron(a, eye), dtype=F32).astype(BF16)
    m1 = expand(np.concatenate([c1, -s1], axis=0))
    m2 = expand(np.concatenate([-s1, -c1], axis=0))
    s2 = jnp.arange(n2, dtype=jnp.int32)
    n_t = n2 // FFT_SUB
    row = jnp.arange(n1 * FFT_SUB, dtype=jnp.int32)
    s2_of = FFT_SUB * jnp.arange(n_t, dtype=jnp.int32)[:, None] + (row % FFT_SUB)[None, :]
    th = (((row // FFT_SUB)[None, :] * s2_of) % s_).astype(F32) * (2.0 * math.pi / s_)
    tc = jnp.broadcast_to(jnp.cos(th)[:, :, None], (n_t, n1 * FFT_SUB, w))
    ts = jnp.broadcast_to(jnp.sin(th)[:, :, None], (n_t, n1 * FFT_SUB, w))
    a2 = ((s2[:, None] * s2[None, :]) % n2).astype(F32) * (2.0 * math.pi / n2)
    scale = 1.0 / math.sqrt(s_)
    g1 = (jnp.cos(a2) * scale).astype(BF16)
    g2 = (jnp.sin(a2) * scale).astype(BF16)
    return m1, m2, tc, ts, g1, g2


def _mla_kernel(*refs, n_chunks, heads_per_step):
    hps = heads_per_step
    n_in = 6 if n_chunks else 4
    if n_chunks:
        zero_ref, qt_ref, k_ref, vt_ref, kc_ref, vct_ref = refs[:n_in]
    else:
        zero_ref, qt_ref, kc_ref, vct_ref = refs[:n_in]
    o_ref = refs[n_in]
    row0 = pl.multiple_of(zero_ref[0], MLA_TK)
    scratch = refs[n_in + 1:]
    tq = scratch[0].shape[1]
    n_sub = qt_ref.shape[2] // tq
    lc = kc_ref.shape[2]
    s_refs = [[scratch[MLA_S_SLOTS * (hps * u + hh):MLA_S_SLOTS * (hps * u + hh + 1)] for hh in range(hps)]
              for u in range(n_sub)]

    def keys(idx):
        if idx == 0:
            return (lambda hh: kc_ref[0, hh]), lc
        return (lambda hh: k_ref[0, hh, (idx - 1) * MLA_TK:idx * MLA_TK, :]), MLA_TK

    def values(idx):
        if idx == 0:
            return lambda hh: vct_ref[0, hh, 0]
        return lambda hh: vt_ref[0, hh, idx - 1]

    def logits(u, idx):
        ks, rows = keys(idx)
        cms = []
        for hh in range(hps):
            qt = qt_ref[0, MLA_PAD * hh:MLA_PAD * (hh + 1), tq * u:tq * (u + 1)]
            s = jnp.dot(ks(hh), qt, preferred_element_type=F32)
            s_refs[u][hh][idx % MLA_S_SLOTS][0:rows, :] = s
            cms.append(jnp.max(s, axis=0, keepdims=True))
        return tuple(cms)

    def accumulate(u, idx, carries, cms):
        _, rows = keys(idx)
        vts = values(idx)
        out = []
        for hh in range(hps):
            m, acc = carries[hh]
            m_new = jnp.maximum(m, cms[hh])
            alpha = jnp.exp2(m - m_new)
            s = s_refs[u][hh][idx % MLA_S_SLOTS][pl.ds(row0, rows), :]
            p = jnp.exp2(s - m_new).astype(BF16)
            acc = alpha * acc + jnp.dot(vts(hh), p, preferred_element_type=F32)
            out.append((m_new, acc))
        return tuple(out)

    init = (jnp.full((1, tq), NEG_INF, F32), jnp.zeros((MLA_VT_ROWS, tq), F32))
    carries = [tuple(init for _ in range(hps)) for _ in range(n_sub)]
    pending = [None] * n_sub
    n_idx = n_chunks + 1
    for t in range(n_idx + n_sub):
        for u in range(n_sub):
            idx = t - u
            new_cms = logits(u, idx) if 0 <= idx < n_idx else None
            if 1 <= idx <= n_idx:
                carries[u] = accumulate(u, idx - 1, carries[u], pending[u])
            pending[u] = new_cms
    for u in range(n_sub):
        o = jnp.concatenate([acc[0:MLA_V_DIM] / acc[MLA_V_DIM:MLA_V_DIM + 1] for (_, acc) in carries[u]],
                            axis=0)
        o_ref[0, tq * u:tq * (u + 1), :] = o.T.astype(o_ref.dtype)


def _mla_call(qt, k, vt, kc, vct, tq, n_sub):
    b_, _, sq = qt.shape
    h_ = kc.shape[1]
    lc = kc.shape[2]
    hps = 2
    n_chunks = 0 if k is None else k.shape[2] // MLA_TK
    assert n_chunks == 0 or lc <= MLA_TK
    in_specs = [pl.BlockSpec(memory_space=pltpu.SMEM),
                pl.BlockSpec((1, hps * MLA_PAD, tq), lambda b, p, i: (b, p, i))]
    args = [jnp.zeros((1,), jnp.int32), qt]
    if k is not None:
        sk = k.shape[2]
        in_specs += [pl.BlockSpec((1, hps, sk, MLA_PAD), lambda b, p, i: (b, p, 0, 0)),
                     pl.BlockSpec((1, hps, n_chunks, MLA_VT_ROWS, MLA_TK), lambda b, p, i: (b, p, 0, 0, 0))]
        args += [k, vt]
    in_specs += [pl.BlockSpec((1, hps, lc, MLA_PAD), lambda b, p, i: (b, p, 0, 0)),
                 pl.BlockSpec((1, hps, 1, MLA_VT_ROWS, lc), lambda b, p, i: (b, p, 0, 0, 0))]
    args += [kc, vct]
    return pl.pallas_call(
        functools.partial(_mla_kernel, n_chunks=n_chunks, heads_per_step=hps),
        grid=(b_, h_ // hps, sq // tq),
        in_specs=in_specs,
        out_specs=pl.BlockSpec((1, tq, hps * MLA_V_DIM), lambda b, p, i: (b, i, p)),
        out_shape=jax.ShapeDtypeStruct((b_, sq, h_ * MLA_V_DIM), BF16),
        scratch_shapes=[pltpu.VMEM((MLA_TK if n_chunks else lc, tq // n_sub), F32)] * (MLA_S_SLOTS * hps * n_sub),
        compiler_params=_cparams(("parallel", "parallel", "arbitrary")),
        name="mla_attn",
    )(*args)


def _swa_kernel(*refs, n_local, s_len):
    zero_ref, sink_ref, qt_ref = refs[:3]
    k_refs = refs[3:3 + n_local]
    vt_refs = refs[3 + n_local:3 + 2 * n_local]
    kc_ref, vct_ref, o_ref = refs[3 + 2 * n_local:6 + 2 * n_local]
    s_refs = refs[6 + 2 * n_local:]
    row0 = pl.multiple_of(zero_ref[0], BLOCK)
    tq = s_refs[0].shape[1]
    step_tq = qt_ref.shape[2]
    n_sub = step_tq // tq
    lc = kc_ref.shape[2]
    sub_blocks = tq // BLOCK + 2 if n_local else 0
    nk = sub_blocks * BLOCK
    sinks = [sink_ref[hh] * LOG2E for hh in range(SWA_HEADS)]

    def mask_bias(u):
        q0 = pl.program_id(1) * step_tq + u * tq
        kpos = q0 - BLOCK + lax.broadcasted_iota(jnp.int32, (nk, tq), 0)
        qpos = q0 + lax.broadcasted_iota(jnp.int32, (nk, tq), 1)
        valid = (jnp.abs(kpos - qpos) <= WINDOW) & (kpos >= 0) & (kpos < s_len)
        return jnp.where(valid, 0.0, NEG_INF).astype(F32)

    def keys(u, kh):
        kc = kc_ref[0, kh]
        vct = vct_ref[0, kh]
        if not n_local:
            return kc, vct
        first = u * (tq // BLOCK)
        blocks = range(first, first + sub_blocks)
        return (jnp.concatenate([kc] + [k_refs[j][0, kh] for j in blocks], axis=0),
                jnp.concatenate([vct] + [vt_refs[j][0, kh] for j in blocks], axis=1))

    def logits(unit, u, heads, bias):
        ms = []
        for k, hh in enumerate(heads):
            k_all = kv[(u, hh // SWA_GROUP)][0]
            qt = qt_ref[0, HEAD_DIM * hh:HEAD_DIM * (hh + 1), tq * u:tq * (u + 1)]
            s_ref = s_refs[SWA_UNIT_HEADS * (unit % 2) + k]
            s_c = jnp.dot(k_all[0:lc], qt, preferred_element_type=F32)
            s_ref[0:lc, :] = s_c
            m = jnp.maximum(jnp.max(s_c, axis=0, keepdims=True), sinks[hh])
            if n_local:
                s_l = jnp.dot(k_all[lc:], qt, preferred_element_type=F32) + bias
                s_ref[lc:lc + nk, :] = s_l
                m = jnp.maximum(m, jnp.max(s_l, axis=0, keepdims=True))
            ms.append(m)
        return ms

    def accumulate(unit, u, heads, ms):
        res = []
        for k, hh in enumerate(heads):
            vt_all = kv[(u, hh // SWA_GROUP)][1]
            s_ref = s_refs[SWA_UNIT_HEADS * (unit % 2) + k]
            p = jnp.exp2(s_ref[pl.ds(row0, lc + nk), :] - ms[k]).astype(BF16)
            acc = jnp.dot(vt_all, p, preferred_element_type=F32)
            l = acc[HEAD_DIM:HEAD_DIM + 1] + jnp.exp2(sinks[hh] - ms[k])
            res.append(acc[0:HEAD_DIM] / l)
        return res

    units = [(u, tuple(range(h0, h0 + SWA_UNIT_HEADS)))
             for u in range(n_sub) for h0 in range(0, SWA_HEADS, SWA_UNIT_HEADS)]
    kv = {(u, kh): keys(u, kh) for u in range(n_sub) for kh in range(SWA_KV_HEADS)}
    biases = [mask_bias(u) if n_local else None for u in range(n_sub)]
    outs = [[] for _ in range(n_sub)]
    m_prev = None
    for unit in range(len(units) + 1):
        m_next = None
        if unit < len(units):
            u, heads = units[unit]
            m_next = logits(unit, u, heads, biases[u])
        if unit > 0:
            u, heads = units[unit - 1]
            outs[u] += accumulate(unit - 1, u, heads, m_prev)
        m_prev = m_next
    for u in range(n_sub):
        o = jnp.concatenate(outs[u], axis=0)
        o_ref[0, tq * u:tq * (u + 1), :] = o.T.astype(o_ref.dtype)


def _swa_call(sink, qt, k, vt, kc, vct):
    b_, _, sq = qt.shape
    lc = kc.shape[2]
    local = k is not None
    sub_tq = SWA_TQ if local else sq
    tq = sub_tq * SWA_SUB_TILES if local else sq
    nb = sq // tq
    n_local = tq // BLOCK + 2 if local else 0
    sub_keys = lc + (sub_tq // BLOCK + 2) * BLOCK if local else lc
    in_specs = [pl.BlockSpec(memory_space=pltpu.SMEM), pl.BlockSpec(memory_space=pltpu.SMEM),
                pl.BlockSpec((1, SWA_Q_WIDTH, tq), lambda b, n: (b, 0, n))]
    args = [jnp.zeros((1,), jnp.int32), sink, qt]
    if local:
        nkb = sq // BLOCK
        per = tq // BLOCK

        def blk_idx(n, j):
            return jnp.clip(n * per - 1 + j, 0, nkb - 1)

        in_specs += [pl.BlockSpec((1, SWA_KV_HEADS, BLOCK, HEAD_DIM),
                                  functools.partial(lambda b, n, j: (b, 0, blk_idx(n, j), 0), j=j))
                     for j in range(n_local)]
        in_specs += [pl.BlockSpec((1, SWA_KV_HEADS, SWA_VT_ROWS, BLOCK),
                                  functools.partial(lambda b, n, j: (b, 0, 0, blk_idx(n, j)), j=j))
                     for j in range(n_local)]
        args += [k] * n_local + [vt] * n_local
    in_specs += [pl.BlockSpec((1, SWA_KV_HEADS, lc, HEAD_DIM), lambda b, n: (b, 0, 0, 0)),
                 pl.BlockSpec((1, SWA_KV_HEADS, SWA_VT_ROWS, lc), lambda b, n: (b, 0, 0, 0))]
    args += [kc, vct]
    return pl.pallas_call(
        functools.partial(_swa_kernel, n_local=n_local, s_len=sq),
        grid=(b_, nb),
        in_specs=in_specs,
        out_specs=pl.BlockSpec((1, tq, SWA_Q_WIDTH), lambda b, n: (b, n, 0)),
        out_shape=jax.ShapeDtypeStruct((b_, sq, SWA_Q_WIDTH), BF16),
        scratch_shapes=[pltpu.VMEM((sub_keys, sub_tq), F32)] * (2 * SWA_UNIT_HEADS),
        compiler_params=_cparams(("parallel", "arbitrary")),
        name="swa_attn",
    )(*args)


def _mlp_kernel(fo_ref, so_ref, mo_ref, h_ref, mod_ref, g2_ref, wo_ref, w1_ref, w2_ref, gf_ref, o_ref, *,
                final, ff_chunk):
    mod = mod_ref[0]
    g1 = mod[:, 2 * D_MODEL:3 * D_MODEL]
    sh2 = mod[:, 3 * D_MODEL:4 * D_MODEL]
    sc2 = mod[:, 4 * D_MODEL:5 * D_MODEL]
    g2 = mod[:, 5 * D_MODEL:6 * D_MODEL]
    mix = jnp.concatenate([fo_ref[0].astype(BF16), so_ref[0], mo_ref[0]], axis=-1)
    h = h_ref[0] + g1 * jnp.dot(mix, wo_ref[...], preferred_element_type=F32)
    y = (_rms(h, g2_ref[...]) * (1.0 + sc2) + sh2).astype(BF16)
    acc = None
    for c in range(D_FF // ff_chunk):
        a = jnp.dot(y, w1_ref[:, c * ff_chunk:(c + 1) * ff_chunk], preferred_element_type=F32)
        a = jnp.maximum(a, 0.0)
        a = (a * a).astype(BF16)
        part = jnp.dot(a, w2_ref[c * ff_chunk:(c + 1) * ff_chunk, :], preferred_element_type=F32)
        acc = part if acc is None else acc + part
    h = h + g2 * acc
    if final:
        h = _rms(h, gf_ref[...])
    o_ref[0] = h


def _mlp_call(fo, so, mo, h, mod3, mod_row, g2, wo, w1, w2, gf, tm, final):
    b_, s_, _ = h.shape
    const = lambda b, i: (0, 0)
    tile = lambda b, i: (b, i, 0)
    if mod_row is None:
        mod_map = lambda b, i: (b, 0, 0)
    else:
        mod_map = lambda b, i: (mod_row, 0, 0)
    once = pl.Buffered(1)
    return pl.pallas_call(
        functools.partial(_mlp_kernel, final=final, ff_chunk=1024),
        grid=(b_, s_ // tm),
        in_specs=[
            pl.BlockSpec((1, tm, FOURIER_WIDTH), tile),
            pl.BlockSpec((1, tm, SWA_Q_WIDTH), tile),
            pl.BlockSpec((1, tm, MLA_HEADS * MLA_V_DIM), tile),
            pl.BlockSpec((1, tm, D_MODEL), tile),
            pl.BlockSpec((1, 1, 6 * D_MODEL), mod_map),
            pl.BlockSpec((1, D_MODEL), const),
            pl.BlockSpec((D_MIX, D_MODEL), const, pipeline_mode=once),
            pl.BlockSpec((D_MODEL, D_FF), const, pipeline_mode=once),
            pl.BlockSpec((D_FF, D_MODEL), const, pipeline_mode=once),
            pl.BlockSpec((1, D_MODEL), const),
        ],
        out_specs=pl.BlockSpec((1, tm, D_MODEL), tile),
        out_shape=jax.ShapeDtypeStruct((b_, s_, D_MODEL), F32),
        compiler_params=_cparams(("parallel", "parallel")),
        name="out_proj_mlp",
    )(fo, so, mo, h, mod3, g2, wo, w1, w2, gf)


def _rot_cols(w, heads, dim):
    w4 = w.reshape(w.shape[0], heads, 2, dim // 2)
    return jnp.stack([-w4[:, :, 1], w4[:, :, 0]], axis=2).reshape(w.shape[0], heads * dim)


def _pad_lanes(w, heads, before, after):
    w3 = w.reshape(w.shape[0], heads, w.shape[1] // heads)
    w3 = jnp.pad(w3, ((0, 0), (0, 0), (before, after)))
    return w3.reshape(w.shape[0], -1)


def _pack_w_in(w_in):
    wk = w_in[:, OFF_SWA_K:OFF_SWA_V]
    wkr = w_in[:, OFF_MLA_KR:OFF_MLA_KR + MLA_ROPE_DIM]
    kr_tile = jnp.concatenate([jnp.zeros((D_MODEL, MLA_NOPE_DIM), F32), wkr, _rot_cols(wkr, 1, MLA_ROPE_DIM)],
                              axis=1)
    cols = [w_in[:, :OFF_SWA_Q], wk, w_in[:, OFF_MLA_CQ:OFF_MLA_KR], kr_tile,
            _rot_cols(wk, SWA_KV_HEADS, HEAD_DIM)]
    rows = w_in[:, OFF_SWA_Q:OFF_SWA_K], w_in[:, OFF_SWA_V:OFF_MLA_CQ]
    return jnp.concatenate(cols, axis=1).astype(BF16), jnp.concatenate(rows, axis=1).T.astype(BF16)


def _pack_w_uq(w_uq):
    per = MLA_NOPE_DIM + MLA_ROPE_DIM
    return _pad_lanes(w_uq, MLA_HEADS, 0, MLA_PAD - per).T.astype(BF16)


def _pack_w_ukv(w_ukv):
    w3 = w_ukv.reshape(MLA_KV_RANK, MLA_HEADS, MLA_NOPE_DIM + MLA_V_DIM)
    wk = w3[:, :, :MLA_NOPE_DIM].reshape(MLA_KV_RANK, MLA_HEADS * MLA_NOPE_DIM)
    wv = w3[:, :, MLA_NOPE_DIM:].reshape(MLA_KV_RANK, MLA_HEADS * MLA_V_DIM)
    return _pad_lanes(wk, MLA_HEADS, 0, MLA_PAD - MLA_NOPE_DIM).astype(BF16), wv.T.astype(BF16)


def _rope_table(rows):
    r, col = jnp.meshgrid(jnp.arange(rows, dtype=F32), jnp.arange(GRID_W, dtype=F32), indexing="ij")
    r = r.reshape(-1)
    col = col.reshape(-1)

    def tables(dim):
        n_freq = dim // 4
        inv = ROPE_THETA ** (-jnp.arange(n_freq, dtype=F32) / n_freq)
        ang = jnp.concatenate([r[:, None] * inv[None, :], col[:, None] * inv[None, :]], axis=-1)
        return jnp.cos(ang), jnp.sin(ang)

    ch, sh = tables(HEAD_DIM)
    cr, sr = tables(MLA_ROPE_DIM)
    mla_pad = ((0, 0), (MLA_NOPE_DIM, LANE - MLA_NOPE_DIM - MLA_ROPE_DIM))
    nope_ones = (jnp.arange(LANE) < MLA_NOPE_DIM).astype(F32)[None, :]
    return jnp.concatenate([jnp.tile(ch, (1, 4)), jnp.tile(sh, (1, 4)),
                            jnp.pad(jnp.tile(cr, (1, 2)), mla_pad) + nope_ones,
                            jnp.pad(jnp.tile(sr, (1, 2)), mla_pad)], axis=-1)


def _identity_rope_table(n):
    ones = jnp.ones((n, LANE), F32)
    zeros = jnp.zeros((n, LANE), F32)
    m = jnp.concatenate([jnp.ones((n, MLA_NOPE_DIM + MLA_ROPE_DIM), F32),
                         jnp.zeros((n, LANE - MLA_NOPE_DIM - MLA_ROPE_DIM), F32)], axis=-1)
    return jnp.concatenate([ones, zeros, m, zeros], axis=-1)


def _dft_tables(n):
    k = jnp.arange(n, dtype=jnp.int32)
    ks = (k[:, None] * k[None, :]) % n
    ang = ks.astype(F32) * (2.0 * math.pi / n)
    scale = 1.0 / math.sqrt(n)
    return (jnp.cos(ang) * scale).astype(BF16), (-jnp.sin(ang) * scale).astype(BF16)


def kernel(x, c, ctx, c_ctx, w_ada, b_ada, norm1_g, norm2_g, w_in, w_fourier, swa_sink, mla_q_norm, w_uq,
           mla_kv_norm, w_ukv, w_out, w_mlp1, w_mlp2, final_norm_g):
    b_, s_, _ = x.shape
    lc = ctx.shape[1]
    depth = w_ada.shape[0]
    rows = s_ // GRID_W

    mod_rows = 16
    cc = jnp.concatenate([c, c_ctx[None, :], jnp.zeros((mod_rows - b_ - 1, D_MODEL), F32)], axis=0)
    mod_all = _ada_call(cc, w_ada, b_ada)

    rope = _rope_table(rows)
    rope_c = _identity_rope_table(lc)
    to_feature_major = lambda t: jnp.concatenate([t[:, 2 * LANE:], t[:, :2 * LANE]], axis=1).T
    ropet = to_feature_major(rope)
    ropet_c = to_feature_major(rope_c)
    fft_tabs = _fft_tables(s_, GRID_W, FOURIER_WIDTH)
    ctab_c, stab_c = _dft_tables(lc)
    kk = jnp.arange(FOURIER_GROUP_DIM, dtype=jnp.int32)
    ang64 = ((kk[:, None] * kk[None, :]) % FOURIER_GROUP_DIM).astype(F32) * (2.0 * math.pi / FOURIER_GROUP_DIM)
    c64, s64 = jnp.cos(ang64), jnp.sin(ang64)

    h, hc = x, ctx
    for l in range(depth):
        last = l == depth - 1
        mod3 = mod_all[l].reshape(mod_rows, 1, 6 * D_MODEL)
        g1 = norm1_g[l].reshape(1, D_MODEL)
        g2 = norm2_g[l].reshape(1, D_MODEL)
        gf = final_norm_g.reshape(1, D_MODEL)
        w1, w1t = _pack_w_in(w_in[l])
        wuq = _pack_w_uq(w_uq[l])
        wukvk, wukvv = _pack_w_ukv(w_ukv[l])
        wcs = _fw_call(c64, s64, w_fourier[l]).astype(BF16)
        qn = mla_q_norm[l].reshape(1, MLA_Q_RANK)
        kvn = mla_kv_norm[l].reshape(1, MLA_KV_RANK)
        wo = w_out[l].astype(BF16)
        wm1 = w_mlp1[l].astype(BF16)
        wm2 = w_mlp2[l].astype(BF16)
        sink = swa_sink[l]

        fu, fv, sq, sk, sv, mq, mk, mv = _proj_call(h, mod3, None, g1, w1, w1t, rope, ropet, wcs, qn, wuq,
                                                    kvn, wukvk, wukvv, tm=1024)
        fuc, fvc, sqc, skc, svc, mqc, mkc, mvc = _proj_call(hc, mod3, b_, g1, w1, w1t, rope_c, ropet_c, wcs, qn,
                                                            wuq, kvn, wukvk, wukvv, tm=lc)

        fo = _fft_call(fft_tabs, fu, fv)
        so = _swa_call(sink, sq, sk, sv, skc, svc)
        mo = _mla_call(mq, mk, mv, mkc, mvc, tq=2048, n_sub=4)
        h = _mlp_call(fo, so, mo, h, mod3, None, g2, wo, wm1, wm2, gf, tm=512, final=last)

        if not last:
            foc = _dft_call(ctab_c, stab_c, fuc, fvc, tk=lc)
            soc = _swa_call(sink, sqc, None, None, skc, svc)
            moc = _mla_call(mqc, None, None, mkc, mvc, tq=lc, n_sub=1)
            flat = lambda t: t.reshape(1, b_ * lc, t.shape[-1])
            hc = _mlp_call(flat(foc), flat(soc), flat(moc), flat(hc), mod3, b_, g2, wo, wm1, wm2, gf,
                           tm=512, final=False).reshape(b_, lc, D_MODEL)
    return h
```
